```python
import math
import jax, jax.numpy as jnp
from jax import lax
import numpy as np

D_MODEL = 2048
BATCH = 4
SEQ = 2048
DEPTH = 1

N_MEM = 256
Q_BLOCK = 128
LN_EPS = 1e-5
DIFF_HEADS = 8
DIFF_HEAD_DIM = 128
DIFF_V_DIM = 2 * DIFF_HEAD_DIM
DSA_HEADS = 8
DSA_HEAD_DIM = 128
DSA_KV_RANK = 512
IDX_HEADS = 16
IDX_HEAD_DIM = 64
DSA_TOPK_MAX = 256
REL_BUCKETS = 32
REL_MAX_DIST = 128
XATTN_HEADS = 4
XATTN_HEAD_DIM = 128
N_EXPERTS = 64
EXPERT_DIM = 512
TOP_K = 8
N_GROUPS = 8
TOPK_GROUPS = 4
ROUTED_SCALE = 2.5
SHARED_DIM = 512
MOE_BLOCK = 128
DN_ALPHA = (2 * DEPTH) ** 0.25
DN_BETA = (8 * DEPTH) ** -0.25
IN_SIZES = (DIFF_HEADS * 2 * DIFF_HEAD_DIM, DIFF_HEADS * 2 * DIFF_HEAD_DIM, DIFF_HEADS * DIFF_V_DIM,
            DSA_HEADS * DSA_HEAD_DIM, DSA_KV_RANK, IDX_HEADS * IDX_HEAD_DIM, IDX_HEAD_DIM, IDX_HEADS,
            D_MODEL, D_MODEL)
IN_COLS = sum(IN_SIZES)

kernel_name = 'hybrid_diffattn_dsa_moe_deepnorm'


def layer_norm(x, g, b):
    xf = x.astype(jnp.float32)
    mu = jnp.mean(xf, axis=-1, keepdims=True)
    var = jnp.mean(jnp.square(xf - mu), axis=-1, keepdims=True)
    return ((xf - mu) * lax.rsqrt(var + LN_EPS)).astype(x.dtype) * g + b


def rms_norm(x, g):
    xf = x.astype(jnp.float32)
    return (xf * lax.rsqrt(jnp.mean(xf * xf, axis=-1, keepdims=True) + LN_EPS)).astype(x.dtype) * g


def split_cols(a, sizes):
    outs, start = [], 0
    for s in sizes:
        outs.append(a[..., start:start + s])
        start += s
    return outs


def to_blocks(a):
    b, t = a.shape[:2]
    return jnp.moveaxis(a.reshape(b, t // Q_BLOCK, Q_BLOCK, *a.shape[2:]), 1, 0)


def from_blocks(a):
    nb, b, qb = a.shape[:3]
    return jnp.moveaxis(a, 0, 1).reshape(b, nb * qb, *a.shape[3:])


def rel_bucket(rel):
    n = jnp.maximum(rel, 0)
    max_exact = REL_BUCKETS // 2
    nf = jnp.maximum(n, 1).astype(jnp.float32)
    large = max_exact + (jnp.log(nf / max_exact) / math.log(REL_MAX_DIST / max_exact)
                         * (REL_BUCKETS - max_exact)).astype(jnp.int32)
    large = jnp.minimum(large, REL_BUCKETS - 1)
    return jnp.where(n < max_exact, n, large)


def differential_attention(q, k, v, positions, table, lam):
    scale = DIFF_HEAD_DIM ** -0.5

    def block(args):
        qb, pq = args
        rel = pq[:, :, None] - positions[:, None, :]
        bias = jnp.moveaxis(table[rel_bucket(rel)], -1, 1)
        s = jnp.einsum('bqhcd,bkhcd->bchqk', qb, k) * scale + bias[:, None]
        s = jnp.where((rel >= 0)[:, None, None], s.astype(jnp.float32), -jnp.inf)
        p = jax.nn.softmax(s, axis=-1)
        p = (p[:, 0] - lam * p[:, 1]).astype(v.dtype)
        return jnp.einsum('bhqk,bkhe->bqhe', p, v)

    return from_blocks(lax.map(block, (to_blocks(q), to_blocks(positions))))


def dsa_attention(q, c, q_idx, k_idx, w_idx, positions, table, w_uk, w_uv, k_sel):
    scale = DSA_HEAD_DIM ** -0.5
    w_idx = w_idx * (IDX_HEADS ** -0.5 * IDX_HEAD_DIM ** -0.5)

    def block(args):
        qb, qib, wb, pq = args
        rel = pq[:, :, None] - positions[:, None, :]
        idx_logits = jax.nn.relu(jnp.einsum('bqhd,bkd->bqhk', qib, k_idx))
        score = jnp.einsum('bqhk,bqh->bqk', idx_logits, wb).astype(jnp.float32)
        score = jnp.where(rel >= 0, score, -jnp.inf)
        _, sel = lax.top_k(score, k_sel)
        c_sel = jax.vmap(lambda cb, ib: cb[ib])(c, sel)
        rel_sel = jnp.take_along_axis(rel, sel, axis=-1)
        bias = jnp.moveaxis(table[rel_bucket(rel_sel)], -1, 2)
        q_lat = jnp.einsum('bqhd,hdr->bqhr', qb, w_uk)
        s = jnp.einsum('bqhr,bqkr->bqhk', q_lat, c_sel) * scale + bias
        s = jnp.where((rel_sel >= 0)[:, :, None, :], s.astype(jnp.float32), -jnp.inf)
        p = jax.nn.softmax(s, axis=-1).astype(c.dtype)
        o_lat = jnp.einsum('bqhk,bqkr->bqhr', p, c_sel)
        return jnp.einsum('bqhr,hrd->bqhd', o_lat, w_uv)

    xs = (to_blocks(q), to_blocks(q_idx), to_blocks(w_idx), to_blocks(positions))
    return from_blocks(lax.map(block, xs))


def hybrid_mixer(h, positions, rel_table, w_in, diff_lambda, diff_subln, idx_k_g, idx_k_b,
                 kv_norm_g, w_uk, w_uv, w_proj_diff, w_proj_dsa, w_mix_out, lambda_init):
    b, t, _ = h.shape
    proj = h @ w_in
    q_d, k_d, v_d, q_s, c_kv, q_i, k_i, w_i, g_d, g_s = split_cols(proj, IN_SIZES)

    q_d = q_d.reshape(b, t, DIFF_HEADS, 2, DIFF_HEAD_DIM)
    k_d = k_d.reshape(b, t, DIFF_HEADS, 2, DIFF_HEAD_DIM)
    v_d = v_d.reshape(b, t, DIFF_HEADS, DIFF_V_DIM)
    lq1, lk1, lq2, lk2 = [diff_lambda[i].astype(jnp.float32) for i in range(4)]
    lam = jnp.exp(jnp.sum(lq1 * lk1)) - jnp.exp(jnp.sum(lq2 * lk2)) + lambda_init
    o_d = differential_attention(q_d, k_d, v_d, positions, rel_table[:, :DIFF_HEADS], lam)
    o_d = (rms_norm(o_d, diff_subln) * (1.0 - lambda_init)).reshape(b, t, DIFF_HEADS * DIFF_V_DIM)

    k_sel = min(DSA_TOPK_MAX, t // 4)
    q_s = q_s.reshape(b, t, DSA_HEADS, DSA_HEAD_DIM)
    c = rms_norm(c_kv, kv_norm_g)
    q_i = q_i.reshape(b, t, IDX_HEADS, IDX_HEAD_DIM)
    k_i = layer_norm(k_i, idx_k_g, idx_k_b)
    o_s = dsa_attention(q_s, c, q_i, k_i, w_i, positions, rel_table[:, DIFF_HEADS:], w_uk, w_uv, k_sel)
    o_s = o_s.reshape(b, t, DSA_HEADS * DSA_HEAD_DIM)

    merged = jax.nn.sigmoid(g_d) * (o_d @ w_proj_diff) + jax.nn.sigmoid(g_s) * (o_s @ w_proj_dsa)
    return merged @ w_mix_out


def memory_cross_attention(h, mem, wq, wkv, wo):
    b, t, _ = h.shape
    q = (h @ wq).reshape(b, t, XATTN_HEADS, XATTN_HEAD_DIM)
    kv = (mem @ wkv).reshape(b, mem.shape[1], 2, XATTN_HEADS, XATTN_HEAD_DIM)
    k, v = kv[:, :, 0], kv[:, :, 1]
    s = jnp.einsum('bqhd,bkhd->bhqk', q, k) * (XATTN_HEAD_DIM ** -0.5)
    p = jax.nn.softmax(s.astype(jnp.float32), axis=-1).astype(v.dtype)
    o = jnp.einsum('bhqk,bkhd->bqhd', p, v).reshape(b, t, XATTN_HEADS * XATTN_HEAD_DIM)
    return o @ wo


def moe_ffn(h, router_w, router_bias, w1, w3, w2, sw1, sw3, sw2):
    b, t, d = h.shape
    n = b * t
    hf = h.reshape(n, d)
    scores = jax.nn.sigmoid((hf @ router_w).astype(jnp.float32))
    sel = scores + router_bias.astype(jnp.float32)
    grp = sel.reshape(n, N_GROUPS, N_EXPERTS // N_GROUPS)
    grp_score = jnp.sum(lax.top_k(grp, 2)[0], axis=-1)
    _, top_g = lax.top_k(grp_score, TOPK_GROUPS)
    g_mask = jnp.sum(jax.nn.one_hot(top_g, N_GROUPS, dtype=jnp.float32), axis=1) > 0
    e_mask = jnp.repeat(g_mask, N_EXPERTS // N_GROUPS, axis=1)
    _, idx = lax.top_k(jnp.where(e_mask, sel, -jnp.inf), TOP_K)
    gate = jnp.take_along_axis(scores, idx, axis=-1)
    gate = gate / jnp.sum(gate, axis=-1, keepdims=True) * ROUTED_SCALE

    flat_e = idx.reshape(-1)
    flat_tok = jnp.repeat(jnp.arange(n, dtype=jnp.int32), TOP_K)
    flat_g = gate.reshape(-1).astype(h.dtype)
    order = jnp.argsort(flat_e)
    se, stok, sg = flat_e[order], flat_tok[order], flat_g[order]
    counts = jnp.zeros((N_EXPERTS,), jnp.int32).at[flat_e].add(1)
    padded = (counts + MOE_BLOCK - 1) // MOE_BLOCK * MOE_BLOCK
    pad_end = jnp.cumsum(padded)
    pad_start = pad_end - padded
    start = jnp.cumsum(counts) - counts
    dest = pad_start[se] + (jnp.arange(n * TOP_K, dtype=jnp.int32) - start[se])
    n_blocks = (n * TOP_K + N_EXPERTS * (MOE_BLOCK - 1) + MOE_BLOCK - 1) // MOE_BLOCK
    rows = n_blocks * MOE_BLOCK
    row_tok = jnp.full((rows,), n, jnp.int32).at[dest].set(stok)
    row_g = jnp.zeros((rows,), h.dtype).at[dest].set(sg)
    block_e = jnp.clip(jnp.searchsorted(pad_end, jnp.arange(n_blocks, dtype=jnp.int32) * MOE_BLOCK,
                                        side='right'), 0, N_EXPERTS - 1)
    h_pad = jnp.concatenate([hf, jnp.zeros((1, d), h.dtype)], axis=0)

    def expert_block(args):
        tok, e = args
        xb = h_pad[tok]
        return (jax.nn.silu(xb @ w1[e]) * (xb @ w3[e])) @ w2[e]

    y_rows = lax.map(expert_block, (row_tok.reshape(n_blocks, MOE_BLOCK), block_e))
    y_rows = y_rows.reshape(rows, d) * row_g[:, None]
    routed = jax.ops.segment_sum(y_rows, row_tok, num_segments=n + 1)[:n]
    shared = (jax.nn.silu(hf @ sw1) * (hf @ sw3)) @ sw2
    return (routed + shared).reshape(b, t, d)


def setup_inputs(seed: int = 0) -> dict:
    key = jax.random.key(seed)
    ks = iter(jax.random.split(key, 40))
    f32 = jnp.float32

    def nrm(shape, scale):
        return jax.random.normal(next(ks), shape, f32) * scale

    def gain(shape):
        return 1.0 + nrm(shape, 0.02)

    L, D = DEPTH, D_MODEL
    x = nrm((BATCH, SEQ, D), 1.0)
    mem = nrm((BATCH, N_MEM, D), 1.0)
    start = jax.random.randint(next(ks), (BATCH, 1), 0, 1024, dtype=jnp.int32)
    positions = start + jnp.arange(SEQ, dtype=jnp.int32)[None, :]
    return {
        'x': x,
        'mem': mem,
        'positions': positions,
        'rel_bias_table': nrm((REL_BUCKETS, DIFF_HEADS + DSA_HEADS), 0.2),
        'w_in': nrm((L, D, IN_COLS), D ** -0.5),
        'diff_lambda': nrm((L, 4, DIFF_HEAD_DIM), 0.1),
        'diff_subln': gain((L, DIFF_V_DIM)),
        'idx_k_g': gain((L, IDX_HEAD_DIM)),
        'idx_k_b': nrm((L, IDX_HEAD_DIM), 0.02),
        'kv_norm_g': gain((L, DSA_KV_RANK)),
        'w_uk': nrm((L, DSA_HEADS, DSA_HEAD_DIM, DSA_KV_RANK), DSA_HEAD_DIM ** -0.5),
        'w_uv': nrm((L, DSA_HEADS, DSA_KV_RANK, DSA_HEAD_DIM), DSA_KV_RANK ** -0.5),
        'w_proj_diff': nrm((L, DIFF_HEADS * DIFF_V_DIM, D), (DIFF_HEADS * DIFF_V_DIM) ** -0.5 * DN_BETA),
        'w_proj_dsa': nrm((L, DSA_HEADS * DSA_HEAD_DIM, D), (DSA_HEADS * DSA_HEAD_DIM) ** -0.5 * DN_BETA),
        'w_mix_out': nrm((L, D, D), D ** -0.5 * DN_BETA),
        'ln1_g': gain((L, D)),
        'ln1_b': nrm((L, D), 0.02),
        'xattn_wq': nrm((L, D, XATTN_HEADS * XATTN_HEAD_DIM), D ** -0.5),
        'xattn_wkv': nrm((L, D, 2 * XATTN_HEADS * XATTN_HEAD_DIM), D ** -0.5),
        'xattn_wo': nrm((L, XATTN_HEADS * XATTN_HEAD_DIM, D), (XATTN_HEADS * XATTN_HEAD_DIM) ** -0.5 * DN_BETA),
        'ln2_g': gain((L, D)),
        'ln2_b': nrm((L, D), 0.02),
        'router_w': nrm((L, D, N_EXPERTS), D ** -0.5),
        'router_bias': nrm((L, N_EXPERTS), 0.01),
        'exp_w1': nrm((L, N_EXPERTS, D, EXPERT_DIM), D ** -0.5),
        'exp_w3': nrm((L, N_EXPERTS, D, EXPERT_DIM), D ** -0.5),
        'exp_w2': nrm((L, N_EXPERTS, EXPERT_DIM, D), EXPERT_DIM ** -0.5 * DN_BETA),
        'sh_w1': nrm((L, D, SHARED_DIM), D ** -0.5),
        'sh_w3': nrm((L, D, SHARED_DIM), D ** -0.5),
        'sh_w2': nrm((L, SHARED_DIM, D), SHARED_DIM ** -0.5 * DN_BETA),
        'ln3_g': gain((L, D)),
        'ln3_b': nrm((L, D), 0.02),
    }


def reference(x, mem, positions, rel_bias_table, w_in, diff_lambda, diff_subln, idx_k_g, idx_k_b,
              kv_norm_g, w_uk, w_uv, w_proj_diff, w_proj_dsa, w_mix_out, ln1_g, ln1_b,
              xattn_wq, xattn_wkv, xattn_wo, ln2_g, ln2_b, router_w, router_bias,
              exp_w1, exp_w3, exp_w2, sh_w1, sh_w3, sh_w2, ln3_g, ln3_b):
    h = x
    for l in range(DEPTH):
        lambda_init = 0.8 - 0.6 * math.exp(-0.3 * l)
        y = hybrid_mixer(h, positions, rel_bias_table, w_in[l], diff_lambda[l], diff_subln[l],
                         idx_k_g[l], idx_k_b[l], kv_norm_g[l], w_uk[l], w_uv[l],
                         w_proj_diff[l], w_proj_dsa[l], w_mix_out[l], lambda_init)
        h = layer_norm(DN_ALPHA * h + y, ln1_g[l], ln1_b[l])
        y = memory_cross_attention(h, mem, xattn_wq[l], xattn_wkv[l], xattn_wo[l])
        h = layer_norm(DN_ALPHA * h + y, ln2_g[l], ln2_b[l])
        y = moe_ffn(h, router_w[l], router_bias[l], exp_w1[l], exp_w3[l], exp_w2[l],
                    sh_w1[l], sh_w3[l], sh_w2[l])
        h = layer_norm(DN_ALPHA * h + y, ln3_g[l], ln3_b[l])
    return h
```

```python
import functools
import math

import jax
import jax.numpy as jnp
from jax import lax
from jax.experimental import pallas as pl
from jax.experimental.pallas import tpu as pltpu

F32 = jnp.float32
BF16 = jnp.bfloat16
NEG = -1e30
INT_MIN = -(2 ** 31)
MIB = 1024 * 1024

LN_EPS = 1e-5
DIFF_HEADS = 8
DIFF_HEAD_DIM = 128
DIFF_V_DIM = 2 * DIFF_HEAD_DIM
DSA_HEADS = 8
DSA_HEAD_DIM = 128
DSA_KV_RANK = 512
IDX_HEADS = 16
IDX_HEAD_DIM = 64
DSA_TOPK_MAX = 256
REL_BUCKETS = 32
REL_MAX_DIST = 128
XATTN_HEADS = 4
XATTN_HEAD_DIM = 128
N_EXPERTS = 64
EXPERT_DIM = 512
TOP_K = 8
N_GROUPS = 8
TOPK_GROUPS = 4
ROUTED_SCALE = 2.5
DEPTH = 1
DN_ALPHA = (2 * DEPTH) ** 0.25

ATT_BLOCK = 256
MOE_ROWS = 256
NT_DIMS = (((1,), (1,)), ((), ()))


def _params(n_grid, vmem_mib):
    return pltpu.CompilerParams(dimension_semantics=("arbitrary",) * n_grid,
                                vmem_limit_bytes=vmem_mib * MIB)


def _sigmoid(x):
    return 1.0 / (1.0 + jnp.exp(-x))


def _layer_norm(z, g, b):
    mu = jnp.mean(z, axis=1, keepdims=True)
    zc = z - mu
    var = jnp.mean(zc * zc, axis=1, keepdims=True)
    return zc * lax.rsqrt(var + LN_EPS) * g + b


def _mm_kernel(a_ref, b_ref, o_ref, bq_ref):
    @pl.when(pl.program_id(1) == 0)
    def _():
        bq_ref[...] = b_ref[...].astype(BF16)

    o_ref[...] = jnp.dot(a_ref[...], bq_ref[...], preferred_element_type=F32).astype(o_ref.dtype)


def _matmul(a, b, col_start, n_cols, out_dtype, tm, tn):
    m, k = a.shape
    tm = min(tm, m)
    assert col_start % tn == 0 and n_cols % tn == 0 and m % tm == 0
    off = col_start // tn
    return pl.pallas_call(
        _mm_kernel,
        grid=(n_cols // tn, m // tm),
        in_specs=[pl.BlockSpec((tm, k), lambda j, i: (i, 0)),
                  pl.BlockSpec((k, tn), lambda j, i: (0, j + off))],
        out_specs=pl.BlockSpec((tm, tn), lambda j, i: (i, j)),
        out_shape=jax.ShapeDtypeStruct((m, n_cols), out_dtype),
        scratch_shapes=[pltpu.VMEM((k, tn), BF16)],
        compiler_params=_params(2, 48),
    )(a, b)


def _rel_bucket(n):
    n = jnp.maximum(n, 0)
    max_exact = REL_BUCKETS // 2
    nf = jnp.maximum(n, 1).astype(F32)
    large = max_exact + (jnp.log(nf / max_exact) / math.log(REL_MAX_DIST / max_exact)
                         * (REL_BUCKETS - max_exact)).astype(jnp.int32)
    large = jnp.minimum(large, REL_BUCKETS - 1)
    return jnp.where(n < max_exact, n, large)


def _rel_bias_tiles(table, blk):
    assert blk >= REL_MAX_DIST
    r = jnp.arange(blk, dtype=jnp.int32)[:, None]
    c = jnp.arange(blk, dtype=jnp.int32)[None, :]
    tiles = []
    for back in range(3):
        n = back * blk + r - c
        t = jnp.where((n >= 0)[..., None], table[_rel_bucket(n)], NEG)
        tiles.append(jnp.moveaxis(t, -1, 0))
    return jnp.stack(tiles, axis=1).astype(F32)


def _online_softmax_step(s, v, m_ref, l_ref, acc_ref):
    tk = s.shape[1]
    e = v.shape[1]
    m_prev = m_ref[...]
    m_next = jnp.maximum(m_prev, jnp.max(s, axis=1, keepdims=True))
    p = jnp.exp(s - pltpu.repeat(m_next, tk // 128, axis=1))
    alpha = jnp.exp(m_prev - m_next)
    l_ref[...] = alpha * l_ref[...] + jnp.sum(p, axis=1, keepdims=True)
    m_ref[...] = m_next
    acc_ref[...] = (acc_ref[...] * pltpu.repeat(alpha, e // 128, axis=1)
                    + jnp.dot(p.astype(BF16), v, preferred_element_type=F32))


def _diff_attn_kernel(q_ref, k_ref, v_ref, bias_ref, lam_ref, g_ref, o_ref, m_scr, l_scr, acc_scr,
                      *, blk, lambda_init):
    i = pl.program_id(2)
    d = DIFF_HEAD_DIM
    scale = d ** -0.5
    m_scr[...] = jnp.full(m_scr.shape, NEG, F32)
    l_scr[...] = jnp.zeros(l_scr.shape, F32)
    acc_scr[...] = jnp.zeros(acc_scr.shape, F32)
    q = q_ref[...]

    def body(j, carry):
        start = pl.multiple_of(j * blk, blk)
        kk = k_ref[pl.ds(start, blk), :]
        vv = v_ref[pl.ds(start, blk), :]
        bt = bias_ref[jnp.minimum(i - j, 2)]
        for c in range(2):
            s = lax.dot_general(q[:, c * d:(c + 1) * d], kk[:, c * d:(c + 1) * d], NT_DIMS,
                                preferred_element_type=F32)
            _online_softmax_step(s * scale + bt, vv, m_scr.at[c], l_scr.at[c], acc_scr.at[c])
        return carry

    lax.fori_loop(0, i + 1, body, 0)

    lp = lam_ref[...]
    lam = (jnp.exp(jnp.sum(lp[0:1] * lp[1:2], axis=1, keepdims=True))
           - jnp.exp(jnp.sum(lp[2:3] * lp[3:4], axis=1, keepdims=True)) + lambda_init)
    o0 = acc_scr[0] * pltpu.repeat(1.0 / l_scr[0], 2, axis=1)
    o1 = acc_scr[1] * pltpu.repeat(1.0 / l_scr[1], 2, axis=1)
    o = o0 - lam * o1
    ms = jnp.mean(o * o, axis=1, keepdims=True)
    o_ref[...] = (o * lax.rsqrt(ms + LN_EPS) * g_ref[...] * (1.0 - lambda_init)).astype(o_ref.dtype)


def _diff_attention(qkv, bias_tiles, diff_lambda, subln, batch, seq, lambda_init):
    blk = min(ATT_BLOCK, seq)
    nq = seq // blk
    h2d = DIFF_V_DIM
    kern = functools.partial(_diff_attn_kernel, blk=blk, lambda_init=lambda_init)
    return pl.pallas_call(
        kern,
        grid=(batch, DIFF_HEADS, nq),
        in_specs=[pl.BlockSpec((blk, h2d), lambda b, h, i: (b * nq + i, h)),
                  pl.BlockSpec((seq, h2d), lambda b, h, i: (b, DIFF_HEADS + h)),
                  pl.BlockSpec((seq, h2d), lambda b, h, i: (b, 2 * DIFF_HEADS + h)),
                  pl.BlockSpec((None, 3, blk, blk), lambda b, h, i: (h, 0, 0, 0)),
                  pl.BlockSpec((4, DIFF_HEAD_DIM), lambda b, h, i: (0, 0)),
                  pl.BlockSpec((1, h2d), lambda b, h, i: (0, 0))],
        out_specs=pl.BlockSpec((blk, h2d), lambda b, h, i: (b * nq + i, h)),
        out_shape=jax.ShapeDtypeStruct((batch * seq, DIFF_HEADS * h2d), BF16),
        scratch_shapes=[pltpu.VMEM((2, blk, 128), F32), pltpu.VMEM((2, blk, 128), F32),
                        pltpu.VMEM((2, blk, h2d), F32)],
        compiler_params=_params(3, 32),
    )(qkv, qkv, qkv, bias_tiles, diff_lambda, subln.reshape(1, h2d))


def _indexer_kernel(qi_ref, kfull_ref, wblk_ref, g_ref, b_ref, o_ref, ki_scr, key_scr, *, blk, seq, k_sel):
    i = pl.program_id(1)

    @pl.when(i == 0)
    def _():
        kr = kfull_ref[:, 0:IDX_HEAD_DIM]
        ki_scr[...] = _layer_norm(kr, g_ref[...], b_ref[...]).astype(BF16)

    w = wblk_ref[:, IDX_HEAD_DIM:IDX_HEAD_DIM + IDX_HEADS] * (IDX_HEADS ** -0.5 * IDX_HEAD_DIM ** -0.5)
    qi = qi_ref[...].astype(BF16)
    ki = ki_scr[...]
    score = jnp.zeros((blk, seq), F32)
    for h in range(IDX_HEADS):
        lg = lax.dot_general(qi[:, h * IDX_HEAD_DIM:(h + 1) * IDX_HEAD_DIM], ki, NT_DIMS,
                             preferred_element_type=F32)
        score = score + jnp.maximum(lg, 0.0) * w[:, h:h + 1]

    row = i * blk + lax.broadcasted_iota(jnp.int32, (blk, seq), 0)
    col = lax.broadcasted_iota(jnp.int32, (blk, seq), 1)
    valid = col <= row
    bits = lax.bitcast_convert_type(score, jnp.int32)
    key = bits ^ ((bits >> 31) & jnp.int32(0x7FFFFFFF))
    key_scr[...] = jnp.where(valid, key, INT_MIN)

    def search(it, thr):
        trial = thr + lax.shift_left(jnp.int32(1), 31 - it)
        cnt = jnp.sum(jnp.where(key_scr[...] >= trial, 1.0, 0.0), axis=1, keepdims=True)
        return jnp.where(cnt >= k_sel, trial, thr)

    thr = lax.fori_loop(0, 32, search, jnp.full((blk, 1), INT_MIN, jnp.int32))
    mask = jnp.where(valid & (key_scr[...] >= thr), 0.0, NEG).astype(o_ref.dtype)
    for jj in range(seq // blk):
        o_ref[jj] = mask[:, jj * blk:(jj + 1) * blk]


def _indexer_mask(qi, kiw, idx_k_g, idx_k_b, batch, seq, k_sel, qi_col_block):
    blk = min(ATT_BLOCK, seq)
    nq = seq // blk
    qw = IDX_HEADS * IDX_HEAD_DIM
    kern = functools.partial(_indexer_kernel, blk=blk, seq=seq, k_sel=k_sel)
    return pl.pallas_call(
        kern,
        grid=(batch, nq),
        in_specs=[pl.BlockSpec((blk, qw), lambda b, i: (b * nq + i, qi_col_block)),
                  pl.BlockSpec((seq, 128), lambda b, i: (b, 0)),
                  pl.BlockSpec((blk, 128), lambda b, i: (b * nq + i, 0)),
                  pl.BlockSpec((1, IDX_HEAD_DIM), lambda b, i: (0, 0)),
                  pl.BlockSpec((1, IDX_HEAD_DIM), lambda b, i: (0, 0))],
        out_specs=pl.BlockSpec((None, nq, blk, blk), lambda b, i: (b, 0, i, 0)),
        out_shape=jax.ShapeDtypeStruct((batch, nq, seq, blk), BF16),
        scratch_shapes=[pltpu.VMEM((seq, IDX_HEAD_DIM), BF16), pltpu.VMEM((blk, seq), jnp.int32)],
        compiler_params=_params(2, 48),
    )(qi, kiw, kiw, idx_k_g.reshape(1, -1), idx_k_b.reshape(1, -1))


def _dsa_attn_kernel(qs_ref, ckv_ref, g_ref, sel_ref, wuk_ref, wuv_ref, bias_ref, o_ref,
                     c_scr, ql_scr, sb_scr, m_scr, l_scr, acc_scr, *, blk):
    i = pl.program_id(1)
    d = DSA_HEAD_DIM
    scale = d ** -0.5

    @pl.when(i == 0)
    def _():
        x = ckv_ref[...]
        c = x * lax.rsqrt(jnp.mean(x * x, axis=1, keepdims=True) + LN_EPS) * g_ref[...]
        c_scr[...] = c.astype(BF16)

    qs = qs_ref[...]
    for h in range(DSA_HEADS):
        ql = jnp.dot(qs[:, h * d:(h + 1) * d], wuk_ref[h], preferred_element_type=F32)
        ql_scr[h * blk:(h + 1) * blk, :] = (ql * scale).astype(BF16)

    m_scr[...] = jnp.full(m_scr.shape, NEG, F32)
    l_scr[...] = jnp.zeros(l_scr.shape, F32)
    acc_scr[...] = jnp.zeros(acc_scr.shape, F32)

    def body(j, carry):
        start = pl.multiple_of(j * blk, blk)
        cc = c_scr[pl.ds(start, blk), :]
        s = lax.dot_general(ql_scr[...], cc, NT_DIMS, preferred_element_type=F32)
        selm = sel_ref[j].astype(F32)
        back = jnp.minimum(i - j, 2)
        for h in range(DSA_HEADS):
            sb_scr[h * blk:(h + 1) * blk, :] = bias_ref[h, back] + selm
        _online_softmax_step(s + sb_scr[...], cc, m_scr, l_scr, acc_scr)
        return carry

    lax.fori_loop(0, i + 1, body, 0)

    o_lat = (acc_scr[...] * pltpu.repeat(1.0 / l_scr[...], DSA_KV_RANK // 128, axis=1)).astype(BF16)
    for h in range(DSA_HEADS):
        o_ref[:, h * d:(h + 1) * d] = jnp.dot(o_lat[h * blk:(h + 1) * blk], wuv_ref[h],
                                              preferred_element_type=F32).astype(o_ref.dtype)


def _dsa_attention(qkv, qs_col_block, ckv, ckv_col_block, kv_norm_g, sel, w_uk, w_uv, bias_tiles, batch, seq):
    blk = min(ATT_BLOCK, seq)
    nq = seq // blk
    hd = DSA_HEADS * DSA_HEAD_DIM
    r = DSA_KV_RANK
    rows = DSA_HEADS * blk
    kern = functools.partial(_dsa_attn_kernel, blk=blk)
    return pl.pallas_call(
        kern,
        grid=(batch, nq),
        in_specs=[pl.BlockSpec((blk, hd), lambda b, i: (b * nq + i, qs_col_block)),
                  pl.BlockSpec((seq, r), lambda b, i: (b, ckv_col_block)),
                  pl.BlockSpec((1, r), lambda b, i: (0, 0)),
                  pl.BlockSpec((None, nq, blk, blk), lambda b, i: (b, 0, i, 0)),
                  pl.BlockSpec((DSA_HEADS, DSA_HEAD_DIM, r), lambda b, i: (0, 0, 0)),
                  pl.BlockSpec((DSA_HEADS, r, DSA_HEAD_DIM), lambda b, i: (0, 0, 0)),
                  pl.BlockSpec((DSA_HEADS, 3, blk, blk), lambda b, i: (0, 0, 0, 0))],
        out_specs=pl.BlockSpec((blk, hd), lambda b, i: (b * nq + i, 0)),
        out_shape=jax.ShapeDtypeStruct((batch * seq, hd), BF16),
        scratch_shapes=[pltpu.VMEM((seq, r), BF16), pltpu.VMEM((rows, r), BF16), pltpu.VMEM((rows, blk), F32),
                        pltpu.VMEM((rows, 128), F32), pltpu.VMEM((rows, 128), F32), pltpu.VMEM((rows, r), F32)],
        compiler_params=_params(2, 56),
    )(qkv, ckv, kv_norm_g.reshape(1, r), sel, w_uk, w_uv, bias_tiles)


def _merge_kernel(x_ref, od_ref, os_ref, wgd_ref, wgs_ref, wpd_ref, wps_ref, o_ref):
    x = x_ref[...]
    gd = jnp.dot(x, wgd_ref[...], preferred_element_type=F32)
    gs = jnp.dot(x, wgs_ref[...], preferred_element_type=F32)
    pd = jnp.dot(od_ref[...], wpd_ref[...], preferred_element_type=F32)
    ps = jnp.dot(os_ref[...], wps_ref[...], preferred_element_type=F32)
    o_ref[...] = (_sigmoid(gd) * pd + _sigmoid(gs) * ps).astype(o_ref.dtype)


def _gated_merge(xb, o_d, o_s, wgd, wgs, wpd, wps):
    m, dm = xb.shape
    tm = min(512, m)
    tn = 512
    kd, ks = o_d.shape[1], o_s.shape[1]
    return pl.pallas_call(
        _merge_kernel,
        grid=(dm // tn, m // tm),
        in_specs=[pl.BlockSpec((tm, dm), lambda j, i: (i, 0)),
                  pl.BlockSpec((tm, kd), lambda j, i: (i, 0)),
                  pl.BlockSpec((tm, ks), lambda j, i: (i, 0)),
                  pl.BlockSpec((dm, tn), lambda j, i: (0, j)),
                  pl.BlockSpec((dm, tn), lambda j, i: (0, j)),
                  pl.BlockSpec((kd, tn), lambda j, i: (0, j)),
                  pl.BlockSpec((ks, tn), lambda j, i: (0, j))],
        out_specs=pl.BlockSpec((tm, tn), lambda j, i: (i, j)),
        out_shape=jax.ShapeDtypeStruct((m, dm), BF16),
        compiler_params=_params(2, 40),
    )(xb, o_d, o_s, wgd, wgs, wpd, wps)


def _mm_res_ln_kernel(a_ref, w_ref, res_ref, g_ref, b_ref, o_ref, ob_ref):
    y = jnp.dot(a_ref[...], w_ref[...], preferred_element_type=F32)
    zn = _layer_norm(DN_ALPHA * res_ref[...] + y, g_ref[...], b_ref[...])
    o_ref[...] = zn
    ob_ref[...] = zn.astype(BF16)


def _matmul_res_ln(a, w, res, g, b):
    m, k = a.shape
    n = w.shape[1]
    tm = min(256, m)
    return pl.pallas_call(
        _mm_res_ln_kernel,
        grid=(m // tm,),
        in_specs=[pl.BlockSpec((tm, k), lambda i: (i, 0)),
                  pl.BlockSpec((k, n), lambda i: (0, 0)),
                  pl.BlockSpec((tm, n), lambda i: (i, 0)),
                  pl.BlockSpec((1, n), lambda i: (0, 0)),
                  pl.BlockSpec((1, n), lambda i: (0, 0))],
        out_specs=[pl.BlockSpec((tm, n), lambda i: (i, 0)), pl.BlockSpec((tm, n), lambda i: (i, 0))],
        out_shape=[jax.ShapeDtypeStruct((m, n), F32), jax.ShapeDtypeStruct((m, n), BF16)],
        compiler_params=_params(1, 40),
    )(a, w, res, g.reshape(1, n), b.reshape(1, n))


def _xattn_kernel(hb_ref, h_ref, wq_ref, kv_ref, wo_ref, g_ref, b_ref, o_ref, ob_ref):
    d = XATTN_HEAD_DIM
    hd = XATTN_HEADS * d
    scale = d ** -0.5
    qb = jnp.dot(hb_ref[...], wq_ref[...], preferred_element_type=F32).astype(BF16)
    kv = kv_ref[...]
    outs = []
    for h in range(XATTN_HEADS):
        s = lax.dot_general(qb[:, h * d:(h + 1) * d], kv[:, h * d:(h + 1) * d], NT_DIMS,
                            preferred_element_type=F32) * scale
        p = jnp.exp(s - jnp.max(s, axis=1, keepdims=True))
        l = jnp.sum(p, axis=1, keepdims=True)
        oh = jnp.dot(p.astype(BF16), kv[:, hd + h * d:hd + (h + 1) * d], preferred_element_type=F32)
        outs.append((oh / l).astype(BF16))
    o = jnp.concatenate(outs, axis=1)
    y = jnp.dot(o, wo_ref[...], preferred_element_type=F32)
    zn = _layer_norm(DN_ALPHA * h_ref[...] + y, g_ref[...], b_ref[...])
    o_ref[...] = zn
    ob_ref[...] = zn.astype(BF16)


def _cross_attention(hb, h, wq, kv, wo, g, b, batch, seq, n_mem):
    m, dm = h.shape
    tm = min(256, seq)
    nq = seq // tm
    hd = XATTN_HEADS * XATTN_HEAD_DIM
    return pl.pallas_call(
        _xattn_kernel,
        grid=(batch, nq),
        in_specs=[pl.BlockSpec((tm, dm), lambda bb, i: (bb * nq + i, 0)),
                  pl.BlockSpec((tm, dm), lambda bb, i: (bb * nq + i, 0)),
                  pl.BlockSpec((dm, hd), lambda bb, i: (0, 0)),
                  pl.BlockSpec((n_mem, 2 * hd), lambda bb, i: (bb, 0)),
                  pl.BlockSpec((hd, dm), lambda bb, i: (0, 0)),
                  pl.BlockSpec((1, dm), lambda bb, i: (0, 0)),
                  pl.BlockSpec((1, dm), lambda bb, i: (0, 0))],
        out_specs=[pl.BlockSpec((tm, dm), lambda bb, i: (bb * nq + i, 0)),
                   pl.BlockSpec((tm, dm), lambda bb, i: (bb * nq + i, 0))],
        out_shape=[jax.ShapeDtypeStruct((m, dm), F32), jax.ShapeDtypeStruct((m, dm), BF16)],
        compiler_params=_params(2, 40),
    )(hb, h, wq, kv, wo, g.reshape(1, dm), b.reshape(1, dm))


def _split_bf16(x):
    hi = x.astype(BF16)
    return hi, (x - hi.astype(F32)).astype(BF16)


def _router_kernel(h_ref, rwt_ref, rb_ref, idx_ref, gate_ref):
    tm = h_ref.shape[0]
    per = N_EXPERTS // N_GROUPS
    h_hi, h_lo = _split_bf16(h_ref[...])
    w_hi, w_lo = _split_bf16(rwt_ref[...])
    logits = (lax.dot_general(w_hi, h_hi, NT_DIMS, preferred_element_type=F32)
              + lax.dot_general(w_hi, h_lo, NT_DIMS, preferred_element_type=F32)
              + lax.dot_general(w_lo, h_hi, NT_DIMS, preferred_element_type=F32))
    scores = _sigmoid(logits)
    sel = scores + rb_ref[...]

    sel3 = sel.reshape(N_GROUPS, per, tm)
    r_iota = lax.broadcasted_iota(jnp.int32, sel3.shape, 1).astype(F32)
    m1 = jnp.max(sel3, axis=1, keepdims=True)
    first = jnp.min(jnp.where(sel3 == m1, r_iota, float(per)), axis=1, keepdims=True)
    m2 = jnp.max(jnp.where(r_iota == first, -jnp.inf, sel3), axis=1, keepdims=True)
    gscore = (m1 + m2).reshape(N_GROUPS, tm)

    g_iota = lax.broadcasted_iota(jnp.int32, gscore.shape, 0).astype(F32)
    keep = jnp.zeros(gscore.shape, F32)
    cur = gscore
    for _ in range(TOPK_GROUPS):
        m = jnp.max(cur, axis=0, keepdims=True)
        first = jnp.min(jnp.where(cur == m, g_iota, float(N_GROUPS)), axis=0, keepdims=True)
        pick = g_iota == first
        keep = jnp.where(pick, 1.0, keep)
        cur = jnp.where(pick, -jnp.inf, cur)
    keep3 = jnp.broadcast_to(keep.reshape(N_GROUPS, 1, tm), sel3.shape)
    cur = jnp.where(keep3 > 0.5, sel3, -jnp.inf).reshape(N_EXPERTS, tm)

    e_iota = lax.broadcasted_iota(jnp.int32, cur.shape, 0).astype(F32)
    gsum = jnp.zeros((1, tm), F32)
    gates = []
    for k in range(TOP_K):
        m = jnp.max(cur, axis=0, keepdims=True)
        first = jnp.min(jnp.where(cur == m, e_iota, float(N_EXPERTS)), axis=0, keepdims=True)
        pick = e_iota == first
        gk = jnp.sum(jnp.where(pick, scores, 0.0), axis=0, keepdims=True)
        idx_ref[k:k + 1, :] = first.astype(jnp.int32)
        gates.append(gk)
        gsum = gsum + gk
        cur = jnp.where(pick, -jnp.inf, cur)
    for k in range(TOP_K):
        gate_ref[k:k + 1, :] = gates[k] / gsum * ROUTED_SCALE


def _router(h, router_w, router_bias):
    m, dm = h.shape
    tm = min(512, m)
    return pl.pallas_call(
        _router_kernel,
        grid=(m // tm,),
        in_specs=[pl.BlockSpec((tm, dm), lambda i: (i, 0)),
                  pl.BlockSpec((N_EXPERTS, dm), lambda i: (0, 0)),
                  pl.BlockSpec((N_EXPERTS, 1), lambda i: (0, 0))],
        out_specs=[pl.BlockSpec((TOP_K, tm), lambda i: (0, i)), pl.BlockSpec((TOP_K, tm), lambda i: (0, i))],
        out_shape=[jax.ShapeDtypeStruct((TOP_K, m), jnp.int32), jax.ShapeDtypeStruct((TOP_K, m), F32)],
        compiler_params=_params(1, 32),
    )(h, router_w.T, router_bias.reshape(N_EXPERTS, 1))


def _expert_kernel(be_ref, nu_ref, x_ref, w1_ref, w3_ref, w2_ref, o_ref, w1b, w3b, w2b):
    i = pl.program_id(0)
    changed = jnp.logical_or(i == 0, be_ref[i] != be_ref[jnp.maximum(i - 1, 0)])

    @pl.when(changed)
    def _():
        w1b[...] = w1_ref[...].astype(BF16)
        w3b[...] = w3_ref[...].astype(BF16)
        w2b[...] = w2_ref[...].astype(BF16)

    @pl.when(i < nu_ref[0])
    def _():
        x = x_ref[...]
        a = jnp.dot(x, w1b[...], preferred_element_type=F32)
        b = jnp.dot(x, w3b[...], preferred_element_type=F32)
        hmid = (a * _sigmoid(a) * b).astype(BF16)
        o_ref[...] = jnp.dot(hmid, w2b[...], preferred_element_type=F32)

    @pl.when(i >= nu_ref[0])
    def _():
        o_ref[...] = jnp.zeros(o_ref.shape, o_ref.dtype)


def _expert_blocks(x_sorted, block_e, n_used, w1, w3, w2):
    rows, dm = x_sorted.shape
    n_blocks = rows // MOE_ROWS
    de = w1.shape[2]
    grid_spec = pltpu.PrefetchScalarGridSpec(
        num_scalar_prefetch=2,
        grid=(n_blocks,),
        in_specs=[pl.BlockSpec((MOE_ROWS, dm), lambda i, be, nu: (i, 0)),
                  pl.BlockSpec((None, dm, de), lambda i, be, nu: (be[i], 0, 0)),
                  pl.BlockSpec((None, dm, de), lambda i, be, nu: (be[i], 0, 0)),
                  pl.BlockSpec((None, de, dm), lambda i, be, nu: (be[i], 0, 0))],
        out_specs=pl.BlockSpec((MOE_ROWS, dm), lambda i, be, nu: (i, 0)),
        scratch_shapes=[pltpu.VMEM((dm, de), BF16), pltpu.VMEM((dm, de), BF16), pltpu.VMEM((de, dm), BF16)],
    )
    return pl.pallas_call(
        _expert_kernel,
        grid_spec=grid_spec,
        out_shape=jax.ShapeDtypeStruct((rows, dm), F32),
        compiler_params=_params(1, 48),
    )(block_e, n_used, x_sorted, w1, w3, w2)


def _shared_kernel(x_ref, w1_ref, w3_ref, w2_ref, o_ref):
    x = x_ref[...]
    a = jnp.dot(x, w1_ref[...], preferred_element_type=F32)
    b = jnp.dot(x, w3_ref[...], preferred_element_type=F32)
    hmid = (a * _sigmoid(a) * b).astype(BF16)
    o_ref[...] = jnp.dot(hmid, w2_ref[...], preferred_element_type=F32)


def _shared_expert(xb, w1, w3, w2):
    m, dm = xb.shape
    de = w1.shape[1]
    tm = min(512, m)
    return pl.pallas_call(
        _shared_kernel,
        grid=(m // tm,),
        in_specs=[pl.BlockSpec((tm, dm), lambda i: (i, 0)),
                  pl.BlockSpec((dm, de), lambda i: (0, 0)),
                  pl.BlockSpec((dm, de), lambda i: (0, 0)),
                  pl.BlockSpec((de, dm), lambda i: (0, 0))],
        out_specs=pl.BlockSpec((tm, dm), lambda i: (i, 0)),
        out_shape=jax.ShapeDtypeStruct((m, dm), F32),
        compiler_params=_params(1, 32),
    )(xb, w1, w3, w2)


def _combine_kernel(h_ref, sh_ref, yg_ref, gate_ref, g_ref, b_ref, o_ref):
    gate = gate_ref[...]
    y = sh_ref[...]
    for k in range(TOP_K):
        y = y + yg_ref[k] * gate[:, k:k + 1]
    o_ref[...] = _layer_norm(DN_ALPHA * h_ref[...] + y, g_ref[...], b_ref[...])


def _combine_ln(h, shared, yg, gate, g, b):
    m, dm = h.shape
    tm = min(128, m)
    return pl.pallas_call(
        _combine_kernel,
        grid=(m // tm,),
        in_specs=[pl.BlockSpec((tm, dm), lambda i: (i, 0)),
                  pl.BlockSpec((tm, dm), lambda i: (i, 0)),
                  pl.BlockSpec((TOP_K, tm, dm), lambda i: (0, i, 0)),
                  pl.BlockSpec((tm, TOP_K), lambda i: (i, 0)),
                  pl.BlockSpec((1, dm), lambda i: (0, 0)),
                  pl.BlockSpec((1, dm), lambda i: (0, 0))],
        out_specs=pl.BlockSpec((tm, dm), lambda i: (i, 0)),
        out_shape=jax.ShapeDtypeStruct((m, dm), F32),
        compiler_params=_params(1, 40),
    )(h, shared, yg, gate, g.reshape(1, dm), b.reshape(1, dm))


def _dispatch_plan(idx_t, n_tok):
    flat_e = idx_t.reshape(-1)
    n_assign = flat_e.shape[0]
    flat_tok = jnp.tile(jnp.arange(n_tok, dtype=jnp.int32), TOP_K)
    order = jnp.argsort(flat_e)
    se, stok = flat_e[order], flat_tok[order]
    counts = jnp.sum((flat_e[:, None] == jnp.arange(N_EXPERTS, dtype=jnp.int32)[None, :]).astype(jnp.int32), axis=0)
    padded = (counts + MOE_ROWS - 1) // MOE_ROWS * MOE_ROWS
    pad_end = jnp.cumsum(padded)
    pad_start = pad_end - padded
    start = jnp.cumsum(counts) - counts
    dest_sorted = pad_start[se] + (jnp.arange(n_assign, dtype=jnp.int32) - start[se])
    n_blocks = (n_assign + N_EXPERTS * (MOE_ROWS - 1) + MOE_ROWS - 1) // MOE_ROWS
    rows = n_blocks * MOE_ROWS
    row_tok = jnp.zeros((rows,), jnp.int32).at[dest_sorted].set(stok)
    dest = jnp.zeros((n_assign,), jnp.int32).at[order].set(dest_sorted)
    block_e = jnp.clip(jnp.searchsorted(pad_end, jnp.arange(n_blocks, dtype=jnp.int32) * MOE_ROWS, side='right'),
                       0, N_EXPERTS - 1).astype(jnp.int32)
    n_used = (pad_end[-1] // MOE_ROWS).astype(jnp.int32).reshape(1)
    return row_tok, dest, block_e, n_used


def _moe_sublayer(h, hb, router_w, router_bias, w1, w3, w2, sw1, sw3, sw2, g, b):
    n_tok, dm = h.shape
    idx_t, gate_t = _router(h, router_w, router_bias)
    row_tok, dest, block_e, n_used = _dispatch_plan(idx_t, n_tok)
    x_sorted = hb[row_tok]
    y_sorted = _expert_blocks(x_sorted, block_e, n_used, w1, w3, w2)
    yg = y_sorted[dest].reshape(TOP_K, n_tok, dm)
    shared = _shared_expert(hb, sw1.astype(BF16), sw3.astype(BF16), sw2.astype(BF16))
    return _combine_ln(h, shared, yg, gate_t.T, g, b)


def kernel(x, mem, positions, rel_bias_table, w_in, diff_lambda, diff_subln, idx_k_g, idx_k_b, kv_norm_g, w_uk, w_uv, w_proj_diff, w_proj_dsa, w_mix_out, ln1_g, ln1_b, xattn_wq, xattn_wkv, xattn_wo, ln2_g, ln2_b, router_w, router_bias, exp_w1, exp_w3, exp_w2, sh_w1, sh_w3, sh_w2, ln3_g, ln3_b):
    del positions
    batch, seq, dm = x.shape
    n_mem = mem.shape[1]
    n_tok = batch * seq
    lambda_init = 0.8 - 0.6 * math.exp(-0.3 * 0)
    blk = min(ATT_BLOCK, seq)

    dq = DIFF_HEADS * DIFF_V_DIM
    c_qs = 3 * dq
    c_ckv = c_qs + DSA_HEADS * DSA_HEAD_DIM
    c_qi = c_ckv + DSA_KV_RANK
    c_ki = c_qi + IDX_HEADS * IDX_HEAD_DIM
    c_gd = c_ki + IDX_HEAD_DIM + IDX_HEADS
    c_gs = c_gd + dm

    xf = x.reshape(n_tok, dm)
    xb = xf.astype(BF16)
    w_in0 = w_in[0]
    qkv = _matmul(xb, w_in0, 0, c_ckv, BF16, 1024, 1024)
    ckv = _matmul(xb, w_in0, c_ckv, DSA_KV_RANK, F32, 1024, 512)
    qi = _matmul(xb, w_in0, c_qi, c_ki - c_qi, BF16, 1024, 512)
    w_tail = jnp.pad(w_in0[:, c_ki:c_gd], ((0, 0), (0, 128 - (c_gd - c_ki))))
    kiw = _matmul(xb, w_tail, 0, 128, F32, 1024, 128)

    tiles = _rel_bias_tiles(rel_bias_table, blk)
    o_d = _diff_attention(qkv, tiles[:DIFF_HEADS], diff_lambda[0], diff_subln[0], batch, seq, lambda_init)

    k_sel = min(DSA_TOPK_MAX, seq // 4)
    sel = _indexer_mask(qi, kiw, idx_k_g[0], idx_k_b[0], batch, seq, k_sel, qi_col_block=0)
    o_s = _dsa_attention(qkv, c_qs // (DSA_HEADS * DSA_HEAD_DIM), ckv, 0, kv_norm_g[0], sel,
                         w_uk[0].astype(BF16), w_uv[0].astype(BF16), tiles[DIFF_HEADS:], batch, seq)

    merged = _gated_merge(xb, o_d, o_s, w_in0[:, c_gd:c_gs].astype(BF16), w_in0[:, c_gs:c_gs + dm].astype(BF16),
                          w_proj_diff[0].astype(BF16), w_proj_dsa[0].astype(BF16))
    h1, h1b = _matmul_res_ln(merged, w_mix_out[0].astype(BF16), xf, ln1_g[0], ln1_b[0])

    memb = mem.reshape(batch * n_mem, dm).astype(BF16)
    kv = _matmul(memb, xattn_wkv[0], 0, 2 * XATTN_HEADS * XATTN_HEAD_DIM, BF16, 1024, 512)
    h2, h2b = _cross_attention(h1b, h1, xattn_wq[0].astype(BF16), kv, xattn_wo[0].astype(BF16),
                               ln2_g[0], ln2_b[0], batch, seq, n_mem)

    out = _moe_sublayer(h2, h2b, router_w[0], router_bias[0], exp_w1[0], exp_w3[0], exp_w2[0],
                        sh_w1[0], sh_w3[0], sh_w2[0], ln3_g[0], ln3_b[0])
    return out.reshape(batch, seq, dm)
```

```python
import functools
import math

import jax
import jax.numpy as jnp
from jax import lax
from jax.experimental import pallas as pl
from jax.experimental.pallas import tpu as pltpu

F32 = jnp.float32
BF16 = jnp.bfloat16
NEG = -1e30
INT_MIN = -(2 ** 31)
MIB = 1024 * 1024

LN_EPS = 1e-5
DIFF_HEADS = 8
DIFF_HEAD_DIM = 128
DIFF_V_DIM = 2 * DIFF_HEAD_DIM
DSA_HEADS = 8
DSA_HEAD_DIM = 128
DSA_KV_RANK = 512
IDX_HEADS = 16
IDX_HEAD_DIM = 64
DSA_TOPK_MAX = 256
REL_BUCKETS = 32
REL_MAX_DIST = 128
XATTN_HEADS = 4
XATTN_HEAD_DIM = 128
N_EXPERTS = 64
EXPERT_DIM = 512
TOP_K = 8
N_GROUPS = 8
TOPK_GROUPS = 4
ROUTED_SCALE = 2.5
DEPTH = 1
DN_ALPHA = (2 * DEPTH) ** 0.25

ATT_BLOCK = 256
MOE_ROWS = 256
NT_DIMS = (((1,), (1,)), ((), ()))


def _params(n_grid, vmem_mib):
    return pltpu.CompilerParams(dimension_semantics=("arbitrary",) * n_grid,
                                vmem_limit_bytes=vmem_mib * MIB)


def _sigmoid(x):
    return 1.0 / (1.0 + jnp.exp(-x))


def _layer_norm(z, g, b):
    mu = jnp.mean(z, axis=1, keepdims=True)
    zc = z - mu
    var = jnp.mean(zc * zc, axis=1, keepdims=True)
    return zc * lax.rsqrt(var + LN_EPS) * g + b


def _mm_kernel(a_ref, b_ref, o_ref, bq_ref):
    @pl.when(pl.program_id(1) == 0)
    def _():
        bq_ref[...] = b_ref[...].astype(BF16)

    o_ref[...] = jnp.dot(a_ref[...], bq_ref[...], preferred_element_type=F32).astype(o_ref.dtype)


def _matmul(a, b, col_start, n_cols, out_dtype, tm, tn):
    m, k = a.shape
    tm = min(tm, m)
    assert col_start % tn == 0 and n_cols % tn == 0 and m % tm == 0
    off = col_start // tn
    return pl.pallas_call(
        _mm_kernel,
        grid=(n_cols // tn, m // tm),
        in_specs=[pl.BlockSpec((tm, k), lambda j, i: (i, 0)),
                  pl.BlockSpec((k, tn), lambda j, i: (0, j + off))],
        out_specs=pl.BlockSpec((tm, tn), lambda j, i: (i, j)),
        out_shape=jax.ShapeDtypeStruct((m, n_cols), out_dtype),
        scratch_shapes=[pltpu.VMEM((k, tn), BF16)],
        compiler_params=_params(2, 48),
    )(a, b)


def _rel_bucket(n):
    n = jnp.maximum(n, 0)
    max_exact = REL_BUCKETS // 2
    nf = jnp.maximum(n, 1).astype(F32)
    large = max_exact + (jnp.log(nf / max_exact) / math.log(REL_MAX_DIST / max_exact)
                         * (REL_BUCKETS - max_exact)).astype(jnp.int32)
    large = jnp.minimum(large, REL_BUCKETS - 1)
    return jnp.where(n < max_exact, n, large)


def _rel_bias_tiles(table, blk):
    assert blk >= REL_MAX_DIST
    r = jnp.arange(blk, dtype=jnp.int32)[:, None]
    c = jnp.arange(blk, dtype=jnp.int32)[None, :]
    tiles = []
    for back in range(3):
        n = back * blk + r - c
        t = jnp.where((n >= 0)[..., None], table[_rel_bucket(n)], NEG)
        tiles.append(jnp.moveaxis(t, -1, 0))
    return jnp.stack(tiles, axis=1).astype(F32)


def _lane_repeat(x, n, axis=1):
    assert axis == 1
    return x if n == 1 else jnp.concatenate([x] * n, axis=1)


def _online_softmax_step(s, v, m_ref, l_ref, acc_ref):
    tk = s.shape[1]
    e = v.shape[1]
    m_prev = m_ref[...]
    m_next = jnp.maximum(m_prev, jnp.max(s, axis=1, keepdims=True))
    p = jnp.exp(s - _lane_repeat(m_next, tk // 128, axis=1))
    alpha = jnp.exp(m_prev - m_next)
    l_ref[...] = alpha * l_ref[...] + jnp.sum(p, axis=1, keepdims=True)
    m_ref[...] = m_next
    acc_ref[...] = (acc_ref[...] * _lane_repeat(alpha, e // 128, axis=1)
                    + jnp.dot(p.astype(BF16), v, preferred_element_type=F32))


def _diff_attn_kernel(q_ref, k_ref, v_ref, bias_ref, lam_ref, g_ref, o_ref, m_scr, l_scr, acc_scr,
                      *, blk, lambda_init):
    i = pl.program_id(2)
    d = DIFF_HEAD_DIM
    scale = d ** -0.5
    m_scr[...] = jnp.full(m_scr.shape, NEG, F32)
    l_scr[...] = jnp.zeros(l_scr.shape, F32)
    acc_scr[...] = jnp.zeros(acc_scr.shape, F32)
    q = q_ref[...]

    def body(j, carry):
        start = pl.multiple_of(j * blk, blk)
        kk = k_ref[pl.ds(start, blk), :]
        vv = v_ref[pl.ds(start, blk), :]
        bt = bias_ref[jnp.minimum(i - j, 2)]
        for c in range(2):
            s = lax.dot_general(q[:, c * d:(c + 1) * d], kk[:, c * d:(c + 1) * d], NT_DIMS,
                                preferred_element_type=F32)
            _online_softmax_step(s * scale + bt, vv, m_scr.at[c], l_scr.at[c], acc_scr.at[c])
        return carry

    lax.fori_loop(0, i + 1, body, 0)

    lp = lam_ref[...]
    lam = (jnp.exp(jnp.sum(lp[0:1] * lp[1:2], axis=1, keepdims=True))
           - jnp.exp(jnp.sum(lp[2:3] * lp[3:4], axis=1, keepdims=True)) + lambda_init)
    o0 = acc_scr[0] * _lane_repeat(1.0 / l_scr[0], 2, axis=1)
    o1 = acc_scr[1] * _lane_repeat(1.0 / l_scr[1], 2, axis=1)
    o = o0 - lam * o1
    ms = jnp.mean(o * o, axis=1, keepdims=True)
    o_ref[...] = (o * lax.rsqrt(ms + LN_EPS) * g_ref[...] * (1.0 - lambda_init)).astype(o_ref.dtype)


def _diff_attention(qkv, bias_tiles, diff_lambda, subln, batch, seq, lambda_init):
    blk = min(ATT_BLOCK, seq)
    nq = seq // blk
    h2d = DIFF_V_DIM
    kern = functools.partial(_diff_attn_kernel, blk=blk, lambda_init=lambda_init)
    return pl.pallas_call(
        kern,
        grid=(batch, DIFF_HEADS, nq),
        in_specs=[pl.BlockSpec((blk, h2d), lambda b, h, i: (b * nq + i, h)),
                  pl.BlockSpec((seq, h2d), lambda b, h, i: (b, DIFF_HEADS + h)),
                  pl.BlockSpec((seq, h2d), lambda b, h, i: (b, 2 * DIFF_HEADS + h)),
                  pl.BlockSpec((None, 3, blk, blk), lambda b, h, i: (h, 0, 0, 0)),
                  pl.BlockSpec((4, DIFF_HEAD_DIM), lambda b, h, i: (0, 0)),
                  pl.BlockSpec((1, h2d), lambda b, h, i: (0, 0))],
        out_specs=pl.BlockSpec((blk, h2d), lambda b, h, i: (b * nq + i, h)),
        out_shape=jax.ShapeDtypeStruct((batch * seq, DIFF_HEADS * h2d), BF16),
        scratch_shapes=[pltpu.VMEM((2, blk, 128), F32), pltpu.VMEM((2, blk, 128), F32),
                        pltpu.VMEM((2, blk, h2d), F32)],
        compiler_params=_params(3, 32),
    )(qkv, qkv, qkv, bias_tiles, diff_lambda, subln.reshape(1, h2d))


def _indexer_kernel(qi_ref, kfull_ref, wblk_ref, g_ref, b_ref, o_ref, ki_scr, key_scr, *, blk, seq, k_sel):
    i = pl.program_id(1)

    @pl.when(i == 0)
    def _():
        kr = kfull_ref[:, 0:IDX_HEAD_DIM]
        ki_scr[...] = _layer_norm(kr, g_ref[...], b_ref[...]).astype(BF16)

    w = wblk_ref[:, IDX_HEAD_DIM:IDX_HEAD_DIM + IDX_HEADS] * (IDX_HEADS ** -0.5 * IDX_HEAD_DIM ** -0.5)
    qi = qi_ref[...].astype(BF16)
    ki = ki_scr[...]
    score = jnp.zeros((blk, seq), F32)
    for h in range(IDX_HEADS):
        lg = lax.dot_general(qi[:, h * IDX_HEAD_DIM:(h + 1) * IDX_HEAD_DIM], ki, NT_DIMS,
                             preferred_element_type=F32)
        score = score + jnp.maximum(lg, 0.0) * w[:, h:h + 1]

    row = i * blk + lax.broadcasted_iota(jnp.int32, (blk, seq), 0)
    col = lax.broadcasted_iota(jnp.int32, (blk, seq), 1)
    valid = col <= row
    bits = lax.bitcast_convert_type(score, jnp.int32)
    key = bits ^ ((bits >> 31) & jnp.int32(0x7FFFFFFF))
    key_scr[...] = jnp.where(valid, key, INT_MIN)

    def search(it, thr):
        trial = thr + lax.shift_left(jnp.int32(1), 31 - it)
        cnt = jnp.sum(jnp.where(key_scr[...] >= trial, 1.0, 0.0), axis=1, keepdims=True)
        return jnp.where(cnt >= k_sel, trial, thr)

    thr = lax.fori_loop(0, 32, search, jnp.full((blk, 1), INT_MIN, jnp.int32))
    mask = jnp.where(valid & (key_scr[...] >= thr), 0.0, NEG).astype(o_ref.dtype)
    for jj in range(seq // blk):
        o_ref[jj] = mask[:, jj * blk:(jj + 1) * blk]


def _indexer_mask(qi, kiw, idx_k_g, idx_k_b, batch, seq, k_sel, qi_col_block):
    blk = min(ATT_BLOCK, seq)
    nq = seq // blk
    qw = IDX_HEADS * IDX_HEAD_DIM
    kern = functools.partial(_indexer_kernel, blk=blk, seq=seq, k_sel=k_sel)
    return pl.pallas_call(
        kern,
        grid=(batch, nq),
        in_specs=[pl.BlockSpec((blk, qw), lambda b, i: (b * nq + i, qi_col_block)),
                  pl.BlockSpec((seq, 128), lambda b, i: (b, 0)),
                  pl.BlockSpec((blk, 128), lambda b, i: (b * nq + i, 0)),
                  pl.BlockSpec((1, IDX_HEAD_DIM), lambda b, i: (0, 0)),
                  pl.BlockSpec((1, IDX_HEAD_DIM), lambda b, i: (0, 0))],
        out_specs=pl.BlockSpec((None, nq, blk, blk), lambda b, i: (b, 0, i, 0)),
        out_shape=jax.ShapeDtypeStruct((batch, nq, seq, blk), BF16),
        scratch_shapes=[pltpu.VMEM((seq, IDX_HEAD_DIM), BF16), pltpu.VMEM((blk, seq), jnp.int32)],
        compiler_params=_params(2, 48),
    )(qi, kiw, kiw, idx_k_g.reshape(1, -1), idx_k_b.reshape(1, -1))


def _dsa_attn_kernel(qs_ref, ckv_ref, g_ref, sel_ref, wuk_ref, wuv_ref, bias_ref, o_ref,
                     c_scr, ql_scr, sb_scr, m_scr, l_scr, acc_scr, *, blk):
    i = pl.program_id(1)
    d = DSA_HEAD_DIM
    scale = d ** -0.5

    @pl.when(i == 0)
    def _():
        x = ckv_ref[...]
        c = x * lax.rsqrt(jnp.mean(x * x, axis=1, keepdims=True) + LN_EPS) * g_ref[...]
        c_scr[...] = c.astype(BF16)

    qs = qs_ref[...]
    for h in range(DSA_HEADS):
        ql = jnp.dot(qs[:, h * d:(h + 1) * d], wuk_ref[h], preferred_element_type=F32)
        ql_scr[h * blk:(h + 1) * blk, :] = (ql * scale).astype(BF16)

    m_scr[...] = jnp.full(m_scr.shape, NEG, F32)
    l_scr[...] = jnp.zeros(l_scr.shape, F32)
    acc_scr[...] = jnp.zeros(acc_scr.shape, F32)

    def body(j, carry):
        start = pl.multiple_of(j * blk, blk)
        cc = c_scr[pl.ds(start, blk), :]
        s = lax.dot_general(ql_scr[...], cc, NT_DIMS, preferred_element_type=F32)
        selm = sel_ref[j].astype(F32)
        back = jnp.minimum(i - j, 2)
        for h in range(DSA_HEADS):
            sb_scr[h * blk:(h + 1) * blk, :] = bias_ref[h, back] + selm
        _online_softmax_step(s + sb_scr[...], cc, m_scr, l_scr, acc_scr)
        return carry

    lax.fori_loop(0, i + 1, body, 0)

    o_lat = (acc_scr[...] * _lane_repeat(1.0 / l_scr[...], DSA_KV_RANK // 128, axis=1)).astype(BF16)
    for h in range(DSA_HEADS):
        o_ref[:, h * d:(h + 1) * d] = jnp.dot(o_lat[h * blk:(h + 1) * blk], wuv_ref[h],
                                              preferred_element_type=F32).astype(o_ref.dtype)


def _dsa_attention(qkv, qs_col_block, ckv, ckv_col_block, kv_norm_g, sel, w_uk, w_uv, bias_tiles, batch, seq):
    blk = min(ATT_BLOCK, seq)
    nq = seq // blk
    hd = DSA_HEADS * DSA_HEAD_DIM
    r = DSA_KV_RANK
    rows = DSA_HEADS * blk
    kern = functools.partial(_dsa_attn_kernel, blk=blk)
    return pl.pallas_call(
        kern,
        grid=(batch, nq),
        in_specs=[pl.BlockSpec((blk, hd), lambda b, i: (b * nq + i, qs_col_block)),
                  pl.BlockSpec((seq, r), lambda b, i: (b, ckv_col_block)),
                  pl.BlockSpec((1, r), lambda b, i: (0, 0)),
                  pl.BlockSpec((None, nq, blk, blk), lambda b, i: (b, 0, i, 0)),
                  pl.BlockSpec((DSA_HEADS, DSA_HEAD_DIM, r), lambda b, i: (0, 0, 0)),
                  pl.BlockSpec((DSA_HEADS, r, DSA_HEAD_DIM), lambda b, i: (0, 0, 0)),
                  pl.BlockSpec((DSA_HEADS, 3, blk, blk), lambda b, i: (0, 0, 0, 0))],
        out_specs=pl.BlockSpec((blk, hd), lambda b, i: (b * nq + i, 0)),
        out_shape=jax.ShapeDtypeStruct((batch * seq, hd), BF16),
        scratch_shapes=[pltpu.VMEM((seq, r), BF16), pltpu.VMEM((rows, r), BF16), pltpu.VMEM((rows, blk), F32),
                        pltpu.VMEM((rows, 128), F32), pltpu.VMEM((rows, 128), F32), pltpu.VMEM((rows, r), F32)],
        compiler_params=_params(2, 56),
    )(qkv, ckv, kv_norm_g.reshape(1, r), sel, w_uk, w_uv, bias_tiles)


def _merge_kernel(x_ref, od_ref, os_ref, wgd_ref, wgs_ref, wpd_ref, wps_ref, o_ref):
    x = x_ref[...]
    gd = jnp.dot(x, wgd_ref[...], preferred_element_type=F32)
    gs = jnp.dot(x, wgs_ref[...], preferred_element_type=F32)
    pd = jnp.dot(od_ref[...], wpd_ref[...], preferred_element_type=F32)
    ps = jnp.dot(os_ref[...], wps_ref[...], preferred_element_type=F32)
    o_ref[...] = (_sigmoid(gd) * pd + _sigmoid(gs) * ps).astype(o_ref.dtype)


def _gated_merge(xb, o_d, o_s, wgd, wgs, wpd, wps):
    m, dm = xb.shape
    tm = min(512, m)
    tn = 512
    kd, ks = o_d.shape[1], o_s.shape[1]
    return pl.pallas_call(
        _merge_kernel,
        grid=(dm // tn, m // tm),
        in_specs=[pl.BlockSpec((tm, dm), lambda j, i: (i, 0)),
                  pl.BlockSpec((tm, kd), lambda j, i: (i, 0)),
                  pl.BlockSpec((tm, ks), lambda j, i: (i, 0)),
                  pl.BlockSpec((dm, tn), lambda j, i: (0, j)),
                  pl.BlockSpec((dm, tn), lambda j, i: (0, j)),
                  pl.BlockSpec((kd, tn), lambda j, i: (0, j)),
                  pl.BlockSpec((ks, tn), lambda j, i: (0, j))],
        out_specs=pl.BlockSpec((tm, tn), lambda j, i: (i, j)),
        out_shape=jax.ShapeDtypeStruct((m, dm), BF16),
        compiler_params=_params(2, 40),
    )(xb, o_d, o_s, wgd, wgs, wpd, wps)


def _mm_res_ln_kernel(a_ref, w_ref, res_ref, g_ref, b_ref, o_ref, ob_ref):
    y = jnp.dot(a_ref[...], w_ref[...], preferred_element_type=F32)
    zn = _layer_norm(DN_ALPHA * res_ref[...] + y, g_ref[...], b_ref[...])
    o_ref[...] = zn
    ob_ref[...] = zn.astype(BF16)


def _matmul_res_ln(a, w, res, g, b):
    m, k = a.shape
    n = w.shape[1]
    tm = min(256, m)
    return pl.pallas_call(
        _mm_res_ln_kernel,
        grid=(m // tm,),
        in_specs=[pl.BlockSpec((tm, k), lambda i: (i, 0)),
                  pl.BlockSpec((k, n), lambda i: (0, 0)),
                  pl.BlockSpec((tm, n), lambda i: (i, 0)),
                  pl.BlockSpec((1, n), lambda i: (0, 0)),
                  pl.BlockSpec((1, n), lambda i: (0, 0))],
        out_specs=[pl.BlockSpec((tm, n), lambda i: (i, 0)), pl.BlockSpec((tm, n), lambda i: (i, 0))],
        out_shape=[jax.ShapeDtypeStruct((m, n), F32), jax.ShapeDtypeStruct((m, n), BF16)],
        compiler_params=_params(1, 40),
    )(a, w, res, g.reshape(1, n), b.reshape(1, n))


def _pack_bf16_pairs(zb):
    half = zb.shape[1] // 2
    bits = lax.bitcast_convert_type(zb.astype(F32), jnp.uint32)
    return (bits[:, :half] & jnp.uint32(0xFFFF0000)) | (bits[:, half:] >> 16)


def _unpack_bf16_pairs(w):
    hi = lax.bitcast_convert_type(w & jnp.uint32(0xFFFF0000), F32).astype(BF16)
    lo = lax.bitcast_convert_type(w << 16, F32).astype(BF16)
    return hi, lo


def _xattn_kernel(hb_ref, h_ref, wq_ref, kv_ref, wo_ref, g_ref, b_ref, o_ref, ob_ref, op_ref):
    d = XATTN_HEAD_DIM
    hd = XATTN_HEADS * d
    scale = d ** -0.5
    qb = jnp.dot(hb_ref[...], wq_ref[...], preferred_element_type=F32).astype(BF16)
    kv = kv_ref[...]
    outs = []
    for h in range(XATTN_HEADS):
        s = lax.dot_general(qb[:, h * d:(h + 1) * d], kv[:, h * d:(h + 1) * d], NT_DIMS,
                            preferred_element_type=F32) * scale
        p = jnp.exp(s - jnp.max(s, axis=1, keepdims=True))
        l = jnp.sum(p, axis=1, keepdims=True)
        oh = jnp.dot(p.astype(BF16), kv[:, hd + h * d:hd + (h + 1) * d], preferred_element_type=F32)
        outs.append((oh / l).astype(BF16))
    o = jnp.concatenate(outs, axis=1)
    y = jnp.dot(o, wo_ref[...], preferred_element_type=F32)
    zn = _layer_norm(DN_ALPHA * h_ref[...] + y, g_ref[...], b_ref[...])
    zb = zn.astype(BF16)
    o_ref[...] = zn
    ob_ref[...] = zb
    op_ref[...] = _pack_bf16_pairs(zb)


def _cross_attention(hb, h, wq, kv, wo, g, b, batch, seq, n_mem):
    m, dm = h.shape
    tm = min(256, seq)
    nq = seq // tm
    hd = XATTN_HEADS * XATTN_HEAD_DIM
    return pl.pallas_call(
        _xattn_kernel,
        grid=(batch, nq),
        in_specs=[pl.BlockSpec((tm, dm), lambda bb, i: (bb * nq + i, 0)),
                  pl.BlockSpec((tm, dm), lambda bb, i: (bb * nq + i, 0)),
                  pl.BlockSpec((dm, hd), lambda bb, i: (0, 0)),
                  pl.BlockSpec((n_mem, 2 * hd), lambda bb, i: (bb, 0)),
                  pl.BlockSpec((hd, dm), lambda bb, i: (0, 0)),
                  pl.BlockSpec((1, dm), lambda bb, i: (0, 0)),
                  pl.BlockSpec((1, dm), lambda bb, i: (0, 0))],
        out_specs=[pl.BlockSpec((tm, dm), lambda bb, i: (bb * nq + i, 0)),
                   pl.BlockSpec((tm, dm), lambda bb, i: (bb * nq + i, 0)),
                   pl.BlockSpec((tm, dm // 2), lambda bb, i: (bb * nq + i, 0))],
        out_shape=[jax.ShapeDtypeStruct((m, dm), F32), jax.ShapeDtypeStruct((m, dm), BF16),
                   jax.ShapeDtypeStruct((m, dm // 2), jnp.uint32)],
        compiler_params=_params(2, 40),
    )(hb, h, wq, kv, wo, g.reshape(1, dm), b.reshape(1, dm))


def _split_bf16(x):
    hi = x.astype(BF16)
    return hi, (x - hi.astype(F32)).astype(BF16)


def _router_kernel(h_ref, rwt_ref, rb_ref, idx_ref, gate_ref, rank_ref, cnt_ref, cnt_scr):
    tm = h_ref.shape[0]
    per = N_EXPERTS // N_GROUPS
    h_hi, h_lo = _split_bf16(h_ref[...])
    w_hi, w_lo = _split_bf16(rwt_ref[...])
    logits = (lax.dot_general(w_hi, h_hi, NT_DIMS, preferred_element_type=F32)
              + lax.dot_general(w_hi, h_lo, NT_DIMS, preferred_element_type=F32)
              + lax.dot_general(w_lo, h_hi, NT_DIMS, preferred_element_type=F32))
    scores = _sigmoid(logits)
    sel = scores + rb_ref[...]

    sel3 = sel.reshape(N_GROUPS, per, tm)
    r_iota = lax.broadcasted_iota(jnp.int32, sel3.shape, 1).astype(F32)
    m1 = jnp.max(sel3, axis=1, keepdims=True)
    first = jnp.min(jnp.where(sel3 == m1, r_iota, float(per)), axis=1, keepdims=True)
    m2 = jnp.max(jnp.where(r_iota == first, -jnp.inf, sel3), axis=1, keepdims=True)
    gscore = (m1 + m2).reshape(N_GROUPS, tm)

    g_iota = lax.broadcasted_iota(jnp.int32, gscore.shape, 0).astype(F32)
    keep = jnp.zeros(gscore.shape, F32)
    cur = gscore
    for _ in range(TOPK_GROUPS):
        m = jnp.max(cur, axis=0, keepdims=True)
        first = jnp.min(jnp.where(cur == m, g_iota, float(N_GROUPS)), axis=0, keepdims=True)
        pick = g_iota == first
        keep = jnp.where(pick, 1.0, keep)
        cur = jnp.where(pick, -jnp.inf, cur)
    keep3 = jnp.broadcast_to(keep.reshape(N_GROUPS, 1, tm), sel3.shape)
    cur = jnp.where(keep3 > 0.5, sel3, -jnp.inf).reshape(N_EXPERTS, tm)

    e_iota = lax.broadcasted_iota(jnp.int32, cur.shape, 0).astype(F32)
    gsum = jnp.zeros((1, tm), F32)
    gates, picks = [], []
    for k in range(TOP_K):
        m = jnp.max(cur, axis=0, keepdims=True)
        first = jnp.min(jnp.where(cur == m, e_iota, float(N_EXPERTS)), axis=0, keepdims=True)
        pick = e_iota == first
        gk = jnp.sum(jnp.where(pick, scores, 0.0), axis=0, keepdims=True)
        idx_ref[k:k + 1, :] = first.astype(jnp.int32)
        gates.append(gk)
        picks.append(pick)
        gsum = gsum + gk
        cur = jnp.where(pick, -jnp.inf, cur)
    for k in range(TOP_K):
        gate_ref[k:k + 1, :] = gates[k] / gsum * ROUTED_SCALE

    @pl.when(pl.program_id(0) == 0)
    def _():
        cnt_scr[...] = jnp.zeros(cnt_scr.shape, F32)

    mask = jnp.zeros(cur.shape, F32)
    for k in range(TOP_K):
        mask = jnp.where(picks[k], 1.0, mask)
    upper = (lax.broadcasted_iota(jnp.int32, (tm, tm), 0) <= lax.broadcasted_iota(jnp.int32, (tm, tm), 1))
    cum = jnp.dot(mask.astype(BF16), jnp.where(upper, 1.0, 0.0).astype(BF16), preferred_element_type=F32)
    before = cnt_scr[...][:, 0:1] + cum - mask
    for k in range(TOP_K):
        rank_ref[k:k + 1, :] = jnp.sum(jnp.where(picks[k], before, 0.0), axis=0, keepdims=True).astype(jnp.int32)
    cnt_scr[...] = cnt_scr[...] + jnp.sum(mask, axis=1, keepdims=True)
    cnt_ref[...] = cnt_scr[...].astype(jnp.int32)


def _router(h, router_w, router_bias):
    m, dm = h.shape
    tm = min(512, m)
    return pl.pallas_call(
        _router_kernel,
        grid=(m // tm,),
        in_specs=[pl.BlockSpec((tm, dm), lambda i: (i, 0)),
                  pl.BlockSpec((N_EXPERTS, dm), lambda i: (0, 0)),
                  pl.BlockSpec((N_EXPERTS, 1), lambda i: (0, 0))],
        out_specs=[pl.BlockSpec((TOP_K, tm), lambda i: (0, i)), pl.BlockSpec((TOP_K, tm), lambda i: (0, i)),
                   pl.BlockSpec((TOP_K, tm), lambda i: (0, i)), pl.BlockSpec((N_EXPERTS, 128), lambda i: (0, 0))],
        out_shape=[jax.ShapeDtypeStruct((TOP_K, m), jnp.int32), jax.ShapeDtypeStruct((TOP_K, m), F32),
                   jax.ShapeDtypeStruct((TOP_K, m), jnp.int32), jax.ShapeDtypeStruct((N_EXPERTS, 128), jnp.int32)],
        scratch_shapes=[pltpu.VMEM((N_EXPERTS, 128), F32)],
        compiler_params=_params(1, 32),
    )(h, router_w.T, router_bias.reshape(N_EXPERTS, 1))


def _dest_kernel(idx_ref, rank_ref, ps_ref, dest_ref):
    tm = idx_ref.shape[1]
    e_iota = lax.broadcasted_iota(jnp.int32, (N_EXPERTS, tm), 0)
    ps = ps_ref[...][:, 0:1]
    for k in range(TOP_K):
        base = jnp.sum(jnp.where(e_iota == idx_ref[k:k + 1, :], ps, 0), axis=0, keepdims=True)
        dest_ref[k:k + 1, :] = rank_ref[k:k + 1, :] + base


def _dest_rows(idx_t, rank_t, pad_start):
    m = idx_t.shape[1]
    tm = min(2048, m)
    ps = jnp.broadcast_to(pad_start.astype(F32).reshape(N_EXPERTS, 1), (N_EXPERTS, 128))
    out = pl.pallas_call(
        _dest_kernel,
        grid=(m // tm,),
        in_specs=[pl.BlockSpec((TOP_K, tm), lambda i: (0, i)), pl.BlockSpec((TOP_K, tm), lambda i: (0, i)),
                  pl.BlockSpec((N_EXPERTS, 128), lambda i: (0, 0))],
        out_specs=pl.BlockSpec((TOP_K, tm), lambda i: (0, i)),
        out_shape=jax.ShapeDtypeStruct((TOP_K, m), F32),
        compiler_params=_params(1, 32),
    )(idx_t, rank_t.astype(F32), ps)
    return out.astype(jnp.int32)


def _dispatch_kernel(dest_ref, x_ref, init_ref, xs_ref, sem):
    del init_ref
    tm = x_ref.shape[0]

    def copy(t, k):
        return pltpu.make_async_copy(x_ref.at[pl.ds(t, 1), :], xs_ref.at[pl.ds(dest_ref[0, k, t], 1), :], sem)

    def start(t, carry):
        for k in range(TOP_K):
            copy(t, k).start()
        return carry

    def wait(t, carry):
        for k in range(TOP_K):
            copy(t, k).wait()
        return carry

    lax.fori_loop(0, tm, start, 0)
    lax.fori_loop(0, tm, wait, 0)


def _dispatch(xp, dest, rows):
    m, wd = xp.shape
    tm = min(512, m)
    dest3 = dest.reshape(TOP_K, m // tm, tm).transpose(1, 0, 2)
    return pl.pallas_call(
        _dispatch_kernel,
        grid=(m // tm,),
        in_specs=[pl.BlockSpec((1, TOP_K, tm), lambda i: (i, 0, 0), memory_space=pltpu.SMEM),
                  pl.BlockSpec((tm, wd), lambda i: (i, 0)),
                  pl.BlockSpec(memory_space=pl.ANY)],
        out_specs=pl.BlockSpec(memory_space=pl.ANY),
        out_shape=jax.ShapeDtypeStruct((rows, wd), xp.dtype),
        scratch_shapes=[pltpu.SemaphoreType.DMA(())],
        input_output_aliases={2: 0},
        compiler_params=_params(1, 32),
    )(dest3, xp, jnp.zeros((rows, wd), xp.dtype))


def _expert_kernel(be_ref, nu_ref, x_ref, w1_ref, w3_ref, w2_ref, o_ref, w1b, w3b, w2b):
    i = pl.program_id(0)
    changed = jnp.logical_or(i == 0, be_ref[i] != be_ref[jnp.maximum(i - 1, 0)])

    @pl.when(changed)
    def _():
        w1b[...] = w1_ref[...].astype(BF16)
        w3b[...] = w3_ref[...].astype(BF16)
        w2b[...] = w2_ref[...].astype(BF16)

    @pl.when(i < nu_ref[0])
    def _():
        xh, xl = _unpack_bf16_pairs(x_ref[...])
        half = xh.shape[1]
        a = (jnp.dot(xh, w1b[0:half, :], preferred_element_type=F32)
             + jnp.dot(xl, w1b[half:, :], preferred_element_type=F32))
        b = (jnp.dot(xh, w3b[0:half, :], preferred_element_type=F32)
             + jnp.dot(xl, w3b[half:, :], preferred_element_type=F32))
        hmid = (a * _sigmoid(a) * b).astype(BF16)
        o_ref[...] = jnp.dot(hmid, w2b[...], preferred_element_type=F32)

    @pl.when(i >= nu_ref[0])
    def _():
        o_ref[...] = jnp.zeros(o_ref.shape, o_ref.dtype)


def _expert_blocks(x_sorted, block_e, n_used, w1, w3, w2):
    rows, wd = x_sorted.shape
    n_blocks = rows // MOE_ROWS
    dm, de = w1.shape[1], w1.shape[2]
    grid_spec = pltpu.PrefetchScalarGridSpec(
        num_scalar_prefetch=2,
        grid=(n_blocks,),
        in_specs=[pl.BlockSpec((MOE_ROWS, wd), lambda i, be, nu: (i, 0)),
                  pl.BlockSpec((None, dm, de), lambda i, be, nu: (be[i], 0, 0)),
                  pl.BlockSpec((None, dm, de), lambda i, be, nu: (be[i], 0, 0)),
                  pl.BlockSpec((None, de, dm), lambda i, be, nu: (be[i], 0, 0))],
        out_specs=pl.BlockSpec((MOE_ROWS, dm), lambda i, be, nu: (i, 0)),
        scratch_shapes=[pltpu.VMEM((dm, de), BF16), pltpu.VMEM((dm, de), BF16), pltpu.VMEM((de, dm), BF16)],
    )
    return pl.pallas_call(
        _expert_kernel,
        grid_spec=grid_spec,
        out_shape=jax.ShapeDtypeStruct((rows, dm), F32),
        compiler_params=_params(1, 48),
    )(block_e, n_used, x_sorted, w1, w3, w2)


def _shared_kernel(x_ref, w1_ref, w3_ref, w2_ref, o_ref):
    x = x_ref[...]
    a = jnp.dot(x, w1_ref[...], preferred_element_type=F32)
    b = jnp.dot(x, w3_ref[...], preferred_element_type=F32)
    hmid = (a * _sigmoid(a) * b).astype(BF16)
    o_ref[...] = jnp.dot(hmid, w2_ref[...], preferred_element_type=F32)


def _shared_expert(xb, w1, w3, w2):
    m, dm = xb.shape
    de = w1.shape[1]
    tm = min(512, m)
    return pl.pallas_call(
        _shared_kernel,
        grid=(m // tm,),
        in_specs=[pl.BlockSpec((tm, dm), lambda i: (i, 0)),
                  pl.BlockSpec((dm, de), lambda i: (0, 0)),
                  pl.BlockSpec((dm, de), lambda i: (0, 0)),
                  pl.BlockSpec((de, dm), lambda i: (0, 0))],
        out_specs=pl.BlockSpec((tm, dm), lambda i: (i, 0)),
        out_shape=jax.ShapeDtypeStruct((m, dm), F32),
        compiler_params=_params(1, 32),
    )(xb, w1, w3, w2)


def _combine_kernel(dcur_ref, dnext_ref, h_ref, sh_ref, gate_ref, g_ref, b_ref, y_ref, o_ref, ybuf, sems):
    i = pl.program_id(0)
    n = pl.num_programs(0)
    tm = h_ref.shape[0]
    slot = lax.rem(i, 2)

    def copy(d_ref, s, t, k):
        return pltpu.make_async_copy(y_ref.at[pl.ds(d_ref[0, k, t], 1), :],
                                     ybuf.at[s, pl.ds(k * tm + t, 1), :], sems.at[s])

    def start_tile(d_ref, s):
        def body(t, carry):
            for k in range(TOP_K):
                copy(d_ref, s, t, k).start()
            return carry
        lax.fori_loop(0, tm, body, 0)

    @pl.when(i == 0)
    def _():
        start_tile(dcur_ref, 0)

    @pl.when(i + 1 < n)
    def _():
        start_tile(dnext_ref, 1 - slot)

    def wait_body(t, carry):
        for k in range(TOP_K):
            copy(dcur_ref, slot, t, k).wait()
        return carry
    lax.fori_loop(0, tm, wait_body, 0)

    gate = gate_ref[...]
    y = sh_ref[...]
    for k in range(TOP_K):
        y = y + ybuf[slot, k * tm:(k + 1) * tm, :] * gate[:, k:k + 1]
    o_ref[...] = _layer_norm(DN_ALPHA * h_ref[...] + y, g_ref[...], b_ref[...])


def _combine_ln(h, shared, y_sorted, dest, gate, g, b):
    m, dm = h.shape
    tm = min(64, m)
    n = m // tm
    dest3 = dest.reshape(TOP_K, n, tm).transpose(1, 0, 2)
    return pl.pallas_call(
        _combine_kernel,
        grid=(n,),
        in_specs=[pl.BlockSpec((1, TOP_K, tm), lambda i: (i, 0, 0), memory_space=pltpu.SMEM),
                  pl.BlockSpec((1, TOP_K, tm), lambda i: (jnp.minimum(i + 1, n - 1), 0, 0), memory_space=pltpu.SMEM),
                  pl.BlockSpec((tm, dm), lambda i: (i, 0)),
                  pl.BlockSpec((tm, dm), lambda i: (i, 0)),
                  pl.BlockSpec((tm, TOP_K), lambda i: (i, 0)),
                  pl.BlockSpec((1, dm), lambda i: (0, 0)),
                  pl.BlockSpec((1, dm), lambda i: (0, 0)),
                  pl.BlockSpec(memory_space=pl.ANY)],
        out_specs=pl.BlockSpec((tm, dm), lambda i: (i, 0)),
        out_shape=jax.ShapeDtypeStruct((m, dm), F32),
        scratch_shapes=[pltpu.VMEM((2, TOP_K * tm, dm), F32), pltpu.SemaphoreType.DMA((2,))],
        compiler_params=_params(1, 40),
    )(dest3, dest3, h, shared, gate, g.reshape(1, dm), b.reshape(1, dm), y_sorted)


def _moe_sublayer(h, hb, hp, router_w, router_bias, w1, w3, w2, sw1, sw3, sw2, g, b):
    n_tok, dm = h.shape
    idx_t, gate_t, rank_t, counts = _router(h, router_w, router_bias)
    counts = counts[:, 0]
    padded = (counts + MOE_ROWS - 1) // MOE_ROWS * MOE_ROWS
    pad_end = jnp.cumsum(padded)
    n_blocks = (n_tok * TOP_K + N_EXPERTS * (MOE_ROWS - 1) + MOE_ROWS - 1) // MOE_ROWS
    block_start = jnp.arange(n_blocks, dtype=jnp.int32) * MOE_ROWS
    block_e = jnp.minimum(jnp.sum((pad_end[None, :] <= block_start[:, None]).astype(jnp.int32), axis=1),
                          N_EXPERTS - 1)
    n_used = (pad_end[-1] // MOE_ROWS).astype(jnp.int32).reshape(1)
    dest = _dest_rows(idx_t, rank_t, pad_end - padded)
    x_sorted = _dispatch(hp, dest, n_blocks * MOE_ROWS)
    y_sorted = _expert_blocks(x_sorted, block_e, n_used, w1, w3, w2)
    shared = _shared_expert(hb, sw1.astype(BF16), sw3.astype(BF16), sw2.astype(BF16))
    return _combine_ln(h, shared, y_sorted, dest, gate_t.T, g, b)


def kernel(x, mem, positions, rel_bias_table, w_in, diff_lambda, diff_subln, idx_k_g, idx_k_b, kv_norm_g, w_uk, w_uv, w_proj_diff, w_proj_dsa, w_mix_out, ln1_g, ln1_b, xattn_wq, xattn_wkv, xattn_wo, ln2_g, ln2_b, router_w, router_bias, exp_w1, exp_w3, exp_w2, sh_w1, sh_w3, sh_w2, ln3_g, ln3_b):
    del positions
    batch, seq, dm = x.shape
    n_mem = mem.shape[1]
    n_tok = batch * seq
    lambda_init = 0.8 - 0.6 * math.exp(-0.3 * 0)
    blk = min(ATT_BLOCK, seq)

    dq = DIFF_HEADS * DIFF_V_DIM
    c_qs = 3 * dq
    c_ckv = c_qs + DSA_HEADS * DSA_HEAD_DIM
    c_qi = c_ckv + DSA_KV_RANK
    c_ki = c_qi + IDX_HEADS * IDX_HEAD_DIM
    c_gd = c_ki + IDX_HEAD_DIM + IDX_HEADS
    c_gs = c_gd + dm

    xf = x.reshape(n_tok, dm)
    xb = xf.astype(BF16)
    w_in0 = w_in[0]
    qkv = _matmul(xb, w_in0, 0, c_ckv, BF16, 1024, 1024)
    ckv = _matmul(xb, w_in0, c_ckv, DSA_KV_RANK, F32, 1024, 512)
    qi = _matmul(xb, w_in0, c_qi, c_ki - c_qi, BF16, 1024, 512)
    w_tail = jnp.pad(w_in0[:, c_ki:c_gd], ((0, 0), (0, 128 - (c_gd - c_ki))))
    kiw = _matmul(xb, w_tail, 0, 128, F32, 1024, 128)

    tiles = _rel_bias_tiles(rel_bias_table, blk)
    o_d = _diff_attention(qkv, tiles[:DIFF_HEADS], diff_lambda[0], diff_subln[0], batch, seq, lambda_init)

    k_sel = min(DSA_TOPK_MAX, seq // 4)
    sel = _indexer_mask(qi, kiw, idx_k_g[0], idx_k_b[0], batch, seq, k_sel, qi_col_block=0)
    o_s = _dsa_attention(qkv, c_qs // (DSA_HEADS * DSA_HEAD_DIM), ckv, 0, kv_norm_g[0], sel,
                         w_uk[0].astype(BF16), w_uv[0].astype(BF16), tiles[DIFF_HEADS:], batch, seq)

    merged = _gated_merge(xb, o_d, o_s, w_in0[:, c_gd:c_gs].astype(BF16), w_in0[:, c_gs:c_gs + dm].astype(BF16),
                          w_proj_diff[0].astype(BF16), w_proj_dsa[0].astype(BF16))
    h1, h1b = _matmul_res_ln(merged, w_mix_out[0].astype(BF16), xf, ln1_g[0], ln1_b[0])

    memb = mem.reshape(batch * n_mem, dm).astype(BF16)
    kv = _matmul(memb, xattn_wkv[0], 0, 2 * XATTN_HEADS * XATTN_HEAD_DIM, BF16, 1024, 512)
    h2, h2b, h2p = _cross_attention(h1b, h1, xattn_wq[0].astype(BF16), kv, xattn_wo[0].astype(BF16),
                                    ln2_g[0], ln2_b[0], batch, seq, n_mem)

    out = _moe_sublayer(h2, h2b, h2p, router_w[0], router_bias[0], exp_w1[0], exp_w3[0], exp_w2[0],
                        sh_w1[0], sh_w3[0], sh_w2[0], ln3_g[0], ln3_b[0])
    return out.reshape(batch, seq, dm)
```

```python
import functools
import math

import jax
import jax.numpy as jnp
from jax import lax
from jax.experimental import pallas as pl
from jax.experimental.pallas import tpu as pltpu

F32 = jnp.float32
BF16 = jnp.bfloat16
NEG = -1e30
INT_MIN = -(2 ** 31)
MIB = 1024 * 1024

LN_EPS = 1e-5
DIFF_HEADS = 8
DIFF_HEAD_DIM = 128
DIFF_V_DIM = 2 * DIFF_HEAD_DIM
DSA_HEADS = 8
DSA_HEAD_DIM = 128
DSA_KV_RANK = 512
IDX_HEADS = 16
IDX_HEAD_DIM = 64
DSA_TOPK_MAX = 256
REL_BUCKETS = 32
REL_MAX_DIST = 128
XATTN_HEADS = 4
XATTN_HEAD_DIM = 128
N_EXPERTS = 64
EXPERT_DIM = 512
TOP_K = 8
N_GROUPS = 8
TOPK_GROUPS = 4
ROUTED_SCALE = 2.5
DEPTH = 1
DN_ALPHA = (2 * DEPTH) ** 0.25

ATT_BLOCK = 256
MOE_ROWS = 512
NT_DIMS = (((1,), (1,)), ((), ()))


def _params(n_grid, vmem_mib):
    return pltpu.CompilerParams(dimension_semantics=("arbitrary",) * n_grid,
                                vmem_limit_bytes=vmem_mib * MIB)


def _sigmoid(x):
    return 1.0 / (1.0 + jnp.exp(-x))


def _layer_norm(z, g, b):
    mu = jnp.mean(z, axis=1, keepdims=True)
    zc = z - mu
    var = jnp.mean(zc * zc, axis=1, keepdims=True)
    return zc * lax.rsqrt(var + LN_EPS) * g + b


def _mm_kernel(a_ref, b_ref, o_ref, bq_ref):
    @pl.when(pl.program_id(1) == 0)
    def _():
        bq_ref[...] = b_ref[...].astype(BF16)

    o_ref[...] = jnp.dot(a_ref[...], bq_ref[...], preferred_element_type=F32).astype(o_ref.dtype)


def _matmul(a, b, col_start, n_cols, out_dtype, tm, tn):
    m, k = a.shape
    tm = min(tm, m)
    assert col_start % tn == 0 and n_cols % tn == 0 and m % tm == 0
    off = col_start // tn
    return pl.pallas_call(
        _mm_kernel,
        grid=(n_cols // tn, m // tm),
        in_specs=[pl.BlockSpec((tm, k), lambda j, i: (i, 0)),
                  pl.BlockSpec((k, tn), lambda j, i: (0, j + off))],
        out_specs=pl.BlockSpec((tm, tn), lambda j, i: (i, j)),
        out_shape=jax.ShapeDtypeStruct((m, n_cols), out_dtype),
        scratch_shapes=[pltpu.VMEM((k, tn), BF16)],
        compiler_params=_params(2, 48),
    )(a, b)


def _rel_bucket(n):
    n = jnp.maximum(n, 0)
    max_exact = REL_BUCKETS // 2
    nf = jnp.maximum(n, 1).astype(F32)
    large = max_exact + (jnp.log(nf / max_exact) / math.log(REL_MAX_DIST / max_exact)
                         * (REL_BUCKETS - max_exact)).astype(jnp.int32)
    large = jnp.minimum(large, REL_BUCKETS - 1)
    return jnp.where(n < max_exact, n, large)


def _rel_bias_tiles(table, blk):
    assert blk >= REL_MAX_DIST
    r = jnp.arange(blk, dtype=jnp.int32)[:, None]
    c = jnp.arange(blk, dtype=jnp.int32)[None, :]
    tiles = []
    for back in range(3):
        n = back * blk + r - c
        bucket = _rel_bucket(n)
        t = jnp.zeros((table.shape[1], blk, blk), F32)
        for bkt in range(REL_BUCKETS):
            t = jnp.where(bucket == bkt, table[bkt][:, None, None], t)
        tiles.append(jnp.where(n >= 0, t, NEG))
    return jnp.stack(tiles, axis=1).astype(F32)


def _lane_repeat(x, n, axis=1):
    assert axis == 1
    return x if n == 1 else jnp.concatenate([x] * n, axis=1)


def _online_softmax_step(s, v, m_ref, l_ref, acc_ref):
    tk = s.shape[1]
    e = v.shape[1]
    m_prev = m_ref[...]
    m_next = jnp.maximum(m_prev, jnp.max(s, axis=1, keepdims=True))
    p = jnp.exp(s - _lane_repeat(m_next, tk // 128, axis=1))
    alpha = jnp.exp(m_prev - m_next)
    l_ref[...] = alpha * l_ref[...] + jnp.sum(p, axis=1, keepdims=True)
    m_ref[...] = m_next
    acc_ref[...] = (acc_ref[...] * _lane_repeat(alpha, e // 128, axis=1)
                    + jnp.dot(p.astype(BF16), v, preferred_element_type=F32))


def _diff_attn_kernel(q_ref, k_ref, v_ref, bias_ref, lam_ref, g_ref, o_ref, m_scr, l_scr, acc_scr,
                      *, blk, lambda_init):
    i = pl.program_id(2)
    d = DIFF_HEAD_DIM
    scale = d ** -0.5
    m_scr[...] = jnp.full(m_scr.shape, NEG, F32)
    l_scr[...] = jnp.zeros(l_scr.shape, F32)
    acc_scr[...] = jnp.zeros(acc_scr.shape, F32)
    q = q_ref[...]

    def body(j, carry):
        start = pl.multiple_of(j * blk, blk)
        kk = k_ref[pl.ds(start, blk), :]
        vv = v_ref[pl.ds(start, blk), :]
        bt = bias_ref[jnp.minimum(i - j, 2)]
        for c in range(2):
            s = lax.dot_general(q[:, c * d:(c + 1) * d], kk[:, c * d:(c + 1) * d], NT_DIMS,
                                preferred_element_type=F32)
            _online_softmax_step(s * scale + bt, vv, m_scr.at[c], l_scr.at[c], acc_scr.at[c])
        return carry

    lax.fori_loop(0, i + 1, body, 0)

    lp = lam_ref[...]
    lam = (jnp.exp(jnp.sum(lp[0:1] * lp[1:2], axis=1, keepdims=True))
           - jnp.exp(jnp.sum(lp[2:3] * lp[3:4], axis=1, keepdims=True)) + lambda_init)
    o0 = acc_scr[0] * _lane_repeat(1.0 / l_scr[0], 2, axis=1)
    o1 = acc_scr[1] * _lane_repeat(1.0 / l_scr[1], 2, axis=1)
    o = o0 - lam * o1
    ms = jnp.mean(o * o, axis=1, keepdims=True)
    o_ref[...] = (o * lax.rsqrt(ms + LN_EPS) * g_ref[...] * (1.0 - lambda_init)).astype(o_ref.dtype)


def _diff_attention(qkv, bias_tiles, diff_lambda, subln, batch, seq, lambda_init):
    blk = min(ATT_BLOCK, seq)
    nq = seq // blk
    h2d = DIFF_V_DIM
    kern = functools.partial(_diff_attn_kernel, blk=blk, lambda_init=lambda_init)
    return pl.pallas_call(
        kern,
        grid=(batch, DIFF_HEADS, nq),
        in_specs=[pl.BlockSpec((blk, h2d), lambda b, h, i: (b * nq + i, h)),
                  pl.BlockSpec((seq, h2d), lambda b, h, i: (b, DIFF_HEADS + h)),
                  pl.BlockSpec((seq, h2d), lambda b, h, i: (b, 2 * DIFF_HEADS + h)),
                  pl.BlockSpec((None, 3, blk, blk), lambda b, h, i: (h, 0, 0, 0)),
                  pl.BlockSpec((4, DIFF_HEAD_DIM), lambda b, h, i: (0, 0)),
                  pl.BlockSpec((1, h2d), lambda b, h, i: (0, 0))],
        out_specs=pl.BlockSpec((blk, h2d), lambda b, h, i: (b * nq + i, h)),
        out_shape=jax.ShapeDtypeStruct((batch * seq, DIFF_HEADS * h2d), BF16),
        scratch_shapes=[pltpu.VMEM((2, blk, 128), F32), pltpu.VMEM((2, blk, 128), F32),
                        pltpu.VMEM((2, blk, h2d), F32)],
        compiler_params=_params(3, 32),
    )(qkv, qkv, qkv, bias_tiles, diff_lambda, subln.reshape(1, h2d))


def _indexer_kernel(qi_ref, kfull_ref, wblk_ref, g_ref, b_ref, o_ref, ki_scr, key_scr, *, blk, seq, k_sel):
    i = pl.program_id(1)

    @pl.when(i == 0)
    def _():
        kr = kfull_ref[:, 0:IDX_HEAD_DIM]
        ki_scr[...] = _layer_norm(kr, g_ref[...], b_ref[...]).astype(BF16)

    w = wblk_ref[:, IDX_HEAD_DIM:IDX_HEAD_DIM + IDX_HEADS] * (IDX_HEADS ** -0.5 * IDX_HEAD_DIM ** -0.5)
    qi = qi_ref[...].astype(BF16)
    ki = ki_scr[...]
    score = jnp.zeros((blk, seq), F32)
    for h in range(IDX_HEADS):
        lg = lax.dot_general(qi[:, h * IDX_HEAD_DIM:(h + 1) * IDX_HEAD_DIM], ki, NT_DIMS,
                             preferred_element_type=F32)
        score = score + jnp.maximum(lg, 0.0) * w[:, h:h + 1]

    row = i * blk + lax.broadcasted_iota(jnp.int32, (blk, seq), 0)
    col = lax.broadcasted_iota(jnp.int32, (blk, seq), 1)
    valid = col <= row
    bits = lax.bitcast_convert_type(score, jnp.int32)
    key = bits ^ ((bits >> 31) & jnp.int32(0x7FFFFFFF))
    key_scr[...] = jnp.where(valid, key, INT_MIN)

    def search(it, thr):
        trial = thr + lax.shift_left(jnp.int32(1), 31 - it)
        cnt = jnp.sum(jnp.where(key_scr[...] >= trial, 1.0, 0.0), axis=1, keepdims=True)
        return jnp.where(cnt >= k_sel, trial, thr)

    thr = lax.fori_loop(0, 32, search, jnp.full((blk, 1), INT_MIN, jnp.int32))
    mask = jnp.where(valid & (key_scr[...] >= thr), 0.0, NEG).astype(o_ref.dtype)
    for jj in range(seq // blk):
        o_ref[jj] = mask[:, jj * blk:(jj + 1) * blk]


def _indexer_mask(qi, kiw, idx_k_g, idx_k_b, batch, seq, k_sel, qi_col_block):
    blk = min(ATT_BLOCK, seq)
    nq = seq // blk
    qw = IDX_HEADS * IDX_HEAD_DIM
    kern = functools.partial(_indexer_kernel, blk=blk, seq=seq, k_sel=k_sel)
    return pl.pallas_call(
        kern,
        grid=(batch, nq),
        in_specs=[pl.BlockSpec((blk, qw), lambda b, i: (b * nq + i, qi_col_block)),
                  pl.BlockSpec((seq, 128), lambda b, i: (b, 0)),
                  pl.BlockSpec((blk, 128), lambda b, i: (b * nq + i, 0)),
                  pl.BlockSpec((1, IDX_HEAD_DIM), lambda b, i: (0, 0)),
                  pl.BlockSpec((1, IDX_HEAD_DIM), lambda b, i: (0, 0))],
        out_specs=pl.BlockSpec((None, nq, blk, blk), lambda b, i: (b, 0, i, 0)),
        out_shape=jax.ShapeDtypeStruct((batch, nq, seq, blk), BF16),
        scratch_shapes=[pltpu.VMEM((seq, IDX_HEAD_DIM), BF16), pltpu.VMEM((blk, seq), jnp.int32)],
        compiler_params=_params(2, 48),
    )(qi, kiw, kiw, idx_k_g.reshape(1, -1), idx_k_b.reshape(1, -1))


def _dsa_attn_kernel(qs_ref, ckv_ref, g_ref, sel_ref, wuk_ref, wuv_ref, bias_ref, o_ref,
                     c_scr, ql_scr, sb_scr, m_scr, l_scr, acc_scr, *, blk):
    i = pl.program_id(1)
    d = DSA_HEAD_DIM
    scale = d ** -0.5

    @pl.when(i == 0)
    def _():
        x = ckv_ref[...]
        c = x * lax.rsqrt(jnp.mean(x * x, axis=1, keepdims=True) + LN_EPS) * g_ref[...]
        c_scr[...] = c.astype(BF16)

    qs = qs_ref[...]
    for h in range(DSA_HEADS):
        ql = jnp.dot(qs[:, h * d:(h + 1) * d], wuk_ref[h], preferred_element_type=F32)
        ql_scr[h * blk:(h + 1) * blk, :] = (ql * scale).astype(BF16)

    m_scr[...] = jnp.full(m_scr.shape, NEG, F32)
    l_scr[...] = jnp.zeros(l_scr.shape, F32)
    acc_scr[...] = jnp.zeros(acc_scr.shape, F32)

    def body(j, carry):
        start = pl.multiple_of(j * blk, blk)
        cc = c_scr[pl.ds(start, blk), :]
        s = lax.dot_general(ql_scr[...], cc, NT_DIMS, preferred_element_type=F32)
        selm = sel_ref[j].astype(F32)
        back = jnp.minimum(i - j, 2)
        for h in range(DSA_HEADS):
            sb_scr[h * blk:(h + 1) * blk, :] = bias_ref[h, back] + selm
        _online_softmax_step(s + sb_scr[...], cc, m_scr, l_scr, acc_scr)
        return carry

    lax.fori_loop(0, i + 1, body, 0)

    o_lat = (acc_scr[...] * _lane_repeat(1.0 / l_scr[...], DSA_KV_RANK // 128, axis=1)).astype(BF16)
    for h in range(DSA_HEADS):
        o_ref[:, h * d:(h + 1) * d] = jnp.dot(o_lat[h * blk:(h + 1) * blk], wuv_ref[h],
                                              preferred_element_type=F32).astype(o_ref.dtype)


def _dsa_attention(qkv, qs_col_block, ckv, ckv_col_block, kv_norm_g, sel, w_uk, w_uv, bias_tiles, batch, seq):
    blk = min(ATT_BLOCK, seq)
    nq = seq // blk
    hd = DSA_HEADS * DSA_HEAD_DIM
    r = DSA_KV_RANK
    rows = DSA_HEADS * blk
    kern = functools.partial(_dsa_attn_kernel, blk=blk)
    return pl.pallas_call(
        kern,
        grid=(batch, nq),
        in_specs=[pl.BlockSpec((blk, hd), lambda b, i: (b * nq + i, qs_col_block)),
                  pl.BlockSpec((seq, r), lambda b, i: (b, ckv_col_block)),
                  pl.BlockSpec((1, r), lambda b, i: (0, 0)),
                  pl.BlockSpec((None, nq, blk, blk), lambda b, i: (b, 0, i, 0)),
                  pl.BlockSpec((DSA_HEADS, DSA_HEAD_DIM, r), lambda b, i: (0, 0, 0)),
                  pl.BlockSpec((DSA_HEADS, r, DSA_HEAD_DIM), lambda b, i: (0, 0, 0)),
                  pl.BlockSpec((DSA_HEADS, 3, blk, blk), lambda b, i: (0, 0, 0, 0))],
        out_specs=pl.BlockSpec((blk, hd), lambda b, i: (b * nq + i, 0)),
        out_shape=jax.ShapeDtypeStruct((batch * seq, hd), BF16),
        scratch_shapes=[pltpu.VMEM((seq, r), BF16), pltpu.VMEM((rows, r), BF16), pltpu.VMEM((rows, blk), F32),
                        pltpu.VMEM((rows, 128), F32), pltpu.VMEM((rows, 128), F32), pltpu.VMEM((rows, r), F32)],
        compiler_params=_params(2, 56),
    )(qkv, ckv, kv_norm_g.reshape(1, r), sel, w_uk, w_uv, bias_tiles)


def _merge_kernel(x_ref, od_ref, os_ref, wgd_ref, wgs_ref, wpd_ref, wps_ref, o_ref):
    x = x_ref[...]
    gd = jnp.dot(x, wgd_ref[...], preferred_element_type=F32)
    gs = jnp.dot(x, wgs_ref[...], preferred_element_type=F32)
    pd = jnp.dot(od_ref[...], wpd_ref[...], preferred_element_type=F32)
    ps = jnp.dot(os_ref[...], wps_ref[...], preferred_element_type=F32)
    o_ref[...] = (_sigmoid(gd) * pd + _sigmoid(gs) * ps).astype(o_ref.dtype)


def _gated_merge(xb, o_d, o_s, wgd, wgs, wpd, wps):
    m, dm = xb.shape
    tm = min(512, m)
    tn = 512
    kd, ks = o_d.shape[1], o_s.shape[1]
    return pl.pallas_call(
        _merge_kernel,
        grid=(dm // tn, m // tm),
        in_specs=[pl.BlockSpec((tm, dm), lambda j, i: (i, 0)),
                  pl.BlockSpec((tm, kd), lambda j, i: (i, 0)),
                  pl.BlockSpec((tm, ks), lambda j, i: (i, 0)),
                  pl.BlockSpec((dm, tn), lambda j, i: (0, j)),
                  pl.BlockSpec((dm, tn), lambda j, i: (0, j)),
                  pl.BlockSpec((kd, tn), lambda j, i: (0, j)),
                  pl.BlockSpec((ks, tn), lambda j, i: (0, j))],
        out_specs=pl.BlockSpec((tm, tn), lambda j, i: (i, j)),
        out_shape=jax.ShapeDtypeStruct((m, dm), BF16),
        compiler_params=_params(2, 40),
    )(xb, o_d, o_s, wgd, wgs, wpd, wps)


def _mm_res_ln_kernel(a_ref, w_ref, res_ref, g_ref, b_ref, o_ref, ob_ref):
    y = jnp.dot(a_ref[...], w_ref[...], preferred_element_type=F32)
    zn = _layer_norm(DN_ALPHA * res_ref[...] + y, g_ref[...], b_ref[...])
    o_ref[...] = zn
    ob_ref[...] = zn.astype(BF16)


def _matmul_res_ln(a, w, res, g, b):
    m, k = a.shape
    n = w.shape[1]
    tm = min(256, m)
    return pl.pallas_call(
        _mm_res_ln_kernel,
        grid=(m // tm,),
        in_specs=[pl.BlockSpec((tm, k), lambda i: (i, 0)),
                  pl.BlockSpec((k, n), lambda i: (0, 0)),
                  pl.BlockSpec((tm, n), lambda i: (i, 0)),
                  pl.BlockSpec((1, n), lambda i: (0, 0)),
                  pl.BlockSpec((1, n), lambda i: (0, 0))],
        out_specs=[pl.BlockSpec((tm, n), lambda i: (i, 0)), pl.BlockSpec((tm, n), lambda i: (i, 0))],
        out_shape=[jax.ShapeDtypeStruct((m, n), F32), jax.ShapeDtypeStruct((m, n), BF16)],
        compiler_params=_params(1, 40),
    )(a, w, res, g.reshape(1, n), b.reshape(1, n))


def _pack_bf16_pairs(zb):
    half = zb.shape[1] // 2
    bits = lax.bitcast_convert_type(zb.astype(F32), jnp.uint32)
    return (bits[:, :half] & jnp.uint32(0xFFFF0000)) | (bits[:, half:] >> 16)


def _unpack_bf16_pairs(w):
    hi = lax.bitcast_convert_type(w & jnp.uint32(0xFFFF0000), F32).astype(BF16)
    lo = lax.bitcast_convert_type(w << 16, F32).astype(BF16)
    return hi, lo


def _xattn_kernel(hb_ref, h_ref, wq_ref, kv_ref, wo_ref, g_ref, b_ref, o_ref, ob_ref, op_ref):
    d = XATTN_HEAD_DIM
    hd = XATTN_HEADS * d
    scale = d ** -0.5
    qb = jnp.dot(hb_ref[...], wq_ref[...], preferred_element_type=F32).astype(BF16)
    kv = kv_ref[...]
    outs = []
    for h in range(XATTN_HEADS):
        s = lax.dot_general(qb[:, h * d:(h + 1) * d], kv[:, h * d:(h + 1) * d], NT_DIMS,
                            preferred_element_type=F32) * scale
        p = jnp.exp(s - jnp.max(s, axis=1, keepdims=True))
        l = jnp.sum(p, axis=1, keepdims=True)
        oh = jnp.dot(p.astype(BF16), kv[:, hd + h * d:hd + (h + 1) * d], preferred_element_type=F32)
        outs.append((oh / l).astype(BF16))
    o = jnp.concatenate(outs, axis=1)
    y = jnp.dot(o, wo_ref[...], preferred_element_type=F32)
    zn = _layer_norm(DN_ALPHA * h_ref[...] + y, g_ref[...], b_ref[...])
    zb = zn.astype(BF16)
    o_ref[...] = zn
    ob_ref[...] = zb
    op_ref[...] = _pack_bf16_pairs(zb)


def _cross_attention(hb, h, wq, kv, wo, g, b, batch, seq, n_mem):
    m, dm = h.shape
    tm = min(256, seq)
    nq = seq // tm
    hd = XATTN_HEADS * XATTN_HEAD_DIM
    return pl.pallas_call(
        _xattn_kernel,
        grid=(batch, nq),
        in_specs=[pl.BlockSpec((tm, dm), lambda bb, i: (bb * nq + i, 0)),
                  pl.BlockSpec((tm, dm), lambda bb, i: (bb * nq + i, 0)),
                  pl.BlockSpec((dm, hd), lambda bb, i: (0, 0)),
                  pl.BlockSpec((n_mem, 2 * hd), lambda bb, i: (bb, 0)),
                  pl.BlockSpec((hd, dm), lambda bb, i: (0, 0)),
                  pl.BlockSpec((1, dm), lambda bb, i: (0, 0)),
                  pl.BlockSpec((1, dm), lambda bb, i: (0, 0))],
        out_specs=[pl.BlockSpec((tm, dm), lambda bb, i: (bb * nq + i, 0)),
                   pl.BlockSpec((tm, dm), lambda bb, i: (bb * nq + i, 0)),
                   pl.BlockSpec((tm, dm // 2), lambda bb, i: (bb * nq + i, 0))],
        out_shape=[jax.ShapeDtypeStruct((m, dm), F32), jax.ShapeDtypeStruct((m, dm), BF16),
                   jax.ShapeDtypeStruct((m, dm // 2), jnp.uint32)],
        compiler_params=_params(2, 40),
    )(hb, h, wq, kv, wo, g.reshape(1, dm), b.reshape(1, dm))


def _split_bf16(x):
    hi = x.astype(BF16)
    return hi, (x - hi.astype(F32)).astype(BF16)


def _router_kernel(h_ref, rwt_ref, rb_ref, idx_ref, gate_ref, rank_ref, cnt_ref, cnt_scr):
    tm = h_ref.shape[0]
    per = N_EXPERTS // N_GROUPS
    h_hi, h_lo = _split_bf16(h_ref[...])
    w_hi, w_lo = _split_bf16(rwt_ref[...])
    logits = (lax.dot_general(w_hi, h_hi, NT_DIMS, preferred_element_type=F32)
              + lax.dot_general(w_hi, h_lo, NT_DIMS, preferred_element_type=F32)
              + lax.dot_general(w_lo, h_hi, NT_DIMS, preferred_element_type=F32))
    scores = _sigmoid(logits)
    sel = scores + rb_ref[...]

    sel3 = sel.reshape(N_GROUPS, per, tm)
    r_iota = lax.broadcasted_iota(jnp.int32, sel3.shape, 1).astype(F32)
    m1 = jnp.max(sel3, axis=1, keepdims=True)
    first = jnp.min(jnp.where(sel3 == m1, r_iota, float(per)), axis=1, keepdims=True)
    m2 = jnp.max(jnp.where(r_iota == first, -jnp.inf, sel3), axis=1, keepdims=True)
    gscore = (m1 + m2).reshape(N_GROUPS, tm)

    g_iota = lax.broadcasted_iota(jnp.int32, gscore.shape, 0).astype(F32)
    keep = jnp.zeros(gscore.shape, F32)
    cur = gscore
    for _ in range(TOPK_GROUPS):
        m = jnp.max(cur, axis=0, keepdims=True)
        first = jnp.min(jnp.where(cur == m, g_iota, float(N_GROUPS)), axis=0, keepdims=True)
        pick = g_iota == first
        keep = jnp.where(pick, 1.0, keep)
        cur = jnp.where(pick, -jnp.inf, cur)
    keep3 = jnp.broadcast_to(keep.reshape(N_GROUPS, 1, tm), sel3.shape)
    cur = jnp.where(keep3 > 0.5, sel3, -jnp.inf).reshape(N_EXPERTS, tm)

    e_iota = lax.broadcasted_iota(jnp.int32, cur.shape, 0).astype(F32)
    gsum = jnp.zeros((1, tm), F32)
    gates, picks = [], []
    for k in range(TOP_K):
        m = jnp.max(cur, axis=0, keepdims=True)
        first = jnp.min(jnp.where(cur == m, e_iota, float(N_EXPERTS)), axis=0, keepdims=True)
        pick = e_iota == first
        gk = jnp.sum(jnp.where(pick, scores, 0.0), axis=0, keepdims=True)
        idx_ref[k:k + 1, :] = first.astype(jnp.int32)
        gates.append(gk)
        picks.append(pick)
        gsum = gsum + gk
        cur = jnp.where(pick, -jnp.inf, cur)
    for k in range(TOP_K):
        gate_ref[k:k + 1, :] = gates[k] / gsum * ROUTED_SCALE

    @pl.when(pl.program_id(0) == 0)
    def _():
        cnt_scr[...] = jnp.zeros(cnt_scr.shape, F32)

    mask = jnp.zeros(cur.shape, F32)
    for k in range(TOP_K):
        mask = jnp.where(picks[k], 1.0, mask)
    upper = (lax.broadcasted_iota(jnp.int32, (tm, tm), 0) <= lax.broadcasted_iota(jnp.int32, (tm, tm), 1))
    cum = jnp.dot(mask.astype(BF16), jnp.where(upper, 1.0, 0.0).astype(BF16), preferred_element_type=F32)
    before = cnt_scr[...][:, 0:1] + cum - mask
    for k in range(TOP_K):
        rank_ref[k:k + 1, :] = jnp.sum(jnp.where(picks[k], before, 0.0), axis=0, keepdims=True).astype(jnp.int32)
    cnt_scr[...] = cnt_scr[...] + jnp.sum(mask, axis=1, keepdims=True)
    cnt_ref[...] = cnt_scr[...].astype(jnp.int32)


def _router(h, router_w, router_bias):
    m, dm = h.shape
    tm = min(512, m)
    return pl.pallas_call(
        _router_kernel,
        grid=(m // tm,),
        in_specs=[pl.BlockSpec((tm, dm), lambda i: (i, 0)),
                  pl.BlockSpec((N_EXPERTS, dm), lambda i: (0, 0)),
                  pl.BlockSpec((N_EXPERTS, 1), lambda i: (0, 0))],
        out_specs=[pl.BlockSpec((TOP_K, tm), lambda i: (0, i)), pl.BlockSpec((TOP_K, tm), lambda i: (0, i)),
                   pl.BlockSpec((TOP_K, tm), lambda i: (0, i)), pl.BlockSpec((N_EXPERTS, 128), lambda i: (0, 0))],
        out_shape=[jax.ShapeDtypeStruct((TOP_K, m), jnp.int32), jax.ShapeDtypeStruct((TOP_K, m), F32),
                   jax.ShapeDtypeStruct((TOP_K, m), jnp.int32), jax.ShapeDtypeStruct((N_EXPERTS, 128), jnp.int32)],
        scratch_shapes=[pltpu.VMEM((N_EXPERTS, 128), F32)],
        compiler_params=_params(1, 32),
    )(h, router_w.T, router_bias.reshape(N_EXPERTS, 1))


def _dest_kernel(idx_ref, rank_ref, ps_ref, dest_ref):
    tm = idx_ref.shape[1]
    e_iota = lax.broadcasted_iota(jnp.int32, (N_EXPERTS, tm), 0)
    ps = ps_ref[...][:, 0:1]
    for k in range(TOP_K):
        base = jnp.sum(jnp.where(e_iota == idx_ref[k:k + 1, :], ps, 0), axis=0, keepdims=True)
        dest_ref[k:k + 1, :] = rank_ref[k:k + 1, :] + base


def _dest_rows(idx_t, rank_t, pad_start):
    m = idx_t.shape[1]
    tm = min(2048, m)
    ps = jnp.broadcast_to(pad_start.astype(F32).reshape(N_EXPERTS, 1), (N_EXPERTS, 128))
    out = pl.pallas_call(
        _dest_kernel,
        grid=(m // tm,),
        in_specs=[pl.BlockSpec((TOP_K, tm), lambda i: (0, i)), pl.BlockSpec((TOP_K, tm), lambda i: (0, i)),
                  pl.BlockSpec((N_EXPERTS, 128), lambda i: (0, 0))],
        out_specs=pl.BlockSpec((TOP_K, tm), lambda i: (0, i)),
        out_shape=jax.ShapeDtypeStruct((TOP_K, m), F32),
        compiler_params=_params(1, 32),
    )(idx_t, rank_t.astype(F32), ps)
    return out.astype(jnp.int32)


def _dispatch_kernel(dest_ref, x_ref, init_ref, xs_ref, sem):
    del init_ref
    tm = x_ref.shape[0]

    def copy(t, k):
        return pltpu.make_async_copy(x_ref.at[pl.ds(t, 1), :], xs_ref.at[pl.ds(dest_ref[0, k, t], 1), :], sem)

    def start(t, carry):
        for k in range(TOP_K):
            copy(t, k).start()
        return carry

    lax.fori_loop(0, tm, start, 0)
    n_rows = TOP_K * tm
    pltpu.make_async_copy(xs_ref.at[pl.ds(0, n_rows), :], xs_ref.at[pl.ds(n_rows, n_rows), :], sem).wait()


def _dispatch(xp, dest, rows):
    m, wd = xp.shape
    tm = min(512, m)
    dest3 = dest.reshape(TOP_K, m // tm, tm).transpose(1, 0, 2)
    return pl.pallas_call(
        _dispatch_kernel,
        grid=(m // tm,),
        in_specs=[pl.BlockSpec((1, TOP_K, tm), lambda i: (i, 0, 0), memory_space=pltpu.SMEM),
                  pl.BlockSpec((tm, wd), lambda i: (i, 0)),
                  pl.BlockSpec(memory_space=pl.ANY)],
        out_specs=pl.BlockSpec(memory_space=pl.ANY),
        out_shape=jax.ShapeDtypeStruct((rows, wd), xp.dtype),
        scratch_shapes=[pltpu.SemaphoreType.DMA(())],
        input_output_aliases={2: 0},
        compiler_params=_params(1, 32),
    )(dest3, xp, jnp.zeros((rows, wd), xp.dtype))


def _expert_kernel(be_ref, nu_ref, x_ref, w1_ref, w3_ref, w2_ref, o_ref, w1b, w3b, w2b):
    i = pl.program_id(0)
    changed = jnp.logical_or(i == 0, be_ref[i] != be_ref[jnp.maximum(i - 1, 0)])

    @pl.when(changed)
    def _():
        w1b[...] = w1_ref[...].astype(BF16)
        w3b[...] = w3_ref[...].astype(BF16)
        w2b[...] = w2_ref[...].astype(BF16)

    @pl.when(i < nu_ref[0])
    def _():
        xh, xl = _unpack_bf16_pairs(x_ref[...])
        half = xh.shape[1]
        a = (jnp.dot(xh, w1b[0:half, :], preferred_element_type=F32)
             + jnp.dot(xl, w1b[half:, :], preferred_element_type=F32))
        b = (jnp.dot(xh, w3b[0:half, :], preferred_element_type=F32)
             + jnp.dot(xl, w3b[half:, :], preferred_element_type=F32))
        hmid = (a * _sigmoid(a) * b).astype(BF16)
        o_ref[...] = jnp.dot(hmid, w2b[...], preferred_element_type=F32)

    @pl.when(i >= nu_ref[0])
    def _():
        o_ref[...] = jnp.zeros(o_ref.shape, o_ref.dtype)


def _expert_blocks(x_sorted, block_e, n_used, w1, w3, w2):
    rows, wd = x_sorted.shape
    n_blocks = rows // MOE_ROWS
    dm, de = w1.shape[1], w1.shape[2]
    grid_spec = pltpu.PrefetchScalarGridSpec(
        num_scalar_prefetch=2,
        grid=(n_blocks,),
        in_specs=[pl.BlockSpec((MOE_ROWS, wd), lambda i, be, nu: (i, 0)),
                  pl.BlockSpec((None, dm, de), lambda i, be, nu: (be[i], 0, 0)),
                  pl.BlockSpec((None, dm, de), lambda i, be, nu: (be[i], 0, 0)),
                  pl.BlockSpec((None, de, dm), lambda i, be, nu: (be[i], 0, 0))],
        out_specs=pl.BlockSpec((MOE_ROWS, dm), lambda i, be, nu: (i, 0)),
        scratch_shapes=[pltpu.VMEM((dm, de), BF16), pltpu.VMEM((dm, de), BF16), pltpu.VMEM((de, dm), BF16)],
    )
    return pl.pallas_call(
        _expert_kernel,
        grid_spec=grid_spec,
        out_shape=jax.ShapeDtypeStruct((rows, dm), F32),
        compiler_params=_params(1, 56),
    )(block_e, n_used, x_sorted, w1, w3, w2)


def _shared_kernel(x_ref, w1_ref, w3_ref, w2_ref, o_ref):
    x = x_ref[...]
    a = jnp.dot(x, w1_ref[...], preferred_element_type=F32)
    b = jnp.dot(x, w3_ref[...], preferred_element_type=F32)
    hmid = (a * _sigmoid(a) * b).astype(BF16)
    o_ref[...] = jnp.dot(hmid, w2_ref[...], preferred_element_type=F32)


def _shared_expert(xb, w1, w3, w2):
    m, dm = xb.shape
    de = w1.shape[1]
    tm = min(512, m)
    return pl.pallas_call(
        _shared_kernel,
        grid=(m // tm,),
        in_specs=[pl.BlockSpec((tm, dm), lambda i: (i, 0)),
                  pl.BlockSpec((dm, de), lambda i: (0, 0)),
                  pl.BlockSpec((dm, de), lambda i: (0, 0)),
                  pl.BlockSpec((de, dm), lambda i: (0, 0))],
        out_specs=pl.BlockSpec((tm, dm), lambda i: (i, 0)),
        out_shape=jax.ShapeDtypeStruct((m, dm), F32),
        compiler_params=_params(1, 32),
    )(xb, w1, w3, w2)


def _combine_kernel(dcur_ref, dnext_ref, h_ref, sh_ref, gate_ref, g_ref, b_ref, y_ref, o_ref, ybuf, sems):
    i = pl.program_id(0)
    n = pl.num_programs(0)
    tm = h_ref.shape[0]
    slot = lax.rem(i, 2)

    def copy(d_ref, s, t, k):
        return pltpu.make_async_copy(y_ref.at[pl.ds(d_ref[0, k, t], 1), :],
                                     ybuf.at[s, pl.ds(k * tm + t, 1), :], sems.at[s])

    def start_tile(d_ref, s):
        def body(t, carry):
            for k in range(TOP_K):
                copy(d_ref, s, t, k).start()
            return carry
        lax.fori_loop(0, tm, body, 0)

    @pl.when(i == 0)
    def _():
        start_tile(dcur_ref, 0)

    @pl.when(i + 1 < n)
    def _():
        start_tile(dnext_ref, 1 - slot)

    pltpu.make_async_copy(y_ref.at[pl.ds(0, TOP_K * tm), :], ybuf.at[slot], sems.at[slot]).wait()

    gate = gate_ref[...]
    y = sh_ref[...]
    for k in range(TOP_K):
        y = y + ybuf[slot, k * tm:(k + 1) * tm, :] * gate[:, k:k + 1]
    o_ref[...] = _layer_norm(DN_ALPHA * h_ref[...] + y, g_ref[...], b_ref[...])


def _combine_ln(h, shared, y_sorted, dest, gate, g, b):
    m, dm = h.shape
    tm = min(64, m)
    n = m // tm
    dest3 = dest.reshape(TOP_K, n, tm).transpose(1, 0, 2)
    return pl.pallas_call(
        _combine_kernel,
        grid=(n,),
        in_specs=[pl.BlockSpec((1, TOP_K, tm), lambda i: (i, 0, 0), memory_space=pltpu.SMEM),
                  pl.BlockSpec((1, TOP_K, tm), lambda i: (jnp.minimum(i + 1, n - 1), 0, 0), memory_space=pltpu.SMEM),
                  pl.BlockSpec((tm, dm), lambda i: (i, 0)),
                  pl.BlockSpec((tm, dm), lambda i: (i, 0)),
                  pl.BlockSpec((tm, TOP_K), lambda i: (i, 0)),
                  pl.BlockSpec((1, dm), lambda i: (0, 0)),
                  pl.BlockSpec((1, dm), lambda i: (0, 0)),
                  pl.BlockSpec(memory_space=pl.ANY)],
        out_specs=pl.BlockSpec((tm, dm), lambda i: (i, 0)),
        out_shape=jax.ShapeDtypeStruct((m, dm), F32),
        scratch_shapes=[pltpu.VMEM((2, TOP_K * tm, dm), F32), pltpu.SemaphoreType.DMA((2,))],
        compiler_params=_params(1, 40),
    )(dest3, dest3, h, shared, gate, g.reshape(1, dm), b.reshape(1, dm), y_sorted)


def _moe_sublayer(h, hb, hp, router_w, router_bias, w1, w3, w2, sw1, sw3, sw2, g, b):
    n_tok, dm = h.shape
    idx_t, gate_t, rank_t, counts = _router(h, router_w, router_bias)
    counts = counts[:, 0]
    padded = (counts + MOE_ROWS - 1) // MOE_ROWS * MOE_ROWS
    pad_end = jnp.cumsum(padded)
    n_blocks = (n_tok * TOP_K + N_EXPERTS * (MOE_ROWS - 1) + MOE_ROWS - 1) // MOE_ROWS
    block_start = jnp.arange(n_blocks, dtype=jnp.int32) * MOE_ROWS
    block_e = jnp.minimum(jnp.sum((pad_end[None, :] <= block_start[:, None]).astype(jnp.int32), axis=1),
                          N_EXPERTS - 1)
    n_used = (pad_end[-1] // MOE_ROWS).astype(jnp.int32).reshape(1)
    dest = _dest_rows(idx_t, rank_t, pad_end - padded)
    x_sorted = _dispatch(hp, dest, n_blocks * MOE_ROWS)
    y_sorted = _expert_blocks(x_sorted, block_e, n_used, w1, w3, w2)
    shared = _shared_expert(hb, sw1.astype(BF16), sw3.astype(BF16), sw2.astype(BF16))
    return _combine_ln(h, shared, y_sorted, dest, gate_t.T, g, b)


def kernel(x, mem, positions, rel_bias_table, w_in, diff_lambda, diff_subln, idx_k_g, idx_k_b, kv_norm_g, w_uk, w_uv, w_proj_diff, w_proj_dsa, w_mix_out, ln1_g, ln1_b, xattn_wq, xattn_wkv, xattn_wo, ln2_g, ln2_b, router_w, router_bias, exp_w1, exp_w3, exp_w2, sh_w1, sh_w3, sh_w2, ln3_g, ln3_b):
    del positions
    batch, seq, dm = x.shape
    n_mem = mem.shape[1]
    n_tok = batch * seq
    lambda_init = 0.8 - 0.6 * math.exp(-0.3 * 0)
    blk = min(ATT_BLOCK, seq)

    dq = DIFF_HEADS * DIFF_V_DIM
    c_qs = 3 * dq
    c_ckv = c_qs + DSA_HEADS * DSA_HEAD_DIM
    c_qi = c_ckv + DSA_KV_RANK
    c_ki = c_qi + IDX_HEADS * IDX_HEAD_DIM
    c_gd = c_ki + IDX_HEAD_DIM + IDX_HEADS
    c_gs = c_gd + dm

    xf = x.reshape(n_tok, dm)
    xb = xf.astype(BF16)
    w_in0 = w_in[0]
    qkv = _matmul(xb, w_in0, 0, c_ckv, BF16, 1024, 1024)
    ckv = _matmul(xb, w_in0, c_ckv, DSA_KV_RANK, F32, 1024, 512)
    qi = _matmul(xb, w_in0, c_qi, c_ki - c_qi, BF16, 1024, 512)
    w_tail = jnp.pad(w_in0[:, c_ki:c_gd], ((0, 0), (0, 128 - (c_gd - c_ki))))
    kiw = _matmul(xb, w_tail, 0, 128, F32, 1024, 128)

    tiles = _rel_bias_tiles(rel_bias_table, blk)
    o_d = _diff_attention(qkv, tiles[:DIFF_HEADS], diff_lambda[0], diff_subln[0], batch, seq, lambda_init)

    k_sel = min(DSA_TOPK_MAX, seq // 4)
    sel = _indexer_mask(qi, kiw, idx_k_g[0], idx_k_b[0], batch, seq, k_sel, qi_col_block=0)
    o_s = _dsa_attention(qkv, c_qs // (DSA_HEADS * DSA_HEAD_DIM), ckv, 0, kv_norm_g[0], sel,
                         w_uk[0].astype(BF16), w_uv[0].astype(BF16), tiles[DIFF_HEADS:], batch, seq)

    merged = _gated_merge(xb, o_d, o_s, w_in0[:, c_gd:c_gs].astype(BF16), w_in0[:, c_gs:c_gs + dm].astype(BF16),
                          w_proj_diff[0].astype(BF16), w_proj_dsa[0].astype(BF16))
    h1, h1b = _matmul_res_ln(merged, w_mix_out[0].astype(BF16), xf, ln1_g[0], ln1_b[0])

    memb = mem.reshape(batch * n_mem, dm).astype(BF16)
    kv = _matmul(memb, xattn_wkv[0], 0, 2 * XATTN_HEADS * XATTN_HEAD_DIM, BF16, 1024, 512)
    h2, h2b, h2p = _cross_attention(h1b, h1, xattn_wq[0].astype(BF16), kv, xattn_wo[0].astype(BF16),
                                    ln2_g[0], ln2_b[0], batch, seq, n_mem)

    out = _moe_sublayer(h2, h2b, h2p, router_w[0], router_bias[0], exp_w1[0], exp_w3[0], exp_w2[0],
                        sh_w1[0], sh_w3[0], sh_w2[0], ln3_g[0], ln3_b[0])
    return out.reshape(batch, seq, dm)
```

```python
import functools
import math

import jax
import jax.numpy as jnp
from jax import lax
from jax.experimental import pallas as pl
from jax.experimental.pallas import tpu as pltpu

F32 = jnp.float32
BF16 = jnp.bfloat16
NEG = -1e30
INT_MIN = -(2 ** 31)
MIB = 1024 * 1024

LN_EPS = 1e-5
DIFF_HEADS = 8
DIFF_HEAD_DIM = 128
DIFF_V_DIM = 2 * DIFF_HEAD_DIM
DSA_HEADS = 8
DSA_HEAD_DIM = 128
DSA_KV_RANK = 512
IDX_HEADS = 16
IDX_HEAD_DIM = 64
DSA_TOPK_MAX = 256
REL_BUCKETS = 32
REL_MAX_DIST = 128
XATTN_HEADS = 4
XATTN_HEAD_DIM = 128
N_EXPERTS = 64
EXPERT_DIM = 512
TOP_K = 8
N_GROUPS = 8
TOPK_GROUPS = 4
ROUTED_SCALE = 2.5
DEPTH = 1
DN_ALPHA = (2 * DEPTH) ** 0.25

ATT_BLOCK = 256
DIFF_BLOCK = 512
MOE_ROWS = 256
NT_DIMS = (((1,), (1,)), ((), ()))


def _params(n_grid, vmem_mib):
    return pltpu.CompilerParams(dimension_semantics=("arbitrary",) * n_grid,
                                vmem_limit_bytes=vmem_mib * MIB)


def _sigmoid(x):
    return 1.0 / (1.0 + jnp.exp(-x))


def _layer_norm(z, g, b):
    mu = jnp.mean(z, axis=1, keepdims=True)
    zc = z - mu
    var = jnp.mean(zc * zc, axis=1, keepdims=True)
    return zc * lax.rsqrt(var + LN_EPS) * g + b


def _mm_kernel(a_ref, b_ref, o_ref, bq_ref):
    @pl.when(pl.program_id(1) == 0)
    def _():
        bq_ref[...] = b_ref[...].astype(BF16)

    o_ref[...] = jnp.dot(a_ref[...], bq_ref[...], preferred_element_type=F32).astype(o_ref.dtype)


def _matmul(a, b, col_start, n_cols, out_dtype, tm, tn):
    m, k = a.shape
    tm = min(tm, m)
    assert col_start % tn == 0 and n_cols % tn == 0 and m % tm == 0
    off = col_start // tn
    return pl.pallas_call(
        _mm_kernel,
        grid=(n_cols // tn, m // tm),
        in_specs=[pl.BlockSpec((tm, k), lambda j, i: (i, 0)),
                  pl.BlockSpec((k, tn), lambda j, i: (0, j + off))],
        out_specs=pl.BlockSpec((tm, tn), lambda j, i: (i, j)),
        out_shape=jax.ShapeDtypeStruct((m, n_cols), out_dtype),
        scratch_shapes=[pltpu.VMEM((k, tn), BF16)],
        compiler_params=_params(2, 48),
    )(a, b)


def _rel_bucket(n):
    n = jnp.maximum(n, 0)
    max_exact = REL_BUCKETS // 2
    nf = jnp.maximum(n, 1).astype(F32)
    large = max_exact + (jnp.log(nf / max_exact) / math.log(REL_MAX_DIST / max_exact)
                         * (REL_BUCKETS - max_exact)).astype(jnp.int32)
    large = jnp.minimum(large, REL_BUCKETS - 1)
    return jnp.where(n < max_exact, n, large)


def _rel_bias_tiles(table, blk):
    assert blk >= REL_MAX_DIST
    r = jnp.arange(blk, dtype=jnp.int32)[:, None]
    c = jnp.arange(blk, dtype=jnp.int32)[None, :]
    tiles = []
    for back in range(3):
        n = back * blk + r - c
        bucket = _rel_bucket(n)
        t = jnp.zeros((table.shape[1], blk, blk), F32)
        for bkt in range(REL_BUCKETS):
            t = jnp.where(bucket == bkt, table[bkt][:, None, None], t)
        tiles.append(jnp.where(n >= 0, t, NEG))
    return jnp.stack(tiles, axis=1).astype(F32)


def _lane_repeat(x, n, axis=1):
    assert axis == 1
    return x if n == 1 else jnp.concatenate([x] * n, axis=1)


def _online_softmax_step(s, v, m_ref, l_ref, acc_ref):
    tk = s.shape[1]
    e = v.shape[1]
    m_prev = m_ref[...]
    m_next = jnp.maximum(m_prev, jnp.max(s, axis=1, keepdims=True))
    p = jnp.exp(s - _lane_repeat(m_next, tk // 128, axis=1))
    alpha = jnp.exp(m_prev - m_next)
    l_ref[...] = alpha * l_ref[...] + jnp.sum(p, axis=1, keepdims=True)
    m_ref[...] = m_next
    acc_ref[...] = (acc_ref[...] * _lane_repeat(alpha, e // 128, axis=1)
                    + jnp.dot(p.astype(BF16), v, preferred_element_type=F32))


def _diff_attn_kernel(q_ref, k_ref, v_ref, bias_ref, lam_ref, g_ref, o_ref, m_scr, l_scr, acc_scr,
                      *, blk, lambda_init):
    i = pl.program_id(2)
    d = DIFF_HEAD_DIM
    scale = d ** -0.5
    m_scr[...] = jnp.full(m_scr.shape, NEG, F32)
    l_scr[...] = jnp.zeros(l_scr.shape, F32)
    acc_scr[...] = jnp.zeros(acc_scr.shape, F32)
    q = q_ref[...]

    def body(j, carry):
        start = pl.multiple_of(j * blk, blk)
        kk = k_ref[pl.ds(start, blk), :]
        vv = v_ref[pl.ds(start, blk), :]
        bt = bias_ref[jnp.minimum(i - j, 2)]
        for c in range(2):
            s = lax.dot_general(q[:, c * d:(c + 1) * d], kk[:, c * d:(c + 1) * d], NT_DIMS,
                                preferred_element_type=F32)
            _online_softmax_step(s * scale + bt, vv, m_scr.at[c], l_scr.at[c], acc_scr.at[c])
        return carry

    lax.fori_loop(0, i + 1, body, 0)

    lp = lam_ref[...]
    lam = (jnp.exp(jnp.sum(lp[0:1] * lp[1:2], axis=1, keepdims=True))
           - jnp.exp(jnp.sum(lp[2:3] * lp[3:4], axis=1, keepdims=True)) + lambda_init)
    o0 = acc_scr[0] * _lane_repeat(1.0 / l_scr[0], 2, axis=1)
    o1 = acc_scr[1] * _lane_repeat(1.0 / l_scr[1], 2, axis=1)
    o = o0 - lam * o1
    ms = jnp.mean(o * o, axis=1, keepdims=True)
    o_ref[...] = (o * lax.rsqrt(ms + LN_EPS) * g_ref[...] * (1.0 - lambda_init)).astype(o_ref.dtype)


def _diff_attention(qkv, bias_tiles, diff_lambda, subln, batch, seq, lambda_init):
    blk = bias_tiles.shape[2]
    nq = seq // blk
    h2d = DIFF_V_DIM
    kern = functools.partial(_diff_attn_kernel, blk=blk, lambda_init=lambda_init)
    return pl.pallas_call(
        kern,
        grid=(batch, DIFF_HEADS, nq),
        in_specs=[pl.BlockSpec((blk, h2d), lambda b, h, i: (b * nq + i, h)),
                  pl.BlockSpec((seq, h2d), lambda b, h, i: (b, DIFF_HEADS + h)),
                  pl.BlockSpec((seq, h2d), lambda b, h, i: (b, 2 * DIFF_HEADS + h)),
                  pl.BlockSpec((None, 3, blk, blk), lambda b, h, i: (h, 0, 0, 0)),
                  pl.BlockSpec((4, DIFF_HEAD_DIM), lambda b, h, i: (0, 0)),
                  pl.BlockSpec((1, h2d), lambda b, h, i: (0, 0))],
        out_specs=pl.BlockSpec((blk, h2d), lambda b, h, i: (b * nq + i, h)),
        out_shape=jax.ShapeDtypeStruct((batch * seq, DIFF_HEADS * h2d), BF16),
        scratch_shapes=[pltpu.VMEM((2, blk, 128), F32), pltpu.VMEM((2, blk, 128), F32),
                        pltpu.VMEM((2, blk, h2d), F32)],
        compiler_params=_params(3, 48),
    )(qkv, qkv, qkv, bias_tiles, diff_lambda, subln.reshape(1, h2d))


def _indexer_kernel(qi_ref, kfull_ref, wblk_ref, g_ref, b_ref, o_ref, ki_scr, key_scr, *, blk, seq, k_sel):
    i = pl.program_id(1)

    @pl.when(i == 0)
    def _():
        kr = kfull_ref[:, 0:IDX_HEAD_DIM]
        ki_scr[...] = _layer_norm(kr, g_ref[...], b_ref[...]).astype(BF16)

    w = wblk_ref[:, IDX_HEAD_DIM:IDX_HEAD_DIM + IDX_HEADS] * (IDX_HEADS ** -0.5 * IDX_HEAD_DIM ** -0.5)
    qi = qi_ref[...].astype(BF16)
    ki = ki_scr[...]
    score = jnp.zeros((blk, seq), F32)
    for h in range(IDX_HEADS):
        lg = lax.dot_general(qi[:, h * IDX_HEAD_DIM:(h + 1) * IDX_HEAD_DIM], ki, NT_DIMS,
                             preferred_element_type=F32)
        score = score + jnp.maximum(lg, 0.0) * w[:, h:h + 1]

    row = i * blk + lax.broadcasted_iota(jnp.int32, (blk, seq), 0)
    col = lax.broadcasted_iota(jnp.int32, (blk, seq), 1)
    valid = col <= row
    bits = lax.bitcast_convert_type(score, jnp.int32)
    key = bits ^ ((bits >> 31) & jnp.int32(0x7FFFFFFF))
    key_scr[...] = jnp.where(valid, key, INT_MIN)

    def search(it, thr):
        trial = thr + lax.shift_left(jnp.int32(1), 31 - it)
        cnt = jnp.sum(jnp.where(key_scr[...] >= trial, 1.0, 0.0), axis=1, keepdims=True)
        return jnp.where(cnt >= k_sel, trial, thr)

    thr = lax.fori_loop(0, 32, search, jnp.full((blk, 1), INT_MIN, jnp.int32))
    mask = jnp.where(valid & (key_scr[...] >= thr), 0.0, NEG).astype(o_ref.dtype)
    for jj in range(seq // blk):
        o_ref[jj] = mask[:, jj * blk:(jj + 1) * blk]


def _indexer_mask(qi, kiw, idx_k_g, idx_k_b, batch, seq, k_sel, qi_col_block):
    blk = min(ATT_BLOCK, seq)
    nq = seq // blk
    qw = IDX_HEADS * IDX_HEAD_DIM
    kern = functools.partial(_indexer_kernel, blk=blk, seq=seq, k_sel=k_sel)
    return pl.pallas_call(
        kern,
        grid=(batch, nq),
        in_specs=[pl.BlockSpec((blk, qw), lambda b, i: (b * nq + i, qi_col_block)),
                  pl.BlockSpec((seq, 128), lambda b, i: (b, 0)),
                  pl.BlockSpec((blk, 128), lambda b, i: (b * nq + i, 0)),
                  pl.BlockSpec((1, IDX_HEAD_DIM), lambda b, i: (0, 0)),
                  pl.BlockSpec((1, IDX_HEAD_DIM), lambda b, i: (0, 0))],
        out_specs=pl.BlockSpec((None, nq, blk, blk), lambda b, i: (b, 0, i, 0)),
        out_shape=jax.ShapeDtypeStruct((batch, nq, seq, blk), BF16),
        scratch_shapes=[pltpu.VMEM((seq, IDX_HEAD_DIM), BF16), pltpu.VMEM((blk, seq), jnp.int32)],
        compiler_params=_params(2, 48),
    )(qi, kiw, kiw, idx_k_g.reshape(1, -1), idx_k_b.reshape(1, -1))


def _dsa_attn_kernel(qs_ref, ckv_ref, g_ref, sel_ref, wuk_ref, wuv_ref, bias_ref, o_ref,
                     c_scr, ql_scr, sb_scr, m_scr, l_scr, acc_scr, *, blk):
    i = pl.program_id(1)
    d = DSA_HEAD_DIM
    scale = d ** -0.5

    @pl.when(i == 0)
    def _():
        x = ckv_ref[...]
        c = x * lax.rsqrt(jnp.mean(x * x, axis=1, keepdims=True) + LN_EPS) * g_ref[...]
        c_scr[...] = c.astype(BF16)

    qs = qs_ref[...]
    for h in range(DSA_HEADS):
        ql = jnp.dot(qs[:, h * d:(h + 1) * d], wuk_ref[h], preferred_element_type=F32)
        ql_scr[h * blk:(h + 1) * blk, :] = (ql * scale).astype(BF16)

    m_scr[...] = jnp.full(m_scr.shape, NEG, F32)
    l_scr[...] = jnp.zeros(l_scr.shape, F32)
    acc_scr[...] = jnp.zeros(acc_scr.shape, F32)

    def body(j, carry):
        start = pl.multiple_of(j * blk, blk)
        cc = c_scr[pl.ds(start, blk), :]
        s = lax.dot_general(ql_scr[...], cc, NT_DIMS, preferred_element_type=F32)
        selm = sel_ref[j].astype(F32)
        back = jnp.minimum(i - j, 2)
        for h in range(DSA_HEADS):
            sb_scr[h * blk:(h + 1) * blk, :] = bias_ref[h, back] + selm
        _online_softmax_step(s + sb_scr[...], cc, m_scr, l_scr, acc_scr)
        return carry

    lax.fori_loop(0, i + 1, body, 0)

    o_lat = (acc_scr[...] * _lane_repeat(1.0 / l_scr[...], DSA_KV_RANK // 128, axis=1)).astype(BF16)
    for h in range(DSA_HEADS):
        o_ref[:, h * d:(h + 1) * d] = jnp.dot(o_lat[h * blk:(h + 1) * blk], wuv_ref[h],
                                              preferred_element_type=F32).astype(o_ref.dtype)


def _dsa_attention(qkv, qs_col_block, ckv, ckv_col_block, kv_norm_g, sel, w_uk, w_uv, bias_tiles, batch, seq):
    blk = min(ATT_BLOCK, seq)
    nq = seq // blk
    hd = DSA_HEADS * DSA_HEAD_DIM
    r = DSA_KV_RANK
    rows = DSA_HEADS * blk
    kern = functools.partial(_dsa_attn_kernel, blk=blk)
    return pl.pallas_call(
        kern,
        grid=(batch, nq),
        in_specs=[pl.BlockSpec((blk, hd), lambda b, i: (b * nq + i, qs_col_block)),
                  pl.BlockSpec((seq, r), lambda b, i: (b, ckv_col_block)),
                  pl.BlockSpec((1, r), lambda b, i: (0, 0)),
                  pl.BlockSpec((None, nq, blk, blk), lambda b, i: (b, 0, i, 0)),
                  pl.BlockSpec((DSA_HEADS, DSA_HEAD_DIM, r), lambda b, i: (0, 0, 0)),
                  pl.BlockSpec((DSA_HEADS, r, DSA_HEAD_DIM), lambda b, i: (0, 0, 0)),
                  pl.BlockSpec((DSA_HEADS, 3, blk, blk), lambda b, i: (0, 0, 0, 0))],
        out_specs=pl.BlockSpec((blk, hd), lambda b, i: (b * nq + i, 0)),
        out_shape=jax.ShapeDtypeStruct((batch * seq, hd), BF16),
        scratch_shapes=[pltpu.VMEM((seq, r), BF16), pltpu.VMEM((rows, r), BF16), pltpu.VMEM((rows, blk), F32),
                        pltpu.VMEM((rows, 128), F32), pltpu.VMEM((rows, 128), F32), pltpu.VMEM((rows, r), F32)],
        compiler_params=_params(2, 56),
    )(qkv, ckv, kv_norm_g.reshape(1, r), sel, w_uk, w_uv, bias_tiles)


def _merge_kernel(x_ref, od_ref, os_ref, wgd_ref, wgs_ref, wpd_ref, wps_ref, o_ref):
    x = x_ref[...]
    gd = jnp.dot(x, wgd_ref[...], preferred_element_type=F32)
    gs = jnp.dot(x, wgs_ref[...], preferred_element_type=F32)
    pd = jnp.dot(od_ref[...], wpd_ref[...], preferred_element_type=F32)
    ps = jnp.dot(os_ref[...], wps_ref[...], preferred_element_type=F32)
    o_ref[...] = (_sigmoid(gd) * pd + _sigmoid(gs) * ps).astype(o_ref.dtype)


def _gated_merge(xb, o_d, o_s, wgd, wgs, wpd, wps):
    m, dm = xb.shape
    tm = min(512, m)
    tn = 512
    kd, ks = o_d.shape[1], o_s.shape[1]
    return pl.pallas_call(
        _merge_kernel,
        grid=(dm // tn, m // tm),
        in_specs=[pl.BlockSpec((tm, dm), lambda j, i: (i, 0)),
                  pl.BlockSpec((tm, kd), lambda j, i: (i, 0)),
                  pl.BlockSpec((tm, ks), lambda j, i: (i, 0)),
                  pl.BlockSpec((dm, tn), lambda j, i: (0, j)),
                  pl.BlockSpec((dm, tn), lambda j, i: (0, j)),
                  pl.BlockSpec((kd, tn), lambda j, i: (0, j)),
                  pl.BlockSpec((ks, tn), lambda j, i: (0, j))],
        out_specs=pl.BlockSpec((tm, tn), lambda j, i: (i, j)),
        out_shape=jax.ShapeDtypeStruct((m, dm), BF16),
        compiler_params=_params(2, 40),
    )(xb, o_d, o_s, wgd, wgs, wpd, wps)


def _mm_res_ln_kernel(a_ref, w_ref, res_ref, g_ref, b_ref, o_ref, ob_ref):
    y = jnp.dot(a_ref[...], w_ref[...], preferred_element_type=F32)
    zn = _layer_norm(DN_ALPHA * res_ref[...] + y, g_ref[...], b_ref[...])
    o_ref[...] = zn
    ob_ref[...] = zn.astype(BF16)


def _matmul_res_ln(a, w, res, g, b):
    m, k = a.shape
    n = w.shape[1]
    tm = min(256, m)
    return pl.pallas_call(
        _mm_res_ln_kernel,
        grid=(m // tm,),
        in_specs=[pl.BlockSpec((tm, k), lambda i: (i, 0)),
                  pl.BlockSpec((k, n), lambda i: (0, 0)),
                  pl.BlockSpec((tm, n), lambda i: (i, 0)),
                  pl.BlockSpec((1, n), lambda i: (0, 0)),
                  pl.BlockSpec((1, n), lambda i: (0, 0))],
        out_specs=[pl.BlockSpec((tm, n), lambda i: (i, 0)), pl.BlockSpec((tm, n), lambda i: (i, 0))],
        out_shape=[jax.ShapeDtypeStruct((m, n), F32), jax.ShapeDtypeStruct((m, n), BF16)],
        compiler_params=_params(1, 40),
    )(a, w, res, g.reshape(1, n), b.reshape(1, n))


def _pack_bf16_pairs(zb):
    half = zb.shape[1] // 2
    bits = lax.bitcast_convert_type(zb.astype(F32), jnp.uint32)
    return (bits[:, :half] & jnp.uint32(0xFFFF0000)) | (bits[:, half:] >> 16)


def _unpack_bf16_pairs(w):
    hi = lax.bitcast_convert_type(w & jnp.uint32(0xFFFF0000), F32).astype(BF16)
    lo = lax.bitcast_convert_type(w << 16, F32).astype(BF16)
    return hi, lo


def _xattn_kernel(hb_ref, h_ref, wq_ref, kv_ref, wo_ref, g_ref, b_ref, o_ref, ob_ref, op_ref):
    d = XATTN_HEAD_DIM
    hd = XATTN_HEADS * d
    scale = d ** -0.5
    qb = jnp.dot(hb_ref[...], wq_ref[...], preferred_element_type=F32).astype(BF16)
    kv = kv_ref[...]
    outs = []
    for h in range(XATTN_HEADS):
        s = lax.dot_general(qb[:, h * d:(h + 1) * d], kv[:, h * d:(h + 1) * d], NT_DIMS,
                            preferred_element_type=F32) * scale
        p = jnp.exp(s - jnp.max(s, axis=1, keepdims=True))
        l = jnp.sum(p, axis=1, keepdims=True)
        oh = jnp.dot(p.astype(BF16), kv[:, hd + h * d:hd + (h + 1) * d], preferred_element_type=F32)
        outs.append((oh / l).astype(BF16))
    o = jnp.concatenate(outs, axis=1)
    y = jnp.dot(o, wo_ref[...], preferred_element_type=F32)
    zn = _layer_norm(DN_ALPHA * h_ref[...] + y, g_ref[...], b_ref[...])
    zb = zn.astype(BF16)
    o_ref[...] = zn
    ob_ref[...] = zb
    op_ref[...] = _pack_bf16_pairs(zb)


def _cross_attention(hb, h, wq, kv, wo, g, b, batch, seq, n_mem):
    m, dm = h.shape
    tm = min(256, seq)
    nq = seq // tm
    hd = XATTN_HEADS * XATTN_HEAD_DIM
    return pl.pallas_call(
        _xattn_kernel,
        grid=(batch, nq),
        in_specs=[pl.BlockSpec((tm, dm), lambda bb, i: (bb * nq + i, 0)),
                  pl.BlockSpec((tm, dm), lambda bb, i: (bb * nq + i, 0)),
                  pl.BlockSpec((dm, hd), lambda bb, i: (0, 0)),
                  pl.BlockSpec((n_mem, 2 * hd), lambda bb, i: (bb, 0)),
                  pl.BlockSpec((hd, dm), lambda bb, i: (0, 0)),
                  pl.BlockSpec((1, dm), lambda bb, i: (0, 0)),
                  pl.BlockSpec((1, dm), lambda bb, i: (0, 0))],
        out_specs=[pl.BlockSpec((tm, dm), lambda bb, i: (bb * nq + i, 0)),
                   pl.BlockSpec((tm, dm), lambda bb, i: (bb * nq + i, 0)),
                   pl.BlockSpec((tm, dm // 2), lambda bb, i: (bb * nq + i, 0))],
        out_shape=[jax.ShapeDtypeStruct((m, dm), F32), jax.ShapeDtypeStruct((m, dm), BF16),
                   jax.ShapeDtypeStruct((m, dm // 2), jnp.uint32)],
        compiler_params=_params(2, 40),
    )(hb, h, wq, kv, wo, g.reshape(1, dm), b.reshape(1, dm))


def _split_bf16(x):
    hi = x.astype(BF16)
    return hi, (x - hi.astype(F32)).astype(BF16)


def _router_kernel(h_ref, rwt_ref, rb_ref, idx_ref, gate_ref, rank_ref, cnt_ref, cnt_scr):
    tm = h_ref.shape[0]
    per = N_EXPERTS // N_GROUPS
    h_hi, h_lo = _split_bf16(h_ref[...])
    w_hi, w_lo = _split_bf16(rwt_ref[...])
    logits = (lax.dot_general(w_hi, h_hi, NT_DIMS, preferred_element_type=F32)
              + lax.dot_general(w_hi, h_lo, NT_DIMS, preferred_element_type=F32)
              + lax.dot_general(w_lo, h_hi, NT_DIMS, preferred_element_type=F32))
    scores = _sigmoid(logits)
    sel = scores + rb_ref[...]

    sel3 = sel.reshape(N_GROUPS, per, tm)
    r_iota = lax.broadcasted_iota(jnp.int32, sel3.shape, 1).astype(F32)
    m1 = jnp.max(sel3, axis=1, keepdims=True)
    first = jnp.min(jnp.where(sel3 == m1, r_iota, float(per)), axis=1, keepdims=True)
    m2 = jnp.max(jnp.where(r_iota == first, -jnp.inf, sel3), axis=1, keepdims=True)
    gscore = (m1 + m2).reshape(N_GROUPS, tm)

    g_iota = lax.broadcasted_iota(jnp.int32, gscore.shape, 0).astype(F32)
    keep = jnp.zeros(gscore.shape, F32)
    cur = gscore
    for _ in range(TOPK_GROUPS):
        m = jnp.max(cur, axis=0, keepdims=True)
        first = jnp.min(jnp.where(cur == m, g_iota, float(N_GROUPS)), axis=0, keepdims=True)
        pick = g_iota == first
        keep = jnp.where(pick, 1.0, keep)
        cur = jnp.where(pick, -jnp.inf, cur)
    keep3 = jnp.broadcast_to(keep.reshape(N_GROUPS, 1, tm), sel3.shape)
    cur = jnp.where(keep3 > 0.5, sel3, -jnp.inf).reshape(N_EXPERTS, tm)

    e_iota = lax.broadcasted_iota(jnp.int32, cur.shape, 0).astype(F32)
    gsum = jnp.zeros((1, tm), F32)
    gates, picks = [], []
    for k in range(TOP_K):
        m = jnp.max(cur, axis=0, keepdims=True)
        first = jnp.min(jnp.where(cur == m, e_iota, float(N_EXPERTS)), axis=0, keepdims=True)
        pick = e_iota == first
        gk = jnp.sum(jnp.where(pick, scores, 0.0), axis=0, keepdims=True)
        idx_ref[k:k + 1, :] = first.astype(jnp.int32)
        gates.append(gk)
        picks.append(pick)
        gsum = gsum + gk
        cur = jnp.where(pick, -jnp.inf, cur)
    for k in range(TOP_K):
        gate_ref[k:k + 1, :] = gates[k] / gsum * ROUTED_SCALE

    @pl.when(pl.program_id(0) == 0)
    def _():
        cnt_scr[...] = jnp.zeros(cnt_scr.shape, F32)

    mask = jnp.zeros(cur.shape, F32)
    for k in range(TOP_K):
        mask = jnp.where(picks[k], 1.0, mask)
    upper = (lax.broadcasted_iota(jnp.int32, (tm, tm), 0) <= lax.broadcasted_iota(jnp.int32, (tm, tm), 1))
    cum = jnp.dot(mask.astype(BF16), jnp.where(upper, 1.0, 0.0).astype(BF16), preferred_element_type=F32)
    before = cnt_scr[...][:, 0:1] + cum - mask
    for k in range(TOP_K):
        rank_ref[k:k + 1, :] = jnp.sum(jnp.where(picks[k], before, 0.0), axis=0, keepdims=True).astype(jnp.int32)
    cnt_scr[...] = cnt_scr[...] + jnp.sum(mask, axis=1, keepdims=True)
    cnt_ref[...] = cnt_scr[...].astype(jnp.int32)


def _router(h, router_w, router_bias):
    m, dm = h.shape
    tm = min(512, m)
    return pl.pallas_call(
        _router_kernel,
        grid=(m // tm,),
        in_specs=[pl.BlockSpec((tm, dm), lambda i: (i, 0)),
                  pl.BlockSpec((N_EXPERTS, dm), lambda i: (0, 0)),
                  pl.BlockSpec((N_EXPERTS, 1), lambda i: (0, 0))],
        out_specs=[pl.BlockSpec((TOP_K, tm), lambda i: (0, i)), pl.BlockSpec((TOP_K, tm), lambda i: (0, i)),
                   pl.BlockSpec((TOP_K, tm), lambda i: (0, i)), pl.BlockSpec((N_EXPERTS, 128), lambda i: (0, 0))],
        out_shape=[jax.ShapeDtypeStruct((TOP_K, m), jnp.int32), jax.ShapeDtypeStruct((TOP_K, m), F32),
                   jax.ShapeDtypeStruct((TOP_K, m), jnp.int32), jax.ShapeDtypeStruct((N_EXPERTS, 128), jnp.int32)],
        scratch_shapes=[pltpu.VMEM((N_EXPERTS, 128), F32)],
        compiler_params=_params(1, 32),
    )(h, router_w.T, router_bias.reshape(N_EXPERTS, 1))


def _dest_kernel(idx_ref, rank_ref, ps_ref, dest_ref):
    tm = idx_ref.shape[1]
    e_iota = lax.broadcasted_iota(jnp.int32, (N_EXPERTS, tm), 0)
    ps = ps_ref[...][:, 0:1]
    for k in range(TOP_K):
        base = jnp.sum(jnp.where(e_iota == idx_ref[k:k + 1, :], ps, 0), axis=0, keepdims=True)
        dest_ref[k:k + 1, :] = rank_ref[k:k + 1, :] + base


def _dest_rows(idx_t, rank_t, pad_start):
    m = idx_t.shape[1]
    tm = min(2048, m)
    ps = jnp.broadcast_to(pad_start.astype(F32).reshape(N_EXPERTS, 1), (N_EXPERTS, 128))
    out = pl.pallas_call(
        _dest_kernel,
        grid=(m // tm,),
        in_specs=[pl.BlockSpec((TOP_K, tm), lambda i: (0, i)), pl.BlockSpec((TOP_K, tm), lambda i: (0, i)),
                  pl.BlockSpec((N_EXPERTS, 128), lambda i: (0, 0))],
        out_specs=pl.BlockSpec((TOP_K, tm), lambda i: (0, i)),
        out_shape=jax.ShapeDtypeStruct((TOP_K, m), F32),
        compiler_params=_params(1, 32),
    )(idx_t, rank_t.astype(F32), ps)
    return out.astype(jnp.int32)


def _dispatch_kernel(dest_ref, x_ref, init_ref, xs_ref, sem):
    del init_ref
    tm = x_ref.shape[0]

    def copy(t, k):
        return pltpu.make_async_copy(x_ref.at[pl.ds(t, 1), :], xs_ref.at[pl.ds(dest_ref[0, k, t], 1), :], sem)

    def start(t, carry):
        for k in range(TOP_K):
            copy(t, k).start()
        return carry

    lax.fori_loop(0, tm, start, 0)
    n_rows = TOP_K * tm
    pltpu.make_async_copy(xs_ref.at[pl.ds(0, n_rows), :], xs_ref.at[pl.ds(n_rows, n_rows), :], sem).wait()


def _dispatch(xp, dest, rows):
    m, wd = xp.shape
    tm = min(512, m)
    dest3 = dest.reshape(TOP_K, m // tm, tm).transpose(1, 0, 2)
    return pl.pallas_call(
        _dispatch_kernel,
        grid=(m // tm,),
        in_specs=[pl.BlockSpec((1, TOP_K, tm), lambda i: (i, 0, 0), memory_space=pltpu.SMEM),
                  pl.BlockSpec((tm, wd), lambda i: (i, 0)),
                  pl.BlockSpec(memory_space=pl.ANY)],
        out_specs=pl.BlockSpec(memory_space=pl.ANY),
        out_shape=jax.ShapeDtypeStruct((rows, wd), xp.dtype),
        scratch_shapes=[pltpu.SemaphoreType.DMA(())],
        input_output_aliases={2: 0},
        compiler_params=_params(1, 32),
    )(dest3, xp, jnp.zeros((rows, wd), xp.dtype))


def _expert_kernel(be_ref, nu_ref, x_ref, w1_ref, w3_ref, w2_ref, o_ref, w1b, w3b, w2b):
    i = pl.program_id(0)
    changed = jnp.logical_or(i == 0, be_ref[i] != be_ref[jnp.maximum(i - 1, 0)])

    @pl.when(changed)
    def _():
        w1b[...] = w1_ref[...].astype(BF16)
        w3b[...] = w3_ref[...].astype(BF16)
        w2b[...] = w2_ref[...].astype(BF16)

    @pl.when(i < nu_ref[0])
    def _():
        xh, xl = _unpack_bf16_pairs(x_ref[...])
        half = xh.shape[1]
        a = (jnp.dot(xh, w1b[0:half, :], preferred_element_type=F32)
             + jnp.dot(xl, w1b[half:, :], preferred_element_type=F32))
        b = (jnp.dot(xh, w3b[0:half, :], preferred_element_type=F32)
             + jnp.dot(xl, w3b[half:, :], preferred_element_type=F32))
        hmid = (a * _sigmoid(a) * b).astype(BF16)
        o_ref[...] = _pack_bf16_pairs(jnp.dot(hmid, w2b[...], preferred_element_type=F32).astype(BF16))

    @pl.when(i >= nu_ref[0])
    def _():
        o_ref[...] = jnp.zeros(o_ref.shape, o_ref.dtype)


def _expert_blocks(x_sorted, block_e, n_used, w1, w3, w2):
    rows, wd = x_sorted.shape
    n_blocks = rows // MOE_ROWS
    dm, de = w1.shape[1], w1.shape[2]
    grid_spec = pltpu.PrefetchScalarGridSpec(
        num_scalar_prefetch=2,
        grid=(n_blocks,),
        in_specs=[pl.BlockSpec((MOE_ROWS, wd), lambda i, be, nu: (i, 0)),
                  pl.BlockSpec((None, dm, de), lambda i, be, nu: (be[i], 0, 0)),
                  pl.BlockSpec((None, dm, de), lambda i, be, nu: (be[i], 0, 0)),
                  pl.BlockSpec((None, de, dm), lambda i, be, nu: (be[i], 0, 0))],
        out_specs=pl.BlockSpec((MOE_ROWS, wd), lambda i, be, nu: (i, 0)),
        scratch_shapes=[pltpu.VMEM((dm, de), BF16), pltpu.VMEM((dm, de), BF16), pltpu.VMEM((de, dm), BF16)],
    )
    return pl.pallas_call(
        _expert_kernel,
        grid_spec=grid_spec,
        out_shape=jax.ShapeDtypeStruct((rows, wd), jnp.uint32),
        compiler_params=_params(1, 56),
    )(block_e, n_used, x_sorted, w1, w3, w2)


def _shared_kernel(x_ref, w1_ref, w3_ref, w2_ref, o_ref):
    x = x_ref[...]
    a = jnp.dot(x, w1_ref[...], preferred_element_type=F32)
    b = jnp.dot(x, w3_ref[...], preferred_element_type=F32)
    hmid = (a * _sigmoid(a) * b).astype(BF16)
    o_ref[...] = jnp.dot(hmid, w2_ref[...], preferred_element_type=F32)


def _shared_expert(xb, w1, w3, w2):
    m, dm = xb.shape
    de = w1.shape[1]
    tm = min(512, m)
    return pl.pallas_call(
        _shared_kernel,
        grid=(m // tm,),
        in_specs=[pl.BlockSpec((tm, dm), lambda i: (i, 0)),
                  pl.BlockSpec((dm, de), lambda i: (0, 0)),
                  pl.BlockSpec((dm, de), lambda i: (0, 0)),
                  pl.BlockSpec((de, dm), lambda i: (0, 0))],
        out_specs=pl.BlockSpec((tm, dm), lambda i: (i, 0)),
        out_shape=jax.ShapeDtypeStruct((m, dm), F32),
        compiler_params=_params(1, 32),
    )(xb, w1, w3, w2)


def _combine_kernel(dcur_ref, dnext_ref, h_ref, sh_ref, gate_ref, g_ref, b_ref, y_ref, o_ref, ybuf, sems):
    i = pl.program_id(0)
    n = pl.num_programs(0)
    tm = h_ref.shape[0]
    slot = lax.rem(i, 2)

    def copy(d_ref, s, t, k):
        return pltpu.make_async_copy(y_ref.at[pl.ds(d_ref[0, k, t], 1), :],
                                     ybuf.at[s, pl.ds(k * tm + t, 1), :], sems.at[s])

    def start_tile(d_ref, s):
        def body(t, carry):
            for k in range(TOP_K):
                copy(d_ref, s, t, k).start()
            return carry
        lax.fori_loop(0, tm, body, 0)

    @pl.when(i == 0)
    def _():
        start_tile(dcur_ref, 0)

    @pl.when(i + 1 < n)
    def _():
        start_tile(dnext_ref, 1 - slot)

    pltpu.make_async_copy(y_ref.at[pl.ds(0, TOP_K * tm), :], ybuf.at[slot], sems.at[slot]).wait()

    gate = gate_ref[...]
    half = ybuf.shape[2]
    yh = jnp.zeros((tm, half), F32)
    yl = jnp.zeros((tm, half), F32)
    for k in range(TOP_K):
        w = ybuf[slot, k * tm:(k + 1) * tm, :]
        gk = gate[:, k:k + 1]
        yh = yh + lax.bitcast_convert_type(w & jnp.uint32(0xFFFF0000), F32) * gk
        yl = yl + lax.bitcast_convert_type(w << 16, F32) * gk
    y = sh_ref[...] + jnp.concatenate([yh, yl], axis=1)
    o_ref[...] = _layer_norm(DN_ALPHA * h_ref[...] + y, g_ref[...], b_ref[...])


def _combine_ln(h, shared, y_sorted, dest, gate, g, b):
    m, dm = h.shape
    tm = min(128, m)
    n = m // tm
    dest3 = dest.reshape(TOP_K, n, tm).transpose(1, 0, 2)
    return pl.pallas_call(
        _combine_kernel,
        grid=(n,),
        in_specs=[pl.BlockSpec((1, TOP_K, tm), lambda i: (i, 0, 0), memory_space=pltpu.SMEM),
                  pl.BlockSpec((1, TOP_K, tm), lambda i: (jnp.minimum(i + 1, n - 1), 0, 0), memory_space=pltpu.SMEM),
                  pl.BlockSpec((tm, dm), lambda i: (i, 0)),
                  pl.BlockSpec((tm, dm), lambda i: (i, 0)),
                  pl.BlockSpec((tm, TOP_K), lambda i: (i, 0)),
                  pl.BlockSpec((1, dm), lambda i: (0, 0)),
                  pl.BlockSpec((1, dm), lambda i: (0, 0)),
                  pl.BlockSpec(memory_space=pl.ANY)],
        out_specs=pl.BlockSpec((tm, dm), lambda i: (i, 0)),
        out_shape=jax.ShapeDtypeStruct((m, dm), F32),
        scratch_shapes=[pltpu.VMEM((2, TOP_K * tm, dm // 2), jnp.uint32), pltpu.SemaphoreType.DMA((2,))],
        compiler_params=_params(1, 40),
    )(dest3, dest3, h, shared, gate, g.reshape(1, dm), b.reshape(1, dm), y_sorted)


def _moe_sublayer(h, hb, hp, router_w, router_bias, w1, w3, w2, sw1, sw3, sw2, g, b):
    n_tok, dm = h.shape
    idx_t, gate_t, rank_t, counts = _router(h, router_w, router_bias)
    counts = counts[:, 0]
    padded = (counts + MOE_ROWS - 1) // MOE_ROWS * MOE_ROWS
    pad_end = jnp.cumsum(padded)
    n_blocks = (n_tok * TOP_K + N_EXPERTS * (MOE_ROWS - 1) + MOE_ROWS - 1) // MOE_ROWS
    block_start = jnp.arange(n_blocks, dtype=jnp.int32) * MOE_ROWS
    block_e = jnp.minimum(jnp.sum((pad_end[None, :] <= block_start[:, None]).astype(jnp.int32), axis=1),
                          N_EXPERTS - 1)
    n_used = (pad_end[-1] // MOE_ROWS).astype(jnp.int32).reshape(1)
    dest = _dest_rows(idx_t, rank_t, pad_end - padded)
    x_sorted = _dispatch(hp, dest, n_blocks * MOE_ROWS)
    y_sorted = _expert_blocks(x_sorted, block_e, n_used, w1, w3, w2)
    shared = _shared_expert(hb, sw1.astype(BF16), sw3.astype(BF16), sw2.astype(BF16))
    return _combine_ln(h, shared, y_sorted, dest, gate_t.T, g, b)


def kernel(x, mem, positions, rel_bias_table, w_in, diff_lambda, diff_subln, idx_k_g, idx_k_b, kv_norm_g, w_uk, w_uv, w_proj_diff, w_proj_dsa, w_mix_out, ln1_g, ln1_b, xattn_wq, xattn_wkv, xattn_wo, ln2_g, ln2_b, router_w, router_bias, exp_w1, exp_w3, exp_w2, sh_w1, sh_w3, sh_w2, ln3_g, ln3_b):
    del positions
    batch, seq, dm = x.shape
    n_mem = mem.shape[1]
    n_tok = batch * seq
    lambda_init = 0.8 - 0.6 * math.exp(-0.3 * 0)
    blk = min(ATT_BLOCK, seq)

    dq = DIFF_HEADS * DIFF_V_DIM
    c_qs = 3 * dq
    c_ckv = c_qs + DSA_HEADS * DSA_HEAD_DIM
    c_qi = c_ckv + DSA_KV_RANK
    c_ki = c_qi + IDX_HEADS * IDX_HEAD_DIM
    c_gd = c_ki + IDX_HEAD_DIM + IDX_HEADS
    c_gs = c_gd + dm

    xf = x.reshape(n_tok, dm)
    xb = xf.astype(BF16)
    w_in0 = w_in[0]
    qkv = _matmul(xb, w_in0, 0, c_ckv, BF16, 1024, 1024)
    ckv = _matmul(xb, w_in0, c_ckv, DSA_KV_RANK, F32, 1024, 512)
    qi = _matmul(xb, w_in0, c_qi, c_ki - c_qi, BF16, 1024, 512)
    w_tail = jnp.pad(w_in0[:, c_ki:c_gd], ((0, 0), (0, 128 - (c_gd - c_ki))))
    kiw = _matmul(xb, w_tail, 0, 128, F32, 1024, 128)

    dsa_tiles = _rel_bias_tiles(rel_bias_table[:, DIFF_HEADS:], blk)
    diff_tiles = _rel_bias_tiles(rel_bias_table[:, :DIFF_HEADS], min(DIFF_BLOCK, seq))
    o_d = _diff_attention(qkv, diff_tiles, diff_lambda[0], diff_subln[0], batch, seq, lambda_init)

    k_sel = min(DSA_TOPK_MAX, seq // 4)
    sel = _indexer_mask(qi, kiw, idx_k_g[0], idx_k_b[0], batch, seq, k_sel, qi_col_block=0)
    o_s = _dsa_attention(qkv, c_qs // (DSA_HEADS * DSA_HEAD_DIM), ckv, 0, kv_norm_g[0], sel,
                         w_uk[0].astype(BF16), w_uv[0].astype(BF16), dsa_tiles, batch, seq)

    merged = _gated_merge(xb, o_d, o_s, w_in0[:, c_gd:c_gs].astype(BF16), w_in0[:, c_gs:c_gs + dm].astype(BF16),
                          w_proj_diff[0].astype(BF16), w_proj_dsa[0].astype(BF16))
    h1, h1b = _matmul_res_ln(merged, w_mix_out[0].astype(BF16), xf, ln1_g[0], ln1_b[0])

    memb = mem.reshape(batch * n_mem, dm).astype(BF16)
    kv = _matmul(memb, xattn_wkv[0], 0, 2 * XATTN_HEADS * XATTN_HEAD_DIM, BF16, 1024, 512)
    h2, h2b, h2p = _cross_attention(h1b, h1, xattn_wq[0].astype(BF16), kv, xattn_wo[0].astype(BF16),
                                    ln2_g[0], ln2_b[0], batch, seq, n_mem)

    out = _moe_sublayer(h2, h2b, h2p, router_w[0], router_bias[0], exp_w1[0], exp_w3[0], exp_w2[0],
                        sh_w1[0], sh_w3[0], sh_w2[0], ln3_g[0], ln3_b[0])
    return out.reshape(batch, seq, dm)
```

```python
import functools
import math

import jax
import jax.numpy as jnp
from jax import lax
from jax.experimental import pallas as pl
from jax.experimental.pallas import tpu as pltpu

F32 = jnp.float32
BF16 = jnp.bfloat16
NEG = -1e30
INT_MIN = -(2 ** 31)
MIB = 1024 * 1024

LN_EPS = 1e-5
DIFF_HEADS = 8
DIFF_HEAD_DIM = 128
DIFF_V_DIM = 2 * DIFF_HEAD_DIM
DSA_HEADS = 8
DSA_HEAD_DIM = 128
DSA_KV_RANK = 512
IDX_HEADS = 16
IDX_HEAD_DIM = 64
DSA_TOPK_MAX = 256
REL_BUCKETS = 32
REL_MAX_DIST = 128
XATTN_HEADS = 4
XATTN_HEAD_DIM = 128
N_EXPERTS = 64
EXPERT_DIM = 512
TOP_K = 8
N_GROUPS = 8
TOPK_GROUPS = 4
ROUTED_SCALE = 2.5
DEPTH = 1
DN_ALPHA = (2 * DEPTH) ** 0.25

ATT_BLOCK = 256
DIFF_BLOCK = 512
MOE_ROWS = 256
NT_DIMS = (((1,), (1,)), ((), ()))


def _params(n_grid, vmem_mib):
    return pltpu.CompilerParams(dimension_semantics=("arbitrary",) * n_grid,
                                vmem_limit_bytes=vmem_mib * MIB)


def _sigmoid(x):
    return 1.0 / (1.0 + jnp.exp(-x))


def _layer_norm(z, g, b):
    mu = jnp.mean(z, axis=1, keepdims=True)
    zc = z - mu
    var = jnp.mean(zc * zc, axis=1, keepdims=True)
    return zc * lax.rsqrt(var + LN_EPS) * g + b


def _mm_kernel(a_ref, b_ref, o_ref, bq_ref):
    @pl.when(pl.program_id(1) == 0)
    def _():
        bq_ref[...] = b_ref[...].astype(BF16)

    o_ref[...] = jnp.dot(a_ref[...], bq_ref[...], preferred_element_type=F32).astype(o_ref.dtype)


def _matmul(a, b, col_start, n_cols, out_dtype, tm, tn):
    m, k = a.shape
    tm = min(tm, m)
    assert col_start % tn == 0 and n_cols % tn == 0 and m % tm == 0
    off = col_start // tn
    return pl.pallas_call(
        _mm_kernel,
        grid=(n_cols // tn, m // tm),
        in_specs=[pl.BlockSpec((tm, k), lambda j, i: (i, 0)),
                  pl.BlockSpec((k, tn), lambda j, i: (0, j + off))],
        out_specs=pl.BlockSpec((tm, tn), lambda j, i: (i, j)),
        out_shape=jax.ShapeDtypeStruct((m, n_cols), out_dtype),
        scratch_shapes=[pltpu.VMEM((k, tn), BF16)],
        compiler_params=_params(2, 48),
    )(a, b)


def _rel_bucket(n):
    n = jnp.maximum(n, 0)
    max_exact = REL_BUCKETS // 2
    nf = jnp.maximum(n, 1).astype(F32)
    large = max_exact + (jnp.log(nf / max_exact) / math.log(REL_MAX_DIST / max_exact)
                         * (REL_BUCKETS - max_exact)).astype(jnp.int32)
    large = jnp.minimum(large, REL_BUCKETS - 1)
    return jnp.where(n < max_exact, n, large)


def _rel_bias_tiles(table, blk):
    assert blk >= REL_MAX_DIST
    r = jnp.arange(blk, dtype=jnp.int32)[:, None]
    c = jnp.arange(blk, dtype=jnp.int32)[None, :]
    tiles = []
    for back in range(3):
        n = back * blk + r - c
        bucket = _rel_bucket(n)
        t = jnp.zeros((table.shape[1], blk, blk), F32)
        for bkt in range(REL_BUCKETS):
            t = jnp.where(bucket == bkt, table[bkt][:, None, None], t)
        tiles.append(jnp.where(n >= 0, t, NEG))
    return jnp.stack(tiles, axis=1).astype(F32)


def _lane_repeat(x, n, axis=1):
    assert axis == 1
    return x if n == 1 else jnp.concatenate([x] * n, axis=1)


def _online_softmax_step(s, v, m_ref, l_ref, acc_ref):
    tk = s.shape[1]
    e = v.shape[1]
    m_prev = m_ref[...]
    m_next = jnp.maximum(m_prev, jnp.max(s, axis=1, keepdims=True))
    p = jnp.exp(s - _lane_repeat(m_next, tk // 128, axis=1))
    alpha = jnp.exp(m_prev - m_next)
    l_ref[...] = alpha * l_ref[...] + jnp.sum(p, axis=1, keepdims=True)
    m_ref[...] = m_next
    acc_ref[...] = (acc_ref[...] * _lane_repeat(alpha, e // 128, axis=1)
                    + jnp.dot(p.astype(BF16), v, preferred_element_type=F32))


def _diff_attn_kernel(q_ref, k_ref, v_ref, bias_ref, lam_ref, g_ref, o_ref, m_scr, l_scr, acc_scr,
                      *, blk, lambda_init):
    i = pl.program_id(2)
    d = DIFF_HEAD_DIM
    scale = d ** -0.5
    m_scr[...] = jnp.full(m_scr.shape, NEG, F32)
    l_scr[...] = jnp.zeros(l_scr.shape, F32)
    acc_scr[...] = jnp.zeros(acc_scr.shape, F32)
    q = q_ref[...]

    def body(j, carry):
        start = pl.multiple_of(j * blk, blk)
        kk = k_ref[pl.ds(start, blk), :]
        vv = v_ref[pl.ds(start, blk), :]
        bt = bias_ref[jnp.minimum(i - j, 2)]
        for c in range(2):
            s = lax.dot_general(q[:, c * d:(c + 1) * d], kk[:, c * d:(c + 1) * d], NT_DIMS,
                                preferred_element_type=F32)
            _online_softmax_step(s * scale + bt, vv, m_scr.at[c], l_scr.at[c], acc_scr.at[c])
        return carry

    lax.fori_loop(0, i + 1, body, 0)

    lp = lam_ref[...]
    lam = (jnp.exp(jnp.sum(lp[0:1] * lp[1:2], axis=1, keepdims=True))
           - jnp.exp(jnp.sum(lp[2:3] * lp[3:4], axis=1, keepdims=True)) + lambda_init)
    o0 = acc_scr[0] * _lane_repeat(1.0 / l_scr[0], 2, axis=1)
    o1 = acc_scr[1] * _lane_repeat(1.0 / l_scr[1], 2, axis=1)
    o = o0 - lam * o1
    ms = jnp.mean(o * o, axis=1, keepdims=True)
    o_ref[...] = (o * lax.rsqrt(ms + LN_EPS) * g_ref[...] * (1.0 - lambda_init)).astype(o_ref.dtype)


def _diff_attention(qkv, bias_tiles, diff_lambda, subln, batch, seq, lambda_init):
    blk = bias_tiles.shape[2]
    nq = seq // blk
    h2d = DIFF_V_DIM
    kern = functools.partial(_diff_attn_kernel, blk=blk, lambda_init=lambda_init)
    return pl.pallas_call(
        kern,
        grid=(batch, DIFF_HEADS, nq),
        in_specs=[pl.BlockSpec((blk, h2d), lambda b, h, i: (b * nq + i, h)),
                  pl.BlockSpec((seq, h2d), lambda b, h, i: (b, DIFF_HEADS + h)),
                  pl.BlockSpec((seq, h2d), lambda b, h, i: (b, 2 * DIFF_HEADS + h)),
                  pl.BlockSpec((None, 3, blk, blk), lambda b, h, i: (h, 0, 0, 0)),
                  pl.BlockSpec((4, DIFF_HEAD_DIM), lambda b, h, i: (0, 0)),
                  pl.BlockSpec((1, h2d), lambda b, h, i: (0, 0))],
        out_specs=pl.BlockSpec((blk, h2d), lambda b, h, i: (b * nq + i, h)),
        out_shape=jax.ShapeDtypeStruct((batch * seq, DIFF_HEADS * h2d), BF16),
        scratch_shapes=[pltpu.VMEM((2, blk, 128), F32), pltpu.VMEM((2, blk, 128), F32),
                        pltpu.VMEM((2, blk, h2d), F32)],
        compiler_params=_params(3, 48),
    )(qkv, qkv, qkv, bias_tiles, diff_lambda, subln.reshape(1, h2d))


def _indexer_kernel(qi_ref, kfull_ref, wblk_ref, g_ref, b_ref, o_ref, ki_scr, key_scr, wb_scr, *, blk, seq, k_sel):
    i = pl.program_id(1)

    @pl.when(i == 0)
    def _():
        kr = kfull_ref[:, 0:IDX_HEAD_DIM]
        ki_scr[...] = _layer_norm(kr, g_ref[...], b_ref[...]).astype(BF16)

    w = wblk_ref[:, IDX_HEAD_DIM:IDX_HEAD_DIM + IDX_HEADS] * (IDX_HEADS ** -0.5 * IDX_HEAD_DIM ** -0.5)
    for h in range(IDX_HEADS):
        wb_scr[h] = jnp.broadcast_to(w[:, h:h + 1], (blk, blk))
    qi = qi_ref[...].astype(BF16)
    row = i * blk + lax.broadcasted_iota(jnp.int32, (blk, blk), 0)
    col = lax.broadcasted_iota(jnp.int32, (blk, blk), 1)

    def score_tile(j, carry):
        start = pl.multiple_of(j * blk, blk)
        kc = ki_scr[pl.ds(start, blk), :]
        score = jnp.zeros((blk, blk), F32)
        for h in range(IDX_HEADS):
            lg = lax.dot_general(qi[:, h * IDX_HEAD_DIM:(h + 1) * IDX_HEAD_DIM], kc, NT_DIMS,
                                 preferred_element_type=F32)
            score = score + jnp.maximum(lg, 0.0) * wb_scr[h]
        bits = lax.bitcast_convert_type(score, jnp.int32)
        key = bits ^ ((bits >> 31) & jnp.int32(0x7FFFFFFF))
        key_scr[j] = jnp.where(start + col <= row, key, INT_MIN)
        return carry

    lax.fori_loop(0, i + 1, score_tile, 0)

    def search(it, thr):
        trial = thr + lax.shift_left(jnp.int32(1), 31 - it)

        def count_tile(j, c):
            ge = jnp.where(key_scr[j] >= trial, 1.0, 0.0)
            for part in range(blk // 128):
                c = c + ge[:, part * 128:(part + 1) * 128]
            return c

        c = lax.fori_loop(0, i + 1, count_tile, jnp.zeros((blk, 128), F32))
        return jnp.where(jnp.sum(c, axis=1, keepdims=True) >= k_sel, trial, thr)

    thr = lax.fori_loop(0, 32, search, jnp.full((blk, 1), INT_MIN, jnp.int32))
    for jj in range(seq // blk):
        @pl.when(jj <= i)
        def _():
            keep = (jj * blk + col <= row) & (key_scr[jj] >= thr)
            o_ref[jj] = jnp.where(keep, 0.0, NEG).astype(o_ref.dtype)

        @pl.when(jj > i)
        def _():
            o_ref[jj] = jnp.full((blk, blk), NEG, o_ref.dtype)


def _indexer_mask(qi, kiw, idx_k_g, idx_k_b, batch, seq, k_sel, qi_col_block):
    blk = min(ATT_BLOCK, seq)
    nq = seq // blk
    qw = IDX_HEADS * IDX_HEAD_DIM
    kern = functools.partial(_indexer_kernel, blk=blk, seq=seq, k_sel=k_sel)
    return pl.pallas_call(
        kern,
        grid=(batch, nq),
        in_specs=[pl.BlockSpec((blk, qw), lambda b, i: (b * nq + i, qi_col_block)),
                  pl.BlockSpec((seq, 128), lambda b, i: (b, 0)),
                  pl.BlockSpec((blk, 128), lambda b, i: (b * nq + i, 0)),
                  pl.BlockSpec((1, IDX_HEAD_DIM), lambda b, i: (0, 0)),
                  pl.BlockSpec((1, IDX_HEAD_DIM), lambda b, i: (0, 0))],
        out_specs=pl.BlockSpec((None, nq, blk, blk), lambda b, i: (b, 0, i, 0)),
        out_shape=jax.ShapeDtypeStruct((batch, nq, seq, blk), BF16),
        scratch_shapes=[pltpu.VMEM((seq, IDX_HEAD_DIM), BF16), pltpu.VMEM((nq, blk, blk), jnp.int32),
                        pltpu.VMEM((IDX_HEADS, blk, blk), F32)],
        compiler_params=_params(2, 48),
    )(qi, kiw, kiw, idx_k_g.reshape(1, -1), idx_k_b.reshape(1, -1))


def _dsa_attn_kernel(qs_ref, ckv_ref, g_ref, sel_ref, wuk_ref, wuv_ref, bias_ref, o_ref,
                     c_scr, ql_scr, sb_scr, m_scr, l_scr, acc_scr, *, blk):
    i = pl.program_id(1)
    d = DSA_HEAD_DIM
    scale = d ** -0.5

    @pl.when(i == 0)
    def _():
        x = ckv_ref[...]
        c = x * lax.rsqrt(jnp.mean(x * x, axis=1, keepdims=True) + LN_EPS) * g_ref[...]
        c_scr[...] = c.astype(BF16)

    qs = qs_ref[...]
    for h in range(DSA_HEADS):
        ql = jnp.dot(qs[:, h * d:(h + 1) * d], wuk_ref[h], preferred_element_type=F32)
        ql_scr[h * blk:(h + 1) * blk, :] = (ql * scale).astype(BF16)

    m_scr[...] = jnp.full(m_scr.shape, NEG, F32)
    l_scr[...] = jnp.zeros(l_scr.shape, F32)
    acc_scr[...] = jnp.zeros(acc_scr.shape, F32)

    def body(j, carry):
        start = pl.multiple_of(j * blk, blk)
        cc = c_scr[pl.ds(start, blk), :]
        s = lax.dot_general(ql_scr[...], cc, NT_DIMS, preferred_element_type=F32)
        selm = sel_ref[j].astype(F32)
        back = jnp.minimum(i - j, 2)
        for h in range(DSA_HEADS):
            sb_scr[h * blk:(h + 1) * blk, :] = bias_ref[h, back] + selm
        _online_softmax_step(s + sb_scr[...], cc, m_scr, l_scr, acc_scr)
        return carry

    lax.fori_loop(0, i + 1, body, 0)

    o_lat = (acc_scr[...] * _lane_repeat(1.0 / l_scr[...], DSA_KV_RANK // 128, axis=1)).astype(BF16)
    for h in range(DSA_HEADS):
        o_ref[:, h * d:(h + 1) * d] = jnp.dot(o_lat[h * blk:(h + 1) * blk], wuv_ref[h],
                                              preferred_element_type=F32).astype(o_ref.dtype)


def _dsa_attention(qkv, qs_col_block, ckv, ckv_col_block, kv_norm_g, sel, w_uk, w_uv, bias_tiles, batch, seq):
    blk = min(ATT_BLOCK, seq)
    nq = seq // blk
    hd = DSA_HEADS * DSA_HEAD_DIM
    r = DSA_KV_RANK
    rows = DSA_HEADS * blk
    kern = functools.partial(_dsa_attn_kernel, blk=blk)
    return pl.pallas_call(
        kern,
        grid=(batch, nq),
        in_specs=[pl.BlockSpec((blk, hd), lambda b, i: (b * nq + i, qs_col_block)),
                  pl.BlockSpec((seq, r), lambda b, i: (b, ckv_col_block)),
                  pl.BlockSpec((1, r), lambda b, i: (0, 0)),
                  pl.BlockSpec((None, nq, blk, blk), lambda b, i: (b, 0, i, 0)),
                  pl.BlockSpec((DSA_HEADS, DSA_HEAD_DIM, r), lambda b, i: (0, 0, 0)),
                  pl.BlockSpec((DSA_HEADS, r, DSA_HEAD_DIM), lambda b, i: (0, 0, 0)),
                  pl.BlockSpec((DSA_HEADS, 3, blk, blk), lambda b, i: (0, 0, 0, 0))],
        out_specs=pl.BlockSpec((blk, hd), lambda b, i: (b * nq + i, 0)),
        out_shape=jax.ShapeDtypeStruct((batch * seq, hd), BF16),
        scratch_shapes=[pltpu.VMEM((seq, r), BF16), pltpu.VMEM((rows, r), BF16), pltpu.VMEM((rows, blk), F32),
                        pltpu.VMEM((rows, 128), F32), pltpu.VMEM((rows, 128), F32), pltpu.VMEM((rows, r), F32)],
        compiler_params=_params(2, 56),
    )(qkv, ckv, kv_norm_g.reshape(1, r), sel, w_uk, w_uv, bias_tiles)


def _merge_kernel(x_ref, od_ref, os_ref, wgd_ref, wgs_ref, wpd_ref, wps_ref, o_ref):
    x = x_ref[...]
    gd = jnp.dot(x, wgd_ref[...], preferred_element_type=F32)
    gs = jnp.dot(x, wgs_ref[...], preferred_element_type=F32)
    pd = jnp.dot(od_ref[...], wpd_ref[...], preferred_element_type=F32)
    ps = jnp.dot(os_ref[...], wps_ref[...], preferred_element_type=F32)
    o_ref[...] = (_sigmoid(gd) * pd + _sigmoid(gs) * ps).astype(o_ref.dtype)


def _gated_merge(xb, o_d, o_s, wgd, wgs, wpd, wps):
    m, dm = xb.shape
    tm = min(512, m)
    tn = 512
    kd, ks = o_d.shape[1], o_s.shape[1]
    return pl.pallas_call(
        _merge_kernel,
        grid=(dm // tn, m // tm),
        in_specs=[pl.BlockSpec((tm, dm), lambda j, i: (i, 0)),
                  pl.BlockSpec((tm, kd), lambda j, i: (i, 0)),
                  pl.BlockSpec((tm, ks), lambda j, i: (i, 0)),
                  pl.BlockSpec((dm, tn), lambda j, i: (0, j)),
                  pl.BlockSpec((dm, tn), lambda j, i: (0, j)),
                  pl.BlockSpec((kd, tn), lambda j, i: (0, j)),
                  pl.BlockSpec((ks, tn), lambda j, i: (0, j))],
        out_specs=pl.BlockSpec((tm, tn), lambda j, i: (i, j)),
        out_shape=jax.ShapeDtypeStruct((m, dm), BF16),
        compiler_params=_params(2, 40),
    )(xb, o_d, o_s, wgd, wgs, wpd, wps)


def _mm_res_ln_kernel(a_ref, w_ref, res_ref, g_ref, b_ref, o_ref, ob_ref):
    y = jnp.dot(a_ref[...], w_ref[...], preferred_element_type=F32)
    zn = _layer_norm(DN_ALPHA * res_ref[...] + y, g_ref[...], b_ref[...])
    o_ref[...] = zn
    ob_ref[...] = zn.astype(BF16)


def _matmul_res_ln(a, w, res, g, b):
    m, k = a.shape
    n = w.shape[1]
    tm = min(256, m)
    return pl.pallas_call(
        _mm_res_ln_kernel,
        grid=(m // tm,),
        in_specs=[pl.BlockSpec((tm, k), lambda i: (i, 0)),
                  pl.BlockSpec((k, n), lambda i: (0, 0)),
                  pl.BlockSpec((tm, n), lambda i: (i, 0)),
                  pl.BlockSpec((1, n), lambda i: (0, 0)),
                  pl.BlockSpec((1, n), lambda i: (0, 0))],
        out_specs=[pl.BlockSpec((tm, n), lambda i: (i, 0)), pl.BlockSpec((tm, n), lambda i: (i, 0))],
        out_shape=[jax.ShapeDtypeStruct((m, n), F32), jax.ShapeDtypeStruct((m, n), BF16)],
        compiler_params=_params(1, 40),
    )(a, w, res, g.reshape(1, n), b.reshape(1, n))


def _pack_bf16_pairs(zb):
    half = zb.shape[1] // 2
    bits = lax.bitcast_convert_type(zb.astype(F32), jnp.uint32)
    return (bits[:, :half] & jnp.uint32(0xFFFF0000)) | (bits[:, half:] >> 16)


def _unpack_bf16_pairs(w):
    hi = lax.bitcast_convert_type(w & jnp.uint32(0xFFFF0000), F32).astype(BF16)
    lo = lax.bitcast_convert_type(w << 16, F32).astype(BF16)
    return hi, lo


def _xattn_kernel(hb_ref, h_ref, wq_ref, kv_ref, wo_ref, g_ref, b_ref, o_ref, ob_ref, op_ref):
    d = XATTN_HEAD_DIM
    hd = XATTN_HEADS * d
    scale = d ** -0.5
    qb = jnp.dot(hb_ref[...], wq_ref[...], preferred_element_type=F32).astype(BF16)
    kv = kv_ref[...]
    outs = []
    for h in range(XATTN_HEADS):
        s = lax.dot_general(qb[:, h * d:(h + 1) * d], kv[:, h * d:(h + 1) * d], NT_DIMS,
                            preferred_element_type=F32) * scale
        p = jnp.exp(s - jnp.max(s, axis=1, keepdims=True))
        l = jnp.sum(p, axis=1, keepdims=True)
        oh = jnp.dot(p.astype(BF16), kv[:, hd + h * d:hd + (h + 1) * d], preferred_element_type=F32)
        outs.append((oh / l).astype(BF16))
    o = jnp.concatenate(outs, axis=1)
    y = jnp.dot(o, wo_ref[...], preferred_element_type=F32)
    zn = _layer_norm(DN_ALPHA * h_ref[...] + y, g_ref[...], b_ref[...])
    zb = zn.astype(BF16)
    o_ref[...] = zn
    ob_ref[...] = zb
    op_ref[...] = _pack_bf16_pairs(zb)


def _cross_attention(hb, h, wq, kv, wo, g, b, batch, seq, n_mem):
    m, dm = h.shape
    tm = min(256, seq)
    nq = seq // tm
    hd = XATTN_HEADS * XATTN_HEAD_DIM
    return pl.pallas_call(
        _xattn_kernel,
        grid=(batch, nq),
        in_specs=[pl.BlockSpec((tm, dm), lambda bb, i: (bb * nq + i, 0)),
                  pl.BlockSpec((tm, dm), lambda bb, i: (bb * nq + i, 0)),
                  pl.BlockSpec((dm, hd), lambda bb, i: (0, 0)),
                  pl.BlockSpec((n_mem, 2 * hd), lambda bb, i: (bb, 0)),
                  pl.BlockSpec((hd, dm), lambda bb, i: (0, 0)),
                  pl.BlockSpec((1, dm), lambda bb, i: (0, 0)),
                  pl.BlockSpec((1, dm), lambda bb, i: (0, 0))],
        out_specs=[pl.BlockSpec((tm, dm), lambda bb, i: (bb * nq + i, 0)),
                   pl.BlockSpec((tm, dm), lambda bb, i: (bb * nq + i, 0)),
                   pl.BlockSpec((tm, dm // 2), lambda bb, i: (bb * nq + i, 0))],
        out_shape=[jax.ShapeDtypeStruct((m, dm), F32), jax.ShapeDtypeStruct((m, dm), BF16),
                   jax.ShapeDtypeStruct((m, dm // 2), jnp.uint32)],
        compiler_params=_params(2, 40),
    )(hb, h, wq, kv, wo, g.reshape(1, dm), b.reshape(1, dm))


def _split_bf16(x):
    hi = x.astype(BF16)
    return hi, (x - hi.astype(F32)).astype(BF16)


def _router_kernel(h_ref, rwt_ref, rb_ref, idx_ref, gate_ref, rank_ref, cnt_ref, cnt_scr):
    tm = h_ref.shape[0]
    per = N_EXPERTS // N_GROUPS
    h_hi, h_lo = _split_bf16(h_ref[...])
    w_hi, w_lo = _split_bf16(rwt_ref[...])
    logits = (lax.dot_general(w_hi, h_hi, NT_DIMS, preferred_element_type=F32)
              + lax.dot_general(w_hi, h_lo, NT_DIMS, preferred_element_type=F32)
              + lax.dot_general(w_lo, h_hi, NT_DIMS, preferred_element_type=F32))
    scores = _sigmoid(logits)
    sel = scores + rb_ref[...]

    sel3 = sel.reshape(N_GROUPS, per, tm)
    r_iota = lax.broadcasted_iota(jnp.int32, sel3.shape, 1).astype(F32)
    m1 = jnp.max(sel3, axis=1, keepdims=True)
    first = jnp.min(jnp.where(sel3 == m1, r_iota, float(per)), axis=1, keepdims=True)
    m2 = jnp.max(jnp.where(r_iota == first, -jnp.inf, sel3), axis=1, keepdims=True)
    gscore = (m1 + m2).reshape(N_GROUPS, tm)

    g_iota = lax.broadcasted_iota(jnp.int32, gscore.shape, 0).astype(F32)
    keep = jnp.zeros(gscore.shape, F32)
    cur = gscore
    for _ in range(TOPK_GROUPS):
        m = jnp.max(cur, axis=0, keepdims=True)
        first = jnp.min(jnp.where(cur == m, g_iota, float(N_GROUPS)), axis=0, keepdims=True)
        pick = g_iota == first
        keep = jnp.where(pick, 1.0, keep)
        cur = jnp.where(pick, -jnp.inf, cur)
    keep3 = jnp.broadcast_to(keep.reshape(N_GROUPS, 1, tm), sel3.shape)
    cur = jnp.where(keep3 > 0.5, sel3, -jnp.inf).reshape(N_EXPERTS, tm)

    e_iota = lax.broadcasted_iota(jnp.int32, cur.shape, 0).astype(F32)
    gsum = jnp.zeros((1, tm), F32)
    gates, picks = [], []
    for k in range(TOP_K):
        m = jnp.max(cur, axis=0, keepdims=True)
        first = jnp.min(jnp.where(cur == m, e_iota, float(N_EXPERTS)), axis=0, keepdims=True)
        pick = e_iota == first
        gk = jnp.sum(jnp.where(pick, scores, 0.0), axis=0, keepdims=True)
        idx_ref[k:k + 1, :] = first.astype(jnp.int32)
        gates.append(gk)
        picks.append(pick)
        gsum = gsum + gk
        cur = jnp.where(pick, -jnp.inf, cur)
    for k in range(TOP_K):
        gate_ref[k:k + 1, :] = gates[k] / gsum * ROUTED_SCALE

    @pl.when(pl.program_id(0) == 0)
    def _():
        cnt_scr[...] = jnp.zeros(cnt_scr.shape, F32)

    mask = jnp.zeros(cur.shape, F32)
    for k in range(TOP_K):
        mask = jnp.where(picks[k], 1.0, mask)
    upper = (lax.broadcasted_iota(jnp.int32, (tm, tm), 0) <= lax.broadcasted_iota(jnp.int32, (tm, tm), 1))
    cum = jnp.dot(mask.astype(BF16), jnp.where(upper, 1.0, 0.0).astype(BF16), preferred_element_type=F32)
    before = cnt_scr[...][:, 0:1] + cum - mask
    for k in range(TOP_K):
        rank_ref[k:k + 1, :] = jnp.sum(jnp.where(picks[k], before, 0.0), axis=0, keepdims=True).astype(jnp.int32)
    cnt_scr[...] = cnt_scr[...] + jnp.sum(mask, axis=1, keepdims=True)
    cnt_ref[...] = cnt_scr[...].astype(jnp.int32)


def _router(h, router_w, router_bias):
    m, dm = h.shape
    tm = min(512, m)
    return pl.pallas_call(
        _router_kernel,
        grid=(m // tm,),
        in_specs=[pl.BlockSpec((tm, dm), lambda i: (i, 0)),
                  pl.BlockSpec((N_EXPERTS, dm), lambda i: (0, 0)),
                  pl.BlockSpec((N_EXPERTS, 1), lambda i: (0, 0))],
        out_specs=[pl.BlockSpec((TOP_K, tm), lambda i: (0, i)), pl.BlockSpec((TOP_K, tm), lambda i: (0, i)),
                   pl.BlockSpec((TOP_K, tm), lambda i: (0, i)), pl.BlockSpec((N_EXPERTS, 128), lambda i: (0, 0))],
        out_shape=[jax.ShapeDtypeStruct((TOP_K, m), jnp.int32), jax.ShapeDtypeStruct((TOP_K, m), F32),
                   jax.ShapeDtypeStruct((TOP_K, m), jnp.int32), jax.ShapeDtypeStruct((N_EXPERTS, 128), jnp.int32)],
        scratch_shapes=[pltpu.VMEM((N_EXPERTS, 128), F32)],
        compiler_params=_params(1, 32),
    )(h, router_w.T, router_bias.reshape(N_EXPERTS, 1))


def _dest_kernel(idx_ref, rank_ref, ps_ref, dest_ref):
    tm = idx_ref.shape[1]
    e_iota = lax.broadcasted_iota(jnp.int32, (N_EXPERTS, tm), 0)
    ps = ps_ref[...][:, 0:1]
    for k in range(TOP_K):
        base = jnp.sum(jnp.where(e_iota == idx_ref[k:k + 1, :], ps, 0), axis=0, keepdims=True)
        dest_ref[k:k + 1, :] = rank_ref[k:k + 1, :] + base


def _dest_rows(idx_t, rank_t, pad_start):
    m = idx_t.shape[1]
    tm = min(2048, m)
    ps = jnp.broadcast_to(pad_start.astype(F32).reshape(N_EXPERTS, 1), (N_EXPERTS, 128))
    out = pl.pallas_call(
        _dest_kernel,
        grid=(m // tm,),
        in_specs=[pl.BlockSpec((TOP_K, tm), lambda i: (0, i)), pl.BlockSpec((TOP_K, tm), lambda i: (0, i)),
                  pl.BlockSpec((N_EXPERTS, 128), lambda i: (0, 0))],
        out_specs=pl.BlockSpec((TOP_K, tm), lambda i: (0, i)),
        out_shape=jax.ShapeDtypeStruct((TOP_K, m), F32),
        compiler_params=_params(1, 32),
    )(idx_t, rank_t.astype(F32), ps)
    return out.astype(jnp.int32)


def _dispatch_kernel(dest_ref, x_ref, init_ref, xs_ref, sem):
    del init_ref
    tm = x_ref.shape[0]

    def copy(t, k):
        return pltpu.make_async_copy(x_ref.at[pl.ds(t, 1), :], xs_ref.at[pl.ds(dest_ref[0, k, t], 1), :], sem)

    def start(t, carry):
        for k in range(TOP_K):
            copy(t, k).start()
        return carry

    lax.fori_loop(0, tm, start, 0)
    n_rows = TOP_K * tm
    pltpu.make_async_copy(xs_ref.at[pl.ds(0, n_rows), :], xs_ref.at[pl.ds(n_rows, n_rows), :], sem).wait()


def _dispatch(xp, dest, rows):
    m, wd = xp.shape
    tm = min(512, m)
    dest3 = dest.reshape(TOP_K, m // tm, tm).transpose(1, 0, 2)
    return pl.pallas_call(
        _dispatch_kernel,
        grid=(m // tm,),
        in_specs=[pl.BlockSpec((1, TOP_K, tm), lambda i: (i, 0, 0), memory_space=pltpu.SMEM),
                  pl.BlockSpec((tm, wd), lambda i: (i, 0)),
                  pl.BlockSpec(memory_space=pl.ANY)],
        out_specs=pl.BlockSpec(memory_space=pl.ANY),
        out_shape=jax.ShapeDtypeStruct((rows, wd), xp.dtype),
        scratch_shapes=[pltpu.SemaphoreType.DMA(())],
        input_output_aliases={2: 0},
        compiler_params=_params(1, 32),
    )(dest3, xp, jnp.zeros((rows, wd), xp.dtype))


def _expert_kernel(be_ref, nu_ref, x_ref, w1_ref, w3_ref, w2_ref, o_ref, w1b, w3b, w2b):
    i = pl.program_id(0)
    changed = jnp.logical_or(i == 0, be_ref[i] != be_ref[jnp.maximum(i - 1, 0)])

    @pl.when(changed)
    def _():
        w1b[...] = w1_ref[...].astype(BF16)
        w3b[...] = w3_ref[...].astype(BF16)
        w2b[...] = w2_ref[...].astype(BF16)

    @pl.when(i < nu_ref[0])
    def _():
        xh, xl = _unpack_bf16_pairs(x_ref[...])
        half = xh.shape[1]
        a = (jnp.dot(xh, w1b[0:half, :], preferred_element_type=F32)
             + jnp.dot(xl, w1b[half:, :], preferred_element_type=F32))
        b = (jnp.dot(xh, w3b[0:half, :], preferred_element_type=F32)
             + jnp.dot(xl, w3b[half:, :], preferred_element_type=F32))
        hmid = (a * _sigmoid(a) * b).astype(BF16)
        o_ref[...] = _pack_bf16_pairs(jnp.dot(hmid, w2b[...], preferred_element_type=F32).astype(BF16))

    @pl.when(i >= nu_ref[0])
    def _():
        o_ref[...] = jnp.zeros(o_ref.shape, o_ref.dtype)


def _expert_blocks(x_sorted, block_e, n_used, w1, w3, w2):
    rows, wd = x_sorted.shape
    n_blocks = rows // MOE_ROWS
    dm, de = w1.shape[1], w1.shape[2]
    grid_spec = pltpu.PrefetchScalarGridSpec(
        num_scalar_prefetch=2,
        grid=(n_blocks,),
        in_specs=[pl.BlockSpec((MOE_ROWS, wd), lambda i, be, nu: (i, 0)),
                  pl.BlockSpec((None, dm, de), lambda i, be, nu: (be[i], 0, 0)),
                  pl.BlockSpec((None, dm, de), lambda i, be, nu: (be[i], 0, 0)),
                  pl.BlockSpec((None, de, dm), lambda i, be, nu: (be[i], 0, 0))],
        out_specs=pl.BlockSpec((MOE_ROWS, wd), lambda i, be, nu: (i, 0)),
        scratch_shapes=[pltpu.VMEM((dm, de), BF16), pltpu.VMEM((dm, de), BF16), pltpu.VMEM((de, dm), BF16)],
    )
    return pl.pallas_call(
        _expert_kernel,
        grid_spec=grid_spec,
        out_shape=jax.ShapeDtypeStruct((rows, wd), jnp.uint32),
        compiler_params=_params(1, 56),
    )(block_e, n_used, x_sorted, w1, w3, w2)


def _shared_kernel(x_ref, w1_ref, w3_ref, w2_ref, o_ref):
    x = x_ref[...]
    a = jnp.dot(x, w1_ref[...], preferred_element_type=F32)
    b = jnp.dot(x, w3_ref[...], preferred_element_type=F32)
    hmid = (a * _sigmoid(a) * b).astype(BF16)
    o_ref[...] = jnp.dot(hmid, w2_ref[...], preferred_element_type=F32)


def _shared_expert(xb, w1, w3, w2):
    m, dm = xb.shape
    de = w1.shape[1]
    tm = min(512, m)
    return pl.pallas_call(
        _shared_kernel,
        grid=(m // tm,),
        in_specs=[pl.BlockSpec((tm, dm), lambda i: (i, 0)),
                  pl.BlockSpec((dm, de), lambda i: (0, 0)),
                  pl.BlockSpec((dm, de), lambda i: (0, 0)),
                  pl.BlockSpec((de, dm), lambda i: (0, 0))],
        out_specs=pl.BlockSpec((tm, dm), lambda i: (i, 0)),
        out_shape=jax.ShapeDtypeStruct((m, dm), F32),
        compiler_params=_params(1, 32),
    )(xb, w1, w3, w2)


def _combine_kernel(dcur_ref, dnext_ref, h_ref, sh_ref, gate_ref, g_ref, b_ref, y_ref, o_ref, ybuf, sems):
    i = pl.program_id(0)
    n = pl.num_programs(0)
    tm = h_ref.shape[0]
    slot = lax.rem(i, 2)

    def copy(d_ref, s, t, k):
        return pltpu.make_async_copy(y_ref.at[pl.ds(d_ref[0, k, t], 1), :],
                                     ybuf.at[s, pl.ds(k * tm + t, 1), :], sems.at[s])

    def start_tile(d_ref, s):
        def body(t, carry):
            for k in range(TOP_K):
                copy(d_ref, s, t, k).start()
            return carry
        lax.fori_loop(0, tm, body, 0)

    @pl.when(i == 0)
    def _():
        start_tile(dcur_ref, 0)

    @pl.when(i + 1 < n)
    def _():
        start_tile(dnext_ref, 1 - slot)

    pltpu.make_async_copy(y_ref.at[pl.ds(0, TOP_K * tm), :], ybuf.at[slot], sems.at[slot]).wait()

    gate = gate_ref[...]
    half = ybuf.shape[2]
    yh = jnp.zeros((tm, half), F32)
    yl = jnp.zeros((tm, half), F32)
    for k in range(TOP_K):
        w = ybuf[slot, k * tm:(k + 1) * tm, :]
        gk = gate[:, k:k + 1]
        yh = yh + lax.bitcast_convert_type(w & jnp.uint32(0xFFFF0000), F32) * gk
        yl = yl + lax.bitcast_convert_type(w << 16, F32) * gk
    y = sh_ref[...] + jnp.concatenate([yh, yl], axis=1)
    o_ref[...] = _layer_norm(DN_ALPHA * h_ref[...] + y, g_ref[...], b_ref[...])


def _combine_ln(h, shared, y_sorted, dest, gate, g, b):
    m, dm = h.shape
    tm = min(128, m)
    n = m // tm
    dest3 = dest.reshape(TOP_K, n, tm).transpose(1, 0, 2)
    return pl.pallas_call(
        _combine_kernel,
        grid=(n,),
        in_specs=[pl.BlockSpec((1, TOP_K, tm), lambda i: (i, 0, 0), memory_space=pltpu.SMEM),
                  pl.BlockSpec((1, TOP_K, tm), lambda i: (jnp.minimum(i + 1, n - 1), 0, 0), memory_space=pltpu.SMEM),
                  pl.BlockSpec((tm, dm), lambda i: (i, 0)),
                  pl.BlockSpec((tm, dm), lambda i: (i, 0)),
                  pl.BlockSpec((tm, TOP_K), lambda i: (i, 0)),
                  pl.BlockSpec((1, dm), lambda i: (0, 0)),
                  pl.BlockSpec((1, dm), lambda i: (0, 0)),
                  pl.BlockSpec(memory_space=pl.ANY)],
        out_specs=pl.BlockSpec((tm, dm), lambda i: (i, 0)),
        out_shape=jax.ShapeDtypeStruct((m, dm), F32),
        scratch_shapes=[pltpu.VMEM((2, TOP_K * tm, dm // 2), jnp.uint32), pltpu.SemaphoreType.DMA((2,))],
        compiler_params=_params(1, 40),
    )(dest3, dest3, h, shared, gate, g.reshape(1, dm), b.reshape(1, dm), y_sorted)


def _moe_sublayer(h, hb, hp, router_w, router_bias, w1, w3, w2, sw1, sw3, sw2, g, b):
    n_tok, dm = h.shape
    idx_t, gate_t, rank_t, counts = _router(h, router_w, router_bias)
    counts = counts[:, 0]
    padded = (counts + MOE_ROWS - 1) // MOE_ROWS * MOE_ROWS
    pad_end = jnp.cumsum(padded)
    n_blocks = (n_tok * TOP_K + N_EXPERTS * (MOE_ROWS - 1) + MOE_ROWS - 1) // MOE_ROWS
    block_start = jnp.arange(n_blocks, dtype=jnp.int32) * MOE_ROWS
    block_e = jnp.minimum(jnp.sum((pad_end[None, :] <= block_start[:, None]).astype(jnp.int32), axis=1),
                          N_EXPERTS - 1)
    n_used = (pad_end[-1] // MOE_ROWS).astype(jnp.int32).reshape(1)
    dest = _dest_rows(idx_t, rank_t, pad_end - padded)
    x_sorted = _dispatch(hp, dest, n_blocks * MOE_ROWS)
    y_sorted = _expert_blocks(x_sorted, block_e, n_used, w1, w3, w2)
    shared = _shared_expert(hb, sw1.astype(BF16), sw3.astype(BF16), sw2.astype(BF16))
    return _combine_ln(h, shared, y_sorted, dest, gate_t.T, g, b)


def kernel(x, mem, positions, rel_bias_table, w_in, diff_lambda, diff_subln, idx_k_g, idx_k_b, kv_norm_g, w_uk, w_uv, w_proj_diff, w_proj_dsa, w_mix_out, ln1_g, ln1_b, xattn_wq, xattn_wkv, xattn_wo, ln2_g, ln2_b, router_w, router_bias, exp_w1, exp_w3, exp_w2, sh_w1, sh_w3, sh_w2, ln3_g, ln3_b):
    del positions
    batch, seq, dm = x.shape
    n_mem = mem.shape[1]
    n_tok = batch * seq
    lambda_init = 0.8 - 0.6 * math.exp(-0.3 * 0)
    blk = min(ATT_BLOCK, seq)

    dq = DIFF_HEADS * DIFF_V_DIM
    c_qs = 3 * dq
    c_ckv = c_qs + DSA_HEADS * DSA_HEAD_DIM
    c_qi = c_ckv + DSA_KV_RANK
    c_ki = c_qi + IDX_HEADS * IDX_HEAD_DIM
    c_gd = c_ki + IDX_HEAD_DIM + IDX_HEADS
    c_gs = c_gd + dm

    xf = x.reshape(n_tok, dm)
    xb = xf.astype(BF16)
    w_in0 = w_in[0]
    qkv = _matmul(xb, w_in0, 0, c_ckv, BF16, 1024, 1024)
    ckv = _matmul(xb, w_in0, c_ckv, DSA_KV_RANK, F32, 1024, 512)
    qi = _matmul(xb, w_in0, c_qi, c_ki - c_qi, BF16, 1024, 512)
    w_tail = jnp.pad(w_in0[:, c_ki:c_gd], ((0, 0), (0, 128 - (c_gd - c_ki))))
    kiw = _matmul(xb, w_tail, 0, 128, F32, 1024, 128)

    dsa_tiles = _rel_bias_tiles(rel_bias_table[:, DIFF_HEADS:], blk)
    diff_tiles = _rel_bias_tiles(rel_bias_table[:, :DIFF_HEADS], min(DIFF_BLOCK, seq))
    o_d = _diff_attention(qkv, diff_tiles, diff_lambda[0], diff_subln[0], batch, seq, lambda_init)

    k_sel = min(DSA_TOPK_MAX, seq // 4)
    sel = _indexer_mask(qi, kiw, idx_k_g[0], idx_k_b[0], batch, seq, k_sel, qi_col_block=0)
    o_s = _dsa_attention(qkv, c_qs // (DSA_HEADS * DSA_HEAD_DIM), ckv, 0, kv_norm_g[0], sel,
                         w_uk[0].astype(BF16), w_uv[0].astype(BF16), dsa_tiles, batch, seq)

    merged = _gated_merge(xb, o_d, o_s, w_in0[:, c_gd:c_gs].astype(BF16), w_in0[:, c_gs:c_gs + dm].astype(BF16),
                          w_proj_diff[0].astype(BF16), w_proj_dsa[0].astype(BF16))
    h1, h1b = _matmul_res_ln(merged, w_mix_out[0].astype(BF16), xf, ln1_g[0], ln1_b[0])

    memb = mem.reshape(batch * n_mem, dm).astype(BF16)
    kv = _matmul(memb, xattn_wkv[0], 0, 2 * XATTN_HEADS * XATTN_HEAD_DIM, BF16, 1024, 512)
    h2, h2b, h2p = _cross_attention(h1b, h1, xattn_wq[0].astype(BF16), kv, xattn_wo[0].astype(BF16),
                                    ln2_g[0], ln2_b[0], batch, seq, n_mem)

    out = _moe_sublayer(h2, h2b, h2p, router_w[0], router_bias[0], exp_w1[0], exp_w3[0], exp_w2[0],
                        sh_w1[0], sh_w3[0], sh_w2[0], ln3_g[0], ln3_b[0])
    return out.reshape(batch, seq, dm)
```

```python
import functools
import math

import jax
import jax.numpy as jnp
from jax import lax
from jax.experimental import pallas as pl
from jax.experimental.pallas import tpu as pltpu

F32 = jnp.float32
BF16 = jnp.bfloat16
NEG = -1e30
INT_MIN = -(2 ** 31)
MIB = 1024 * 1024

LN_EPS = 1e-5
DIFF_HEADS = 8
DIFF_HEAD_DIM = 128
DIFF_V_DIM = 2 * DIFF_HEAD_DIM
DSA_HEADS = 8
DSA_HEAD_DIM = 128
DSA_KV_RANK = 512
IDX_HEADS = 16
IDX_HEAD_DIM = 64
DSA_TOPK_MAX = 256
REL_BUCKETS = 32
REL_MAX_DIST = 128
XATTN_HEADS = 4
XATTN_HEAD_DIM = 128
N_EXPERTS = 64
EXPERT_DIM = 512
TOP_K = 8
N_GROUPS = 8
TOPK_GROUPS = 4
ROUTED_SCALE = 2.5
DEPTH = 1
DN_ALPHA = (2 * DEPTH) ** 0.25

ATT_BLOCK = 256
DIFF_BLOCK = 512
MOE_ROWS = 256
NT_DIMS = (((1,), (1,)), ((), ()))


def _params(n_grid, vmem_mib):
    return pltpu.CompilerParams(dimension_semantics=("arbitrary",) * n_grid,
                                vmem_limit_bytes=vmem_mib * MIB)


def _sigmoid(x):
    return 1.0 / (1.0 + jnp.exp(-x))


def _layer_norm(z, g, b):
    mu = jnp.mean(z, axis=1, keepdims=True)
    zc = z - mu
    var = jnp.mean(zc * zc, axis=1, keepdims=True)
    return zc * lax.rsqrt(var + LN_EPS) * g + b


def _mm_kernel(a_ref, b_ref, o_ref, bq_ref):
    @pl.when(pl.program_id(1) == 0)
    def _():
        bq_ref[...] = b_ref[...].astype(BF16)

    o_ref[...] = jnp.dot(a_ref[...], bq_ref[...], preferred_element_type=F32).astype(o_ref.dtype)


def _matmul(a, b, col_start, n_cols, out_dtype, tm, tn):
    m, k = a.shape
    tm = min(tm, m)
    assert col_start % tn == 0 and n_cols % tn == 0 and m % tm == 0
    off = col_start // tn
    return pl.pallas_call(
        _mm_kernel,
        grid=(n_cols // tn, m // tm),
        in_specs=[pl.BlockSpec((tm, k), lambda j, i: (i, 0)),
                  pl.BlockSpec((k, tn), lambda j, i: (0, j + off))],
        out_specs=pl.BlockSpec((tm, tn), lambda j, i: (i, j)),
        out_shape=jax.ShapeDtypeStruct((m, n_cols), out_dtype),
        scratch_shapes=[pltpu.VMEM((k, tn), BF16)],
        compiler_params=_params(2, 48),
    )(a, b)


def _rel_bucket(n):
    n = jnp.maximum(n, 0)
    max_exact = REL_BUCKETS // 2
    nf = jnp.maximum(n, 1).astype(F32)
    large = max_exact + (jnp.log(nf / max_exact) / math.log(REL_MAX_DIST / max_exact)
                         * (REL_BUCKETS - max_exact)).astype(jnp.int32)
    large = jnp.minimum(large, REL_BUCKETS - 1)
    return jnp.where(n < max_exact, n, large)


def _rel_bias_tiles(table, blk):
    assert blk >= REL_MAX_DIST
    r = jnp.arange(blk, dtype=jnp.int32)[:, None]
    c = jnp.arange(blk, dtype=jnp.int32)[None, :]
    tiles = []
    for back in range(3):
        n = back * blk + r - c
        bucket = _rel_bucket(n)
        t = jnp.zeros((table.shape[1], blk, blk), F32)
        for bkt in range(REL_BUCKETS):
            t = jnp.where(bucket == bkt, table[bkt][:, None, None], t)
        tiles.append(jnp.where(n >= 0, t, NEG))
    return jnp.stack(tiles, axis=1).astype(F32)


def _lane_repeat(x, n, axis=1):
    assert axis == 1
    return x if n == 1 else jnp.concatenate([x] * n, axis=1)


def _online_softmax_step(s, v, m_ref, l_ref, acc_ref):
    tk = s.shape[1]
    e = v.shape[1]
    m_prev = m_ref[...]
    m_next = jnp.maximum(m_prev, jnp.max(s, axis=1, keepdims=True))
    p = jnp.exp(s - _lane_repeat(m_next, tk // 128, axis=1))
    alpha = jnp.exp(m_prev - m_next)
    l_ref[...] = alpha * l_ref[...] + jnp.sum(p, axis=1, keepdims=True)
    m_ref[...] = m_next
    acc_ref[...] = (acc_ref[...] * _lane_repeat(alpha, e // 128, axis=1)
                    + jnp.dot(p.astype(BF16), v, preferred_element_type=F32))


def _diff_attn_kernel(q_ref, k_ref, v_ref, bias_ref, lam_ref, g_ref, o_ref, m_scr, l_scr, acc_scr,
                      *, blk, lambda_init):
    i = pl.program_id(2)
    d = DIFF_HEAD_DIM
    scale = d ** -0.5
    m_scr[...] = jnp.full(m_scr.shape, NEG, F32)
    l_scr[...] = jnp.zeros(l_scr.shape, F32)
    acc_scr[...] = jnp.zeros(acc_scr.shape, F32)
    q = q_ref[...]

    def body(j, carry):
        start = pl.multiple_of(j * blk, blk)
        kk = k_ref[pl.ds(start, blk), :]
        vv = v_ref[pl.ds(start, blk), :]
        bt = bias_ref[jnp.minimum(i - j, 2)]
        for c in range(2):
            s = lax.dot_general(q[:, c * d:(c + 1) * d], kk[:, c * d:(c + 1) * d], NT_DIMS,
                                preferred_element_type=F32)
            _online_softmax_step(s * scale + bt, vv, m_scr.at[c], l_scr.at[c], acc_scr.at[c])
        return carry

    lax.fori_loop(0, i + 1, body, 0)

    lp = lam_ref[...]
    lam = (jnp.exp(jnp.sum(lp[0:1] * lp[1:2], axis=1, keepdims=True))
           - jnp.exp(jnp.sum(lp[2:3] * lp[3:4], axis=1, keepdims=True)) + lambda_init)
    o0 = acc_scr[0] * _lane_repeat(1.0 / l_scr[0], 2, axis=1)
    o1 = acc_scr[1] * _lane_repeat(1.0 / l_scr[1], 2, axis=1)
    o = o0 - lam * o1
    ms = jnp.mean(o * o, axis=1, keepdims=True)
    o_ref[...] = (o * lax.rsqrt(ms + LN_EPS) * g_ref[...] * (1.0 - lambda_init)).astype(o_ref.dtype)


def _diff_attention(qkv, bias_tiles, diff_lambda, subln, batch, seq, lambda_init):
    blk = bias_tiles.shape[2]
    nq = seq // blk
    h2d = DIFF_V_DIM
    kern = functools.partial(_diff_attn_kernel, blk=blk, lambda_init=lambda_init)
    return pl.pallas_call(
        kern,
        grid=(batch, DIFF_HEADS, nq),
        in_specs=[pl.BlockSpec((blk, h2d), lambda b, h, i: (b * nq + i, h)),
                  pl.BlockSpec((seq, h2d), lambda b, h, i: (b, DIFF_HEADS + h)),
                  pl.BlockSpec((seq, h2d), lambda b, h, i: (b, 2 * DIFF_HEADS + h)),
                  pl.BlockSpec((None, 3, blk, blk), lambda b, h, i: (h, 0, 0, 0)),
                  pl.BlockSpec((4, DIFF_HEAD_DIM), lambda b, h, i: (0, 0)),
                  pl.BlockSpec((1, h2d), lambda b, h, i: (0, 0))],
        out_specs=pl.BlockSpec((blk, h2d), lambda b, h, i: (b * nq + i, h)),
        out_shape=jax.ShapeDtypeStruct((batch * seq, DIFF_HEADS * h2d), BF16),
        scratch_shapes=[pltpu.VMEM((2, blk, 128), F32), pltpu.VMEM((2, blk, 128), F32),
                        pltpu.VMEM((2, blk, h2d), F32)],
        compiler_params=_params(3, 48),
    )(qkv, qkv, qkv, bias_tiles, diff_lambda, subln.reshape(1, h2d))


def _indexer_kernel(qi_ref, kfull_ref, wblk_ref, g_ref, b_ref, o_ref, ki_scr, key_scr, wb_scr, *, blk, seq, k_sel):
    i = pl.program_id(1)

    @pl.when(i == 0)
    def _():
        kr = kfull_ref[:, 0:IDX_HEAD_DIM]
        ki_scr[...] = _layer_norm(kr, g_ref[...], b_ref[...]).astype(BF16)

    w = wblk_ref[:, IDX_HEAD_DIM:IDX_HEAD_DIM + IDX_HEADS] * (IDX_HEADS ** -0.5 * IDX_HEAD_DIM ** -0.5)
    for h in range(IDX_HEADS):
        wb_scr[h] = jnp.broadcast_to(w[:, h:h + 1], (blk, blk))
    qi = qi_ref[...].astype(BF16)
    row = i * blk + lax.broadcasted_iota(jnp.int32, (blk, blk), 0)
    col = lax.broadcasted_iota(jnp.int32, (blk, blk), 1)

    def score_tile(j, carry):
        start = pl.multiple_of(j * blk, blk)
        kc = ki_scr[pl.ds(start, blk), :]
        score = jnp.zeros((blk, blk), F32)
        for h in range(IDX_HEADS):
            lg = lax.dot_general(qi[:, h * IDX_HEAD_DIM:(h + 1) * IDX_HEAD_DIM], kc, NT_DIMS,
                                 preferred_element_type=F32)
            score = score + jnp.maximum(lg, 0.0) * wb_scr[h]
        bits = lax.bitcast_convert_type(score, jnp.int32)
        key = bits ^ ((bits >> 31) & jnp.int32(0x7FFFFFFF))
        key_scr[j] = jnp.where(start + col <= row, key, INT_MIN)
        return carry

    lax.fori_loop(0, i + 1, score_tile, 0)

    def search(it, thr):
        trial = thr + lax.shift_left(jnp.int32(1), 31 - it)

        def count_tile(j, c):
            ge = jnp.where(key_scr[j] >= trial, 1.0, 0.0)
            for part in range(blk // 128):
                c = c + ge[:, part * 128:(part + 1) * 128]
            return c

        c = lax.fori_loop(0, i + 1, count_tile, jnp.zeros((blk, 128), F32))
        return jnp.where(jnp.sum(c, axis=1, keepdims=True) >= k_sel, trial, thr)

    thr = lax.fori_loop(0, 32, search, jnp.full((blk, 1), INT_MIN, jnp.int32))
    for jj in range(seq // blk):
        @pl.when(jj <= i)
        def _():
            keep = (jj * blk + col <= row) & (key_scr[jj] >= thr)
            o_ref[jj] = jnp.where(keep, 0.0, NEG).astype(o_ref.dtype)

        @pl.when(jj > i)
        def _():
            o_ref[jj] = jnp.full((blk, blk), NEG, o_ref.dtype)


def _indexer_mask(qi, kiw, idx_k_g, idx_k_b, batch, seq, k_sel, qi_col_block):
    blk = min(ATT_BLOCK, seq)
    nq = seq // blk
    qw = IDX_HEADS * IDX_HEAD_DIM
    kern = functools.partial(_indexer_kernel, blk=blk, seq=seq, k_sel=k_sel)
    return pl.pallas_call(
        kern,
        grid=(batch, nq),
        in_specs=[pl.BlockSpec((blk, qw), lambda b, i: (b * nq + i, qi_col_block)),
                  pl.BlockSpec((seq, 128), lambda b, i: (b, 0)),
                  pl.BlockSpec((blk, 128), lambda b, i: (b * nq + i, 0)),
                  pl.BlockSpec((1, IDX_HEAD_DIM), lambda b, i: (0, 0)),
                  pl.BlockSpec((1, IDX_HEAD_DIM), lambda b, i: (0, 0))],
        out_specs=pl.BlockSpec((None, nq, blk, blk), lambda b, i: (b, 0, i, 0)),
        out_shape=jax.ShapeDtypeStruct((batch, nq, seq, blk), BF16),
        scratch_shapes=[pltpu.VMEM((seq, IDX_HEAD_DIM), BF16), pltpu.VMEM((nq, blk, blk), jnp.int32),
                        pltpu.VMEM((IDX_HEADS, blk, blk), F32)],
        compiler_params=_params(2, 48),
    )(qi, kiw, kiw, idx_k_g.reshape(1, -1), idx_k_b.reshape(1, -1))


def _dsa_attn_kernel(qs_ref, ckv_ref, g_ref, sel_ref, wuk_ref, wuv_ref, bias_ref, o_ref,
                     c_scr, ql_scr, sb_scr, m_scr, l_scr, acc_scr, *, blk):
    i = pl.program_id(1)
    d = DSA_HEAD_DIM
    scale = d ** -0.5

    @pl.when(i == 0)
    def _():
        x = ckv_ref[...]
        c = x * lax.rsqrt(jnp.mean(x * x, axis=1, keepdims=True) + LN_EPS) * g_ref[...]
        c_scr[...] = c.astype(BF16)

    qs = qs_ref[...]
    for h in range(DSA_HEADS):
        ql = jnp.dot(qs[:, h * d:(h + 1) * d], wuk_ref[h], preferred_element_type=F32)
        ql_scr[h * blk:(h + 1) * blk, :] = (ql * scale).astype(BF16)

    m_scr[...] = jnp.full(m_scr.shape, NEG, F32)
    l_scr[...] = jnp.zeros(l_scr.shape, F32)
    acc_scr[...] = jnp.zeros(acc_scr.shape, F32)

    def body(j, carry):
        start = pl.multiple_of(j * blk, blk)
        cc = c_scr[pl.ds(start, blk), :]
        s = lax.dot_general(ql_scr[...], cc, NT_DIMS, preferred_element_type=F32)
        selm = sel_ref[j].astype(F32)
        back = jnp.minimum(i - j, 2)
        for h in range(DSA_HEADS):
            sb_scr[h * blk:(h + 1) * blk, :] = bias_ref[h, back] + selm
        _online_softmax_step(s + sb_scr[...], cc, m_scr, l_scr, acc_scr)
        return carry

    lax.fori_loop(0, i + 1, body, 0)

    o_lat = (acc_scr[...] * _lane_repeat(1.0 / l_scr[...], DSA_KV_RANK // 128, axis=1)).astype(BF16)
    for h in range(DSA_HEADS):
        o_ref[:, h * d:(h + 1) * d] = jnp.dot(o_lat[h * blk:(h + 1) * blk], wuv_ref[h],
                                              preferred_element_type=F32).astype(o_ref.dtype)


def _dsa_attention(qkv, qs_col_block, ckv, ckv_col_block, kv_norm_g, sel, w_uk, w_uv, bias_tiles, batch, seq):
    blk = min(ATT_BLOCK, seq)
    nq = seq // blk
    hd = DSA_HEADS * DSA_HEAD_DIM
    r = DSA_KV_RANK
    rows = DSA_HEADS * blk
    kern = functools.partial(_dsa_attn_kernel, blk=blk)
    return pl.pallas_call(
        kern,
        grid=(batch, nq),
        in_specs=[pl.BlockSpec((blk, hd), lambda b, i: (b * nq + i, qs_col_block)),
                  pl.BlockSpec((seq, r), lambda b, i: (b, ckv_col_block)),
                  pl.BlockSpec((1, r), lambda b, i: (0, 0)),
                  pl.BlockSpec((None, nq, blk, blk), lambda b, i: (b, 0, i, 0)),
                  pl.BlockSpec((DSA_HEADS, DSA_HEAD_DIM, r), lambda b, i: (0, 0, 0)),
                  pl.BlockSpec((DSA_HEADS, r, DSA_HEAD_DIM), lambda b, i: (0, 0, 0)),
                  pl.BlockSpec((DSA_HEADS, 3, blk, blk), lambda b, i: (0, 0, 0, 0))],
        out_specs=pl.BlockSpec((blk, hd), lambda b, i: (b * nq + i, 0)),
        out_shape=jax.ShapeDtypeStruct((batch * seq, hd), BF16),
        scratch_shapes=[pltpu.VMEM((seq, r), BF16), pltpu.VMEM((rows, r), BF16), pltpu.VMEM((rows, blk), F32),
                        pltpu.VMEM((rows, 128), F32), pltpu.VMEM((rows, 128), F32), pltpu.VMEM((rows, r), F32)],
        compiler_params=_params(2, 56),
    )(qkv, ckv, kv_norm_g.reshape(1, r), sel, w_uk, w_uv, bias_tiles)


def _merge_kernel(x_ref, od_ref, os_ref, wgd_ref, wgs_ref, wpd_ref, wps_ref, o_ref):
    x = x_ref[...]
    gd = jnp.dot(x, wgd_ref[...], preferred_element_type=F32)
    gs = jnp.dot(x, wgs_ref[...], preferred_element_type=F32)
    pd = jnp.dot(od_ref[...], wpd_ref[...], preferred_element_type=F32)
    ps = jnp.dot(os_ref[...], wps_ref[...], preferred_element_type=F32)
    o_ref[...] = (_sigmoid(gd) * pd + _sigmoid(gs) * ps).astype(o_ref.dtype)


def _gated_merge(xb, o_d, o_s, wgd, wgs, wpd, wps):
    m, dm = xb.shape
    tm = min(512, m)
    tn = 512
    kd, ks = o_d.shape[1], o_s.shape[1]
    return pl.pallas_call(
        _merge_kernel,
        grid=(dm // tn, m // tm),
        in_specs=[pl.BlockSpec((tm, dm), lambda j, i: (i, 0)),
                  pl.BlockSpec((tm, kd), lambda j, i: (i, 0)),
                  pl.BlockSpec((tm, ks), lambda j, i: (i, 0)),
                  pl.BlockSpec((dm, tn), lambda j, i: (0, j)),
                  pl.BlockSpec((dm, tn), lambda j, i: (0, j)),
                  pl.BlockSpec((kd, tn), lambda j, i: (0, j)),
                  pl.BlockSpec((ks, tn), lambda j, i: (0, j))],
        out_specs=pl.BlockSpec((tm, tn), lambda j, i: (i, j)),
        out_shape=jax.ShapeDtypeStruct((m, dm), BF16),
        compiler_params=_params(2, 40),
    )(xb, o_d, o_s, wgd, wgs, wpd, wps)


def _mm_res_ln_kernel(a_ref, w_ref, res_ref, g_ref, b_ref, o_ref, ob_ref):
    y = jnp.dot(a_ref[...], w_ref[...], preferred_element_type=F32)
    zn = _layer_norm(DN_ALPHA * res_ref[...] + y, g_ref[...], b_ref[...])
    o_ref[...] = zn
    ob_ref[...] = zn.astype(BF16)


def _matmul_res_ln(a, w, res, g, b):
    m, k = a.shape
    n = w.shape[1]
    tm = min(256, m)
    return pl.pallas_call(
        _mm_res_ln_kernel,
        grid=(m // tm,),
        in_specs=[pl.BlockSpec((tm, k), lambda i: (i, 0)),
                  pl.BlockSpec((k, n), lambda i: (0, 0)),
                  pl.BlockSpec((tm, n), lambda i: (i, 0)),
                  pl.BlockSpec((1, n), lambda i: (0, 0)),
                  pl.BlockSpec((1, n), lambda i: (0, 0))],
        out_specs=[pl.BlockSpec((tm, n), lambda i: (i, 0)), pl.BlockSpec((tm, n), lambda i: (i, 0))],
        out_shape=[jax.ShapeDtypeStruct((m, n), F32), jax.ShapeDtypeStruct((m, n), BF16)],
        compiler_params=_params(1, 40),
    )(a, w, res, g.reshape(1, n), b.reshape(1, n))


def _pack_bf16_pairs(zb):
    half = zb.shape[1] // 2
    bits = lax.bitcast_convert_type(zb.astype(F32), jnp.uint32)
    return (bits[:, :half] & jnp.uint32(0xFFFF0000)) | (bits[:, half:] >> 16)


def _unpack_bf16_pairs(w):
    hi = lax.bitcast_convert_type(w & jnp.uint32(0xFFFF0000), F32).astype(BF16)
    lo = lax.bitcast_convert_type(w << 16, F32).astype(BF16)
    return hi, lo


def _xattn_kernel(hb_ref, h_ref, wq_ref, kv_ref, wo_ref, g_ref, b_ref, o_ref, ob_ref, op_ref):
    d = XATTN_HEAD_DIM
    hd = XATTN_HEADS * d
    scale = d ** -0.5
    qb = jnp.dot(hb_ref[...], wq_ref[...], preferred_element_type=F32).astype(BF16)
    kv = kv_ref[...]
    outs = []
    for h in range(XATTN_HEADS):
        s = lax.dot_general(qb[:, h * d:(h + 1) * d], kv[:, h * d:(h + 1) * d], NT_DIMS,
                            preferred_element_type=F32) * scale
        p = jnp.exp(s - jnp.max(s, axis=1, keepdims=True))
        l = jnp.sum(p, axis=1, keepdims=True)
        oh = jnp.dot(p.astype(BF16), kv[:, hd + h * d:hd + (h + 1) * d], preferred_element_type=F32)
        outs.append((oh / l).astype(BF16))
    o = jnp.concatenate(outs, axis=1)
    y = jnp.dot(o, wo_ref[...], preferred_element_type=F32)
    zn = _layer_norm(DN_ALPHA * h_ref[...] + y, g_ref[...], b_ref[...])
    zb = zn.astype(BF16)
    o_ref[...] = zn
    ob_ref[...] = zb
    op_ref[...] = _pack_bf16_pairs(zb)


def _cross_attention(hb, h, wq, kv, wo, g, b, batch, seq, n_mem):
    m, dm = h.shape
    tm = min(256, seq)
    nq = seq // tm
    hd = XATTN_HEADS * XATTN_HEAD_DIM
    return pl.pallas_call(
        _xattn_kernel,
        grid=(batch, nq),
        in_specs=[pl.BlockSpec((tm, dm), lambda bb, i: (bb * nq + i, 0)),
                  pl.BlockSpec((tm, dm), lambda bb, i: (bb * nq + i, 0)),
                  pl.BlockSpec((dm, hd), lambda bb, i: (0, 0)),
                  pl.BlockSpec((n_mem, 2 * hd), lambda bb, i: (bb, 0)),
                  pl.BlockSpec((hd, dm), lambda bb, i: (0, 0)),
                  pl.BlockSpec((1, dm), lambda bb, i: (0, 0)),
                  pl.BlockSpec((1, dm), lambda bb, i: (0, 0))],
        out_specs=[pl.BlockSpec((tm, dm), lambda bb, i: (bb * nq + i, 0)),
                   pl.BlockSpec((tm, dm), lambda bb, i: (bb * nq + i, 0)),
                   pl.BlockSpec((tm, dm // 2), lambda bb, i: (bb * nq + i, 0))],
        out_shape=[jax.ShapeDtypeStruct((m, dm), F32), jax.ShapeDtypeStruct((m, dm), BF16),
                   jax.ShapeDtypeStruct((m, dm // 2), jnp.uint32)],
        compiler_params=_params(2, 40),
    )(hb, h, wq, kv, wo, g.reshape(1, dm), b.reshape(1, dm))


def _split_bf16(x):
    hi = x.astype(BF16)
    return hi, (x - hi.astype(F32)).astype(BF16)


def _router_kernel(h_ref, rwt_ref, rb_ref, idx_ref, gate_ref, rank_ref, cnt_ref, cnt_scr):
    tm = h_ref.shape[0]
    per = N_EXPERTS // N_GROUPS
    h_hi, h_lo = _split_bf16(h_ref[...])
    w_hi, w_lo = _split_bf16(rwt_ref[...])
    logits = (lax.dot_general(w_hi, h_hi, NT_DIMS, preferred_element_type=F32)
              + lax.dot_general(w_hi, h_lo, NT_DIMS, preferred_element_type=F32)
              + lax.dot_general(w_lo, h_hi, NT_DIMS, preferred_element_type=F32))
    scores = _sigmoid(logits)
    sel = scores + rb_ref[...]

    sel3 = sel.reshape(N_GROUPS, per, tm)
    r_iota = lax.broadcasted_iota(jnp.int32, sel3.shape, 1).astype(F32)
    m1 = jnp.max(sel3, axis=1, keepdims=True)
    first = jnp.min(jnp.where(sel3 == m1, r_iota, float(per)), axis=1, keepdims=True)
    m2 = jnp.max(jnp.where(r_iota == first, -jnp.inf, sel3), axis=1, keepdims=True)
    gscore = (m1 + m2).reshape(N_GROUPS, tm)

    g_iota = lax.broadcasted_iota(jnp.int32, gscore.shape, 0).astype(F32)
    keep = jnp.zeros(gscore.shape, F32)
    cur = gscore
    for _ in range(TOPK_GROUPS):
        m = jnp.max(cur, axis=0, keepdims=True)
        first = jnp.min(jnp.where(cur == m, g_iota, float(N_GROUPS)), axis=0, keepdims=True)
        pick = g_iota == first
        keep = jnp.where(pick, 1.0, keep)
        cur = jnp.where(pick, -jnp.inf, cur)
    keep3 = jnp.broadcast_to(keep.reshape(N_GROUPS, 1, tm), sel3.shape)
    cur = jnp.where(keep3 > 0.5, sel3, -jnp.inf).reshape(N_EXPERTS, tm)

    e_iota = lax.broadcasted_iota(jnp.int32, cur.shape, 0).astype(F32)
    gsum = jnp.zeros((1, tm), F32)
    gates, picks = [], []
    for k in range(TOP_K):
        m = jnp.max(cur, axis=0, keepdims=True)
        first = jnp.min(jnp.where(cur == m, e_iota, float(N_EXPERTS)), axis=0, keepdims=True)
        pick = e_iota == first
        gk = jnp.sum(jnp.where(pick, scores, 0.0), axis=0, keepdims=True)
        idx_ref[k:k + 1, :] = first.astype(jnp.int32)
        gates.append(gk)
        picks.append(pick)
        gsum = gsum + gk
        cur = jnp.where(pick, -jnp.inf, cur)
    for k in range(TOP_K):
        gate_ref[k:k + 1, :] = gates[k] / gsum * ROUTED_SCALE

    @pl.when(pl.program_id(0) == 0)
    def _():
        cnt_scr[...] = jnp.zeros(cnt_scr.shape, F32)

    mask = jnp.zeros(cur.shape, F32)
    for k in range(TOP_K):
        mask = jnp.where(picks[k], 1.0, mask)
    upper = (lax.broadcasted_iota(jnp.int32, (tm, tm), 0) <= lax.broadcasted_iota(jnp.int32, (tm, tm), 1))
    cum = jnp.dot(mask.astype(BF16), jnp.where(upper, 1.0, 0.0).astype(BF16), preferred_element_type=F32)
    before = cnt_scr[...][:, 0:1] + cum - mask
    for k in range(TOP_K):
        rank_ref[k:k + 1, :] = jnp.sum(jnp.where(picks[k], before, 0.0), axis=0, keepdims=True).astype(jnp.int32)
    cnt_scr[...] = cnt_scr[...] + jnp.sum(mask, axis=1, keepdims=True)
    cnt_ref[...] = cnt_scr[...].astype(jnp.int32)


def _router(h, router_w, router_bias):
    m, dm = h.shape
    tm = min(512, m)
    return pl.pallas_call(
        _router_kernel,
        grid=(m // tm,),
        in_specs=[pl.BlockSpec((tm, dm), lambda i: (i, 0)),
                  pl.BlockSpec((N_EXPERTS, dm), lambda i: (0, 0)),
                  pl.BlockSpec((N_EXPERTS, 1), lambda i: (0, 0))],
        out_specs=[pl.BlockSpec((TOP_K, tm), lambda i: (0, i)), pl.BlockSpec((TOP_K, tm), lambda i: (0, i)),
                   pl.BlockSpec((TOP_K, tm), lambda i: (0, i)), pl.BlockSpec((N_EXPERTS, 128), lambda i: (0, 0))],
        out_shape=[jax.ShapeDtypeStruct((TOP_K, m), jnp.int32), jax.ShapeDtypeStruct((TOP_K, m), F32),
                   jax.ShapeDtypeStruct((TOP_K, m), jnp.int32), jax.ShapeDtypeStruct((N_EXPERTS, 128), jnp.int32)],
        scratch_shapes=[pltpu.VMEM((N_EXPERTS, 128), F32)],
        compiler_params=_params(1, 32),
    )(h, router_w.T, router_bias.reshape(N_EXPERTS, 1))


def _dest_kernel(idx_ref, rank_ref, ps_ref, dest_ref):
    tm = idx_ref.shape[1]
    e_iota = lax.broadcasted_iota(jnp.int32, (N_EXPERTS, tm), 0)
    ps = ps_ref[...][:, 0:1]
    for k in range(TOP_K):
        base = jnp.sum(jnp.where(e_iota == idx_ref[k:k + 1, :], ps, 0), axis=0, keepdims=True)
        dest_ref[k:k + 1, :] = rank_ref[k:k + 1, :] + base


def _dest_rows(idx_t, rank_t, pad_start):
    m = idx_t.shape[1]
    tm = min(2048, m)
    ps = jnp.broadcast_to(pad_start.astype(F32).reshape(N_EXPERTS, 1), (N_EXPERTS, 128))
    out = pl.pallas_call(
        _dest_kernel,
        grid=(m // tm,),
        in_specs=[pl.BlockSpec((TOP_K, tm), lambda i: (0, i)), pl.BlockSpec((TOP_K, tm), lambda i: (0, i)),
                  pl.BlockSpec((N_EXPERTS, 128), lambda i: (0, 0))],
        out_specs=pl.BlockSpec((TOP_K, tm), lambda i: (0, i)),
        out_shape=jax.ShapeDtypeStruct((TOP_K, m), F32),
        compiler_params=_params(1, 32),
    )(idx_t, rank_t.astype(F32), ps)
    return out.astype(jnp.int32)


def _dispatch_kernel(dest_ref, x_ref, init_ref, xs_ref, sem):
    del init_ref
    tm = x_ref.shape[0]

    def copy(t, k):
        return pltpu.make_async_copy(x_ref.at[pl.ds(t, 1), :], xs_ref.at[pl.ds(dest_ref[0, k, t], 1), :], sem)

    def start(t, carry):
        for k in range(TOP_K):
            copy(t, k).start()
        return carry

    lax.fori_loop(0, tm, start, 0)
    n_rows = TOP_K * tm
    pltpu.make_async_copy(xs_ref.at[pl.ds(0, n_rows), :], xs_ref.at[pl.ds(n_rows, n_rows), :], sem).wait()


def _dispatch(xp, dest, rows):
    m, wd = xp.shape
    tm = min(512, m)
    dest3 = dest.reshape(TOP_K, m // tm, tm).transpose(1, 0, 2)
    return pl.pallas_call(
        _dispatch_kernel,
        grid=(m // tm,),
        in_specs=[pl.BlockSpec((1, TOP_K, tm), lambda i: (i, 0, 0), memory_space=pltpu.SMEM),
                  pl.BlockSpec((tm, wd), lambda i: (i, 0)),
                  pl.BlockSpec(memory_space=pl.ANY)],
        out_specs=pl.BlockSpec(memory_space=pl.ANY),
        out_shape=jax.ShapeDtypeStruct((rows, wd), xp.dtype),
        scratch_shapes=[pltpu.SemaphoreType.DMA(())],
        input_output_aliases={2: 0},
        compiler_params=_params(1, 32),
    )(dest3, xp, jnp.zeros((rows, wd), xp.dtype))


def _expert_kernel(be_ref, slot_ref, next_ref, nu_ref, x_ref, w1_ref, w3_ref, w2_ref, o_ref,
                   w1f, w3f, w2f, w1b, w3b, w2b, sems):
    i = pl.program_id(0)
    used = i < nu_ref[0]
    e = be_ref[i]
    first = jnp.logical_and(used, jnp.logical_or(i == 0, e != be_ref[jnp.maximum(i - 1, 0)]))

    def copies(expert, s):
        return (pltpu.make_async_copy(w1_ref.at[expert], w1f.at[s], sems.at[s, 0]),
                pltpu.make_async_copy(w3_ref.at[expert], w3f.at[s], sems.at[s, 1]),
                pltpu.make_async_copy(w2_ref.at[expert], w2f.at[s], sems.at[s, 2]))

    @pl.when(jnp.logical_and(used, i == 0))
    def _():
        for c in copies(e, slot_ref[0]):
            c.start()

    @pl.when(first)
    def _():
        s = slot_ref[i]
        for c in copies(e, s):
            c.wait()
        w1b[...] = w1f[s].astype(BF16)
        w3b[...] = w3f[s].astype(BF16)
        w2b[...] = w2f[s].astype(BF16)

        @pl.when(next_ref[i] >= 0)
        def _():
            for c in copies(next_ref[i], 1 - s):
                c.start()

    @pl.when(used)
    def _():
        xh, xl = _unpack_bf16_pairs(x_ref[...])
        half = xh.shape[1]
        a = (jnp.dot(xh, w1b[0:half, :], preferred_element_type=F32)
             + jnp.dot(xl, w1b[half:, :], preferred_element_type=F32))
        b = (jnp.dot(xh, w3b[0:half, :], preferred_element_type=F32)
             + jnp.dot(xl, w3b[half:, :], preferred_element_type=F32))
        hmid = (a * _sigmoid(a) * b).astype(BF16)
        o_ref[...] = _pack_bf16_pairs(jnp.dot(hmid, w2b[...], preferred_element_type=F32).astype(BF16))

    @pl.when(i >= nu_ref[0])
    def _():
        o_ref[...] = jnp.zeros(o_ref.shape, o_ref.dtype)


def _expert_blocks(x_sorted, block_e, block_slot, block_next, n_used, w1, w3, w2):
    rows, wd = x_sorted.shape
    n_blocks = rows // MOE_ROWS
    dm, de = w1.shape[1], w1.shape[2]
    grid_spec = pltpu.PrefetchScalarGridSpec(
        num_scalar_prefetch=4,
        grid=(n_blocks,),
        in_specs=[pl.BlockSpec((MOE_ROWS, wd), lambda i, *_: (i, 0)),
                  pl.BlockSpec(memory_space=pl.ANY),
                  pl.BlockSpec(memory_space=pl.ANY),
                  pl.BlockSpec(memory_space=pl.ANY)],
        out_specs=pl.BlockSpec((MOE_ROWS, wd), lambda i, *_: (i, 0)),
        scratch_shapes=[pltpu.VMEM((2, dm, de), F32), pltpu.VMEM((2, dm, de), F32), pltpu.VMEM((2, de, dm), F32),
                        pltpu.VMEM((dm, de), BF16), pltpu.VMEM((dm, de), BF16), pltpu.VMEM((de, dm), BF16),
                        pltpu.SemaphoreType.DMA((2, 3))],
    )
    return pl.pallas_call(
        _expert_kernel,
        grid_spec=grid_spec,
        out_shape=jax.ShapeDtypeStruct((rows, wd), jnp.uint32),
        compiler_params=_params(1, 56),
    )(block_e, block_slot, block_next, n_used, x_sorted, w1, w3, w2)


def _shared_kernel(x_ref, w1_ref, w3_ref, w2_ref, o_ref):
    x = x_ref[...]
    a = jnp.dot(x, w1_ref[...], preferred_element_type=F32)
    b = jnp.dot(x, w3_ref[...], preferred_element_type=F32)
    hmid = (a * _sigmoid(a) * b).astype(BF16)
    o_ref[...] = jnp.dot(hmid, w2_ref[...], preferred_element_type=F32)


def _shared_expert(xb, w1, w3, w2):
    m, dm = xb.shape
    de = w1.shape[1]
    tm = min(512, m)
    return pl.pallas_call(
        _shared_kernel,
        grid=(m // tm,),
        in_specs=[pl.BlockSpec((tm, dm), lambda i: (i, 0)),
                  pl.BlockSpec((dm, de), lambda i: (0, 0)),
                  pl.BlockSpec((dm, de), lambda i: (0, 0)),
                  pl.BlockSpec((de, dm), lambda i: (0, 0))],
        out_specs=pl.BlockSpec((tm, dm), lambda i: (i, 0)),
        out_shape=jax.ShapeDtypeStruct((m, dm), F32),
        compiler_params=_params(1, 32),
    )(xb, w1, w3, w2)


def _combine_kernel(dcur_ref, dnext_ref, h_ref, sh_ref, gate_ref, g_ref, b_ref, y_ref, o_ref, ybuf, sems):
    i = pl.program_id(0)
    n = pl.num_programs(0)
    tm = h_ref.shape[0]
    slot = lax.rem(i, 2)

    def copy(d_ref, s, t, k):
        return pltpu.make_async_copy(y_ref.at[pl.ds(d_ref[0, k, t], 1), :],
                                     ybuf.at[s, pl.ds(k * tm + t, 1), :], sems.at[s])

    def start_tile(d_ref, s):
        def body(t, carry):
            for k in range(TOP_K):
                copy(d_ref, s, t, k).start()
            return carry
        lax.fori_loop(0, tm, body, 0)

    @pl.when(i == 0)
    def _():
        start_tile(dcur_ref, 0)

    @pl.when(i + 1 < n)
    def _():
        start_tile(dnext_ref, 1 - slot)

    pltpu.make_async_copy(y_ref.at[pl.ds(0, TOP_K * tm), :], ybuf.at[slot], sems.at[slot]).wait()

    gate = gate_ref[...]
    half = ybuf.shape[2]
    yh = jnp.zeros((tm, half), F32)
    yl = jnp.zeros((tm, half), F32)
    for k in range(TOP_K):
        w = ybuf[slot, k * tm:(k + 1) * tm, :]
        gk = gate[:, k:k + 1]
        yh = yh + lax.bitcast_convert_type(w & jnp.uint32(0xFFFF0000), F32) * gk
        yl = yl + lax.bitcast_convert_type(w << 16, F32) * gk
    y = sh_ref[...] + jnp.concatenate([yh, yl], axis=1)
    o_ref[...] = _layer_norm(DN_ALPHA * h_ref[...] + y, g_ref[...], b_ref[...])


def _combine_ln(h, shared, y_sorted, dest, gate, g, b):
    m, dm = h.shape
    tm = min(128, m)
    n = m // tm
    dest3 = dest.reshape(TOP_K, n, tm).transpose(1, 0, 2)
    return pl.pallas_call(
        _combine_kernel,
        grid=(n,),
        in_specs=[pl.BlockSpec((1, TOP_K, tm), lambda i: (i, 0, 0), memory_space=pltpu.SMEM),
                  pl.BlockSpec((1, TOP_K, tm), lambda i: (jnp.minimum(i + 1, n - 1), 0, 0), memory_space=pltpu.SMEM),
                  pl.BlockSpec((tm, dm), lambda i: (i, 0)),
                  pl.BlockSpec((tm, dm), lambda i: (i, 0)),
                  pl.BlockSpec((tm, TOP_K), lambda i: (i, 0)),
                  pl.BlockSpec((1, dm), lambda i: (0, 0)),
                  pl.BlockSpec((1, dm), lambda i: (0, 0)),
                  pl.BlockSpec(memory_space=pl.ANY)],
        out_specs=pl.BlockSpec((tm, dm), lambda i: (i, 0)),
        out_shape=jax.ShapeDtypeStruct((m, dm), F32),
        scratch_shapes=[pltpu.VMEM((2, TOP_K * tm, dm // 2), jnp.uint32), pltpu.SemaphoreType.DMA((2,))],
        compiler_params=_params(1, 40),
    )(dest3, dest3, h, shared, gate, g.reshape(1, dm), b.reshape(1, dm), y_sorted)


def _moe_sublayer(h, hb, hp, router_w, router_bias, w1, w3, w2, sw1, sw3, sw2, g, b):
    n_tok, dm = h.shape
    idx_t, gate_t, rank_t, counts = _router(h, router_w, router_bias)
    counts = counts[:, 0]
    padded = (counts + MOE_ROWS - 1) // MOE_ROWS * MOE_ROWS
    pad_end = jnp.cumsum(padded)
    n_blocks = (n_tok * TOP_K + N_EXPERTS * (MOE_ROWS - 1) + MOE_ROWS - 1) // MOE_ROWS
    block_start = jnp.arange(n_blocks, dtype=jnp.int32) * MOE_ROWS
    block_e = jnp.minimum(jnp.sum((pad_end[None, :] <= block_start[:, None]).astype(jnp.int32), axis=1),
                          N_EXPERTS - 1)
    n_used = (pad_end[-1] // MOE_ROWS).astype(jnp.int32).reshape(1)
    e_ids = jnp.arange(N_EXPERTS, dtype=jnp.int32)
    has_rows = counts > 0
    ordinal = jnp.cumsum(has_rows.astype(jnp.int32)) - 1
    later = jnp.where(has_rows[None, :] & (e_ids[None, :] > e_ids[:, None]), e_ids[None, :], N_EXPERTS)
    next_e = jnp.min(later, axis=1)
    next_e = jnp.where(next_e < N_EXPERTS, next_e, -1).astype(jnp.int32)
    block_slot = (ordinal[block_e] % 2).astype(jnp.int32)
    block_next = next_e[block_e]
    dest = _dest_rows(idx_t, rank_t, pad_end - padded)
    x_sorted = _dispatch(hp, dest, n_blocks * MOE_ROWS)
    y_sorted = _expert_blocks(x_sorted, block_e, block_slot, block_next, n_used, w1, w3, w2)
    shared = _shared_expert(hb, sw1.astype(BF16), sw3.astype(BF16), sw2.astype(BF16))
    return _combine_ln(h, shared, y_sorted, dest, gate_t.T, g, b)


def kernel(x, mem, positions, rel_bias_table, w_in, diff_lambda, diff_subln, idx_k_g, idx_k_b, kv_norm_g, w_uk, w_uv, w_proj_diff, w_proj_dsa, w_mix_out, ln1_g, ln1_b, xattn_wq, xattn_wkv, xattn_wo, ln2_g, ln2_b, router_w, router_bias, exp_w1, exp_w3, exp_w2, sh_w1, sh_w3, sh_w2, ln3_g, ln3_b):
    del positions
    batch, seq, dm = x.shape
    n_mem = mem.shape[1]
    n_tok = batch * seq
    lambda_init = 0.8 - 0.6 * math.exp(-0.3 * 0)
    blk = min(ATT_BLOCK, seq)

    dq = DIFF_HEADS * DIFF_V_DIM
    c_qs = 3 * dq
    c_ckv = c_qs + DSA_HEADS * DSA_HEAD_DIM
    c_qi = c_ckv + DSA_KV_RANK
    c_ki = c_qi + IDX_HEADS * IDX_HEAD_DIM
    c_gd = c_ki + IDX_HEAD_DIM + IDX_HEADS
    c_gs = c_gd + dm

    xf = x.reshape(n_tok, dm)
    xb = xf.astype(BF16)
    w_in0 = w_in[0]
    qkv = _matmul(xb, w_in0, 0, c_ckv, BF16, 1024, 1024)
    ckv = _matmul(xb, w_in0, c_ckv, DSA_KV_RANK, F32, 1024, 512)
    qi = _matmul(xb, w_in0, c_qi, c_ki - c_qi, BF16, 1024, 512)
    w_tail = jnp.pad(w_in0[:, c_ki:c_gd], ((0, 0), (0, 128 - (c_gd - c_ki))))
    kiw = _matmul(xb, w_tail, 0, 128, F32, 1024, 128)

    dsa_tiles = _rel_bias_tiles(rel_bias_table[:, DIFF_HEADS:], blk)
    diff_tiles = _rel_bias_tiles(rel_bias_table[:, :DIFF_HEADS], min(DIFF_BLOCK, seq))
    o_d = _diff_attention(qkv, diff_tiles, diff_lambda[0], diff_subln[0], batch, seq, lambda_init)

    k_sel = min(DSA_TOPK_MAX, seq // 4)
    sel = _indexer_mask(qi, kiw, idx_k_g[0], idx_k_b[0], batch, seq, k_sel, qi_col_block=0)
    o_s = _dsa_attention(qkv, c_qs // (DSA_HEADS * DSA_HEAD_DIM), ckv, 0, kv_norm_g[0], sel,
                         w_uk[0].astype(BF16), w_uv[0].astype(BF16), dsa_tiles, batch, seq)

    merged = _gated_merge(xb, o_d, o_s, w_in0[:, c_gd:c_gs].astype(BF16), w_in0[:, c_gs:c_gs + dm].astype(BF16),
                          w_proj_diff[0].astype(BF16), w_proj_dsa[0].astype(BF16))
    h1, h1b = _matmul_res_ln(merged, w_mix_out[0].astype(BF16), xf, ln1_g[0], ln1_b[0])

    memb = mem.reshape(batch * n_mem, dm).astype(BF16)
    kv = _matmul(memb, xattn_wkv[0], 0, 2 * XATTN_HEADS * XATTN_HEAD_DIM, BF16, 1024, 512)
    h2, h2b, h2p = _cross_attention(h1b, h1, xattn_wq[0].astype(BF16), kv, xattn_wo[0].astype(BF16),
                                    ln2_g[0], ln2_b[0], batch, seq, n_mem)

    out = _moe_sublayer(h2, h2b, h2p, router_w[0], router_bias[0], exp_w1[0], exp_w3[0], exp_w2[0],
                        sh_w1[0], sh_w3[0], sh_w2[0], ln3_g[0], ln3_b[0])
    return out.reshape(batch, seq, dm)
```

```python
import functools
import math

import jax
import jax.numpy as jnp
from jax import lax
from jax.experimental import pallas as pl
from jax.experimental.pallas import tpu as pltpu

F32 = jnp.float32
BF16 = jnp.bfloat16
NEG = -1e30
INT_MIN = -(2 ** 31)
MIB = 1024 * 1024

LN_EPS = 1e-5
DIFF_HEADS = 8
DIFF_HEAD_DIM = 128
DIFF_V_DIM = 2 * DIFF_HEAD_DIM
DSA_HEADS = 8
DSA_HEAD_DIM = 128
DSA_KV_RANK = 512
IDX_HEADS = 16
IDX_HEAD_DIM = 64
DSA_TOPK_MAX = 256
REL_BUCKETS = 32
REL_MAX_DIST = 128
XATTN_HEADS = 4
XATTN_HEAD_DIM = 128
N_EXPERTS = 64
EXPERT_DIM = 512
TOP_K = 8
N_GROUPS = 8
TOPK_GROUPS = 4
ROUTED_SCALE = 2.5
DEPTH = 1
DN_ALPHA = (2 * DEPTH) ** 0.25

ATT_BLOCK = 256
DIFF_BLOCK = 512
MOE_ROWS = 256
NT_DIMS = (((1,), (1,)), ((), ()))


def _params(n_grid, vmem_mib):
    return pltpu.CompilerParams(dimension_semantics=("arbitrary",) * n_grid,
                                vmem_limit_bytes=vmem_mib * MIB)


def _sigmoid(x):
    return 1.0 / (1.0 + jnp.exp(-x))


def _layer_norm(z, g, b):
    mu = jnp.mean(z, axis=1, keepdims=True)
    zc = z - mu
    var = jnp.mean(zc * zc, axis=1, keepdims=True)
    return zc * lax.rsqrt(var + LN_EPS) * g + b


def _mm_kernel(a_ref, b_ref, o_ref, bq_ref):
    @pl.when(pl.program_id(1) == 0)
    def _():
        bq_ref[...] = b_ref[...].astype(BF16)

    o_ref[...] = jnp.dot(a_ref[...], bq_ref[...], preferred_element_type=F32).astype(o_ref.dtype)


def _matmul(a, b, col_start, n_cols, out_dtype, tm, tn):
    m, k = a.shape
    tm = min(tm, m)
    assert col_start % tn == 0 and n_cols % tn == 0 and m % tm == 0
    off = col_start // tn
    return pl.pallas_call(
        _mm_kernel,
        grid=(n_cols // tn, m // tm),
        in_specs=[pl.BlockSpec((tm, k), lambda j, i: (i, 0)),
                  pl.BlockSpec((k, tn), lambda j, i: (0, j + off))],
        out_specs=pl.BlockSpec((tm, tn), lambda j, i: (i, j)),
        out_shape=jax.ShapeDtypeStruct((m, n_cols), out_dtype),
        scratch_shapes=[pltpu.VMEM((k, tn), BF16)],
        compiler_params=_params(2, 48),
    )(a, b)


def _rel_bucket(n):
    n = jnp.maximum(n, 0)
    max_exact = REL_BUCKETS // 2
    nf = jnp.maximum(n, 1).astype(F32)
    large = max_exact + (jnp.log(nf / max_exact) / math.log(REL_MAX_DIST / max_exact)
                         * (REL_BUCKETS - max_exact)).astype(jnp.int32)
    large = jnp.minimum(large, REL_BUCKETS - 1)
    return jnp.where(n < max_exact, n, large)


def _rel_bias_tiles(table, blk):
    assert blk >= REL_MAX_DIST
    r = jnp.arange(blk, dtype=jnp.int32)[:, None]
    c = jnp.arange(blk, dtype=jnp.int32)[None, :]
    tiles = []
    for back in range(3):
        n = back * blk + r - c
        if back == 2:
            tiles.append(jnp.broadcast_to(table[REL_BUCKETS - 1][:, None, None], (table.shape[1], blk, blk)))
            continue
        bucket = _rel_bucket(n)
        t = jnp.zeros((table.shape[1], blk, blk), F32)
        for bkt in range(REL_BUCKETS):
            t = jnp.where(bucket == bkt, table[bkt][:, None, None], t)
        tiles.append(jnp.where(n >= 0, t, NEG))
    return jnp.stack(tiles, axis=1).astype(F32)


def _lane_repeat(x, n, axis=1):
    assert axis == 1
    return x if n == 1 else jnp.concatenate([x] * n, axis=1)


def _online_softmax_step(s, v, m_ref, l_ref, acc_ref):
    tk = s.shape[1]
    e = v.shape[1]
    m_prev = m_ref[...]
    m_next = jnp.maximum(m_prev, jnp.max(s, axis=1, keepdims=True))
    p = jnp.exp(s - _lane_repeat(m_next, tk // 128, axis=1))
    alpha = jnp.exp(m_prev - m_next)
    l_ref[...] = alpha * l_ref[...] + jnp.sum(p, axis=1, keepdims=True)
    m_ref[...] = m_next
    acc_ref[...] = (acc_ref[...] * _lane_repeat(alpha, e // 128, axis=1)
                    + jnp.dot(p.astype(BF16), v, preferred_element_type=F32))


def _diff_attn_kernel(q_ref, k_ref, v_ref, bias_ref, lam_ref, g_ref, o_ref, m_scr, l_scr, acc_scr,
                      *, blk, lambda_init):
    i = pl.program_id(2)
    d = DIFF_HEAD_DIM
    scale = d ** -0.5
    m_scr[...] = jnp.full(m_scr.shape, NEG, F32)
    l_scr[...] = jnp.zeros(l_scr.shape, F32)
    acc_scr[...] = jnp.zeros(acc_scr.shape, F32)
    q = q_ref[...]

    def body(j, carry):
        start = pl.multiple_of(j * blk, blk)
        kk = k_ref[pl.ds(start, blk), :]
        vv = v_ref[pl.ds(start, blk), :]
        bt = bias_ref[jnp.minimum(i - j, 2)]
        for c in range(2):
            s = lax.dot_general(q[:, c * d:(c + 1) * d], kk[:, c * d:(c + 1) * d], NT_DIMS,
                                preferred_element_type=F32)
            _online_softmax_step(s * scale + bt, vv, m_scr.at[c], l_scr.at[c], acc_scr.at[c])
        return carry

    lax.fori_loop(0, i + 1, body, 0)

    lp = lam_ref[...]
    lam = (jnp.exp(jnp.sum(lp[0:1] * lp[1:2], axis=1, keepdims=True))
           - jnp.exp(jnp.sum(lp[2:3] * lp[3:4], axis=1, keepdims=True)) + lambda_init)
    o0 = acc_scr[0] * _lane_repeat(1.0 / l_scr[0], 2, axis=1)
    o1 = acc_scr[1] * _lane_repeat(1.0 / l_scr[1], 2, axis=1)
    o = o0 - lam * o1
    ms = jnp.mean(o * o, axis=1, keepdims=True)
    o_ref[...] = (o * lax.rsqrt(ms + LN_EPS) * g_ref[...] * (1.0 - lambda_init)).astype(o_ref.dtype)


def _diff_attention(qkv, bias_tiles, diff_lambda, subln, batch, seq, lambda_init):
    blk = bias_tiles.shape[2]
    nq = seq // blk
    h2d = DIFF_V_DIM
    kern = functools.partial(_diff_attn_kernel, blk=blk, lambda_init=lambda_init)
    return pl.pallas_call(
        kern,
        grid=(batch, DIFF_HEADS, nq),
        in_specs=[pl.BlockSpec((blk, h2d), lambda b, h, i: (b * nq + i, h)),
                  pl.BlockSpec((seq, h2d), lambda b, h, i: (b, DIFF_HEADS + h)),
                  pl.BlockSpec((seq, h2d), lambda b, h, i: (b, 2 * DIFF_HEADS + h)),
                  pl.BlockSpec((None, 3, blk, blk), lambda b, h, i: (h, 0, 0, 0)),
                  pl.BlockSpec((4, DIFF_HEAD_DIM), lambda b, h, i: (0, 0)),
                  pl.BlockSpec((1, h2d), lambda b, h, i: (0, 0))],
        out_specs=pl.BlockSpec((blk, h2d), lambda b, h, i: (b * nq + i, h)),
        out_shape=jax.ShapeDtypeStruct((batch * seq, DIFF_HEADS * h2d), BF16),
        scratch_shapes=[pltpu.VMEM((2, blk, 128), F32), pltpu.VMEM((2, blk, 128), F32),
                        pltpu.VMEM((2, blk, h2d), F32)],
        compiler_params=_params(3, 48),
    )(qkv, qkv, qkv, bias_tiles, diff_lambda, subln.reshape(1, h2d))


def _indexer_kernel(qi_ref, kfull_ref, wblk_ref, g_ref, b_ref, o_ref, ki_scr, key_scr, wb_scr, thr_scr,
                    *, blk, seq, k_sel):
    i = pl.program_id(1)

    @pl.when(i == 0)
    def _():
        kr = kfull_ref[:, 0:IDX_HEAD_DIM]
        ki_scr[...] = _layer_norm(kr, g_ref[...], b_ref[...]).astype(BF16)

    w = wblk_ref[:, IDX_HEAD_DIM:IDX_HEAD_DIM + IDX_HEADS] * (IDX_HEADS ** -0.5 * IDX_HEAD_DIM ** -0.5)
    for h in range(IDX_HEADS):
        wb_scr[h] = jnp.broadcast_to(w[:, h:h + 1], (blk, blk))
    qi = qi_ref[...].astype(BF16)
    row = i * blk + lax.broadcasted_iota(jnp.int32, (blk, blk), 0)
    col = lax.broadcasted_iota(jnp.int32, (blk, blk), 1)

    def score_tile(j, carry):
        start = pl.multiple_of(j * blk, blk)
        kc = ki_scr[pl.ds(start, blk), :]
        score = jnp.zeros((blk, blk), F32)
        for h in range(IDX_HEADS):
            lg = lax.dot_general(qi[:, h * IDX_HEAD_DIM:(h + 1) * IDX_HEAD_DIM], kc, NT_DIMS,
                                 preferred_element_type=F32)
            score = score + jnp.maximum(lg, 0.0) * wb_scr[h]
        bits = lax.bitcast_convert_type(score, jnp.int32)
        key = bits ^ ((bits >> 31) & jnp.int32(0x7FFFFFFF))
        key_scr[j] = jnp.where(start + col <= row, key, INT_MIN)
        return carry

    lax.fori_loop(0, i + 1, score_tile, 0)

    def make_search(n_tiles):
        def search(it, thr):
            trial = thr + lax.shift_left(jnp.int32(1), 31 - it)
            c = jnp.zeros((blk, 128), F32)
            for j in range(n_tiles):
                ge = jnp.where(key_scr[j] >= trial, 1.0, 0.0)
                for part in range(blk // 128):
                    c = c + ge[:, part * 128:(part + 1) * 128]
            return jnp.where(jnp.sum(c, axis=1, keepdims=True) >= k_sel, trial, thr)
        return search

    thr_scr[...] = jnp.full(thr_scr.shape, INT_MIN, jnp.int32)
    for jj in range(seq // blk):
        if (jj + 1) * blk > k_sel:
            @pl.when(i == jj)
            def _():
                t = lax.fori_loop(0, 32, make_search(jj + 1), jnp.full((blk, 1), INT_MIN, jnp.int32))
                thr_scr[...] = jnp.broadcast_to(t, thr_scr.shape)

    thr = thr_scr[...][:, 0:1]
    for jj in range(seq // blk):
        @pl.when(jj <= i)
        def _():
            keep = (jj * blk + col <= row) & (key_scr[jj] >= thr)
            o_ref[jj] = jnp.where(keep, 0.0, NEG).astype(o_ref.dtype)

        @pl.when(jj > i)
        def _():
            o_ref[jj] = jnp.full((blk, blk), NEG, o_ref.dtype)


def _indexer_mask(qi, kiw, idx_k_g, idx_k_b, batch, seq, k_sel, qi_col_block):
    blk = min(ATT_BLOCK, seq)
    nq = seq // blk
    qw = IDX_HEADS * IDX_HEAD_DIM
    kern = functools.partial(_indexer_kernel, blk=blk, seq=seq, k_sel=k_sel)
    return pl.pallas_call(
        kern,
        grid=(batch, nq),
        in_specs=[pl.BlockSpec((blk, qw), lambda b, i: (b * nq + i, qi_col_block)),
                  pl.BlockSpec((seq, 128), lambda b, i: (b, 0)),
                  pl.BlockSpec((blk, 128), lambda b, i: (b * nq + i, 0)),
                  pl.BlockSpec((1, IDX_HEAD_DIM), lambda b, i: (0, 0)),
                  pl.BlockSpec((1, IDX_HEAD_DIM), lambda b, i: (0, 0))],
        out_specs=pl.BlockSpec((None, nq, blk, blk), lambda b, i: (b, 0, i, 0)),
        out_shape=jax.ShapeDtypeStruct((batch, nq, seq, blk), BF16),
        scratch_shapes=[pltpu.VMEM((seq, IDX_HEAD_DIM), BF16), pltpu.VMEM((nq, blk, blk), jnp.int32),
                        pltpu.VMEM((IDX_HEADS, blk, blk), F32), pltpu.VMEM((blk, 128), jnp.int32)],
        compiler_params=_params(2, 48),
    )(qi, kiw, kiw, idx_k_g.reshape(1, -1), idx_k_b.reshape(1, -1))


def _dsa_attn_kernel(qs_ref, ckv_ref, g_ref, sel_ref, wuk_ref, wuv_ref, bias_ref, o_ref,
                     c_scr, ql_scr, sb_scr, m_scr, l_scr, acc_scr, *, blk):
    i = pl.program_id(1)
    d = DSA_HEAD_DIM
    scale = d ** -0.5

    @pl.when(i == 0)
    def _():
        x = ckv_ref[...]
        c = x * lax.rsqrt(jnp.mean(x * x, axis=1, keepdims=True) + LN_EPS) * g_ref[...]
        c_scr[...] = c.astype(BF16)

    qs = qs_ref[...]
    for h in range(DSA_HEADS):
        ql = jnp.dot(qs[:, h * d:(h + 1) * d], wuk_ref[h], preferred_element_type=F32)
        ql_scr[h * blk:(h + 1) * blk, :] = (ql * scale).astype(BF16)

    m_scr[...] = jnp.full(m_scr.shape, NEG, F32)
    l_scr[...] = jnp.zeros(l_scr.shape, F32)
    acc_scr[...] = jnp.zeros(acc_scr.shape, F32)

    def body(j, carry):
        start = pl.multiple_of(j * blk, blk)
        cc = c_scr[pl.ds(start, blk), :]
        s = lax.dot_general(ql_scr[...], cc, NT_DIMS, preferred_element_type=F32)
        selm = sel_ref[j].astype(F32)
        back = jnp.minimum(i - j, 2)
        for h in range(DSA_HEADS):
            sb_scr[h * blk:(h + 1) * blk, :] = bias_ref[h, back] + selm
        _online_softmax_step(s + sb_scr[...], cc, m_scr, l_scr, acc_scr)
        return carry

    lax.fori_loop(0, i + 1, body, 0)

    o_lat = (acc_scr[...] * _lane_repeat(1.0 / l_scr[...], DSA_KV_RANK // 128, axis=1)).astype(BF16)
    for h in range(DSA_HEADS):
        o_ref[:, h * d:(h + 1) * d] = jnp.dot(o_lat[h * blk:(h + 1) * blk], wuv_ref[h],
                                              preferred_element_type=F32).astype(o_ref.dtype)


def _dsa_attention(qkv, qs_col_block, ckv, ckv_col_block, kv_norm_g, sel, w_uk, w_uv, bias_tiles, batch, seq):
    blk = min(ATT_BLOCK, seq)
    nq = seq // blk
    hd = DSA_HEADS * DSA_HEAD_DIM
    r = DSA_KV_RANK
    rows = DSA_HEADS * blk
    kern = functools.partial(_dsa_attn_kernel, blk=blk)
    return pl.pallas_call(
        kern,
        grid=(batch, nq),
        in_specs=[pl.BlockSpec((blk, hd), lambda b, i: (b * nq + i, qs_col_block)),
                  pl.BlockSpec((seq, r), lambda b, i: (b, ckv_col_block)),
                  pl.BlockSpec((1, r), lambda b, i: (0, 0)),
                  pl.BlockSpec((None, nq, blk, blk), lambda b, i: (b, 0, i, 0)),
                  pl.BlockSpec((DSA_HEADS, DSA_HEAD_DIM, r), lambda b, i: (0, 0, 0)),
                  pl.BlockSpec((DSA_HEADS, r, DSA_HEAD_DIM), lambda b, i: (0, 0, 0)),
                  pl.BlockSpec((DSA_HEADS, 3, blk, blk), lambda b, i: (0, 0, 0, 0))],
        out_specs=pl.BlockSpec((blk, hd), lambda b, i: (b * nq + i, 0)),
        out_shape=jax.ShapeDtypeStruct((batch * seq, hd), BF16),
        scratch_shapes=[pltpu.VMEM((seq, r), BF16), pltpu.VMEM((rows, r), BF16), pltpu.VMEM((rows, blk), F32),
                        pltpu.VMEM((rows, 128), F32), pltpu.VMEM((rows, 128), F32), pltpu.VMEM((rows, r), F32)],
        compiler_params=_params(2, 56),
    )(qkv, ckv, kv_norm_g.reshape(1, r), sel, w_uk, w_uv, bias_tiles)


def _merge_kernel(x_ref, od_ref, os_ref, wgd_ref, wgs_ref, wpd_ref, wps_ref, o_ref):
    x = x_ref[...]
    gd = jnp.dot(x, wgd_ref[...], preferred_element_type=F32)
    gs = jnp.dot(x, wgs_ref[...], preferred_element_type=F32)
    pd = jnp.dot(od_ref[...], wpd_ref[...], preferred_element_type=F32)
    ps = jnp.dot(os_ref[...], wps_ref[...], preferred_element_type=F32)
    o_ref[...] = (_sigmoid(gd) * pd + _sigmoid(gs) * ps).astype(o_ref.dtype)


def _gated_merge(xb, o_d, o_s, wgd, wgs, wpd, wps):
    m, dm = xb.shape
    tm = min(512, m)
    tn = 512
    kd, ks = o_d.shape[1], o_s.shape[1]
    return pl.pallas_call(
        _merge_kernel,
        grid=(dm // tn, m // tm),
        in_specs=[pl.BlockSpec((tm, dm), lambda j, i: (i, 0)),
                  pl.BlockSpec((tm, kd), lambda j, i: (i, 0)),
                  pl.BlockSpec((tm, ks), lambda j, i: (i, 0)),
                  pl.BlockSpec((dm, tn), lambda j, i: (0, j)),
                  pl.BlockSpec((dm, tn), lambda j, i: (0, j)),
                  pl.BlockSpec((kd, tn), lambda j, i: (0, j)),
                  pl.BlockSpec((ks, tn), lambda j, i: (0, j))],
        out_specs=pl.BlockSpec((tm, tn), lambda j, i: (i, j)),
        out_shape=jax.ShapeDtypeStruct((m, dm), BF16),
        compiler_params=_params(2, 40),
    )(xb, o_d, o_s, wgd, wgs, wpd, wps)


def _mm_res_ln_kernel(a_ref, w_ref, res_ref, g_ref, b_ref, o_ref, ob_ref):
    y = jnp.dot(a_ref[...], w_ref[...], preferred_element_type=F32)
    zn = _layer_norm(DN_ALPHA * res_ref[...] + y, g_ref[...], b_ref[...])
    o_ref[...] = zn
    ob_ref[...] = zn.astype(BF16)


def _matmul_res_ln(a, w, res, g, b):
    m, k = a.shape
    n = w.shape[1]
    tm = min(256, m)
    return pl.pallas_call(
        _mm_res_ln_kernel,
        grid=(m // tm,),
        in_specs=[pl.BlockSpec((tm, k), lambda i: (i, 0)),
                  pl.BlockSpec((k, n), lambda i: (0, 0)),
                  pl.BlockSpec((tm, n), lambda i: (i, 0)),
                  pl.BlockSpec((1, n), lambda i: (0, 0)),
                  pl.BlockSpec((1, n), lambda i: (0, 0))],
        out_specs=[pl.BlockSpec((tm, n), lambda i: (i, 0)), pl.BlockSpec((tm, n), lambda i: (i, 0))],
        out_shape=[jax.ShapeDtypeStruct((m, n), F32), jax.ShapeDtypeStruct((m, n), BF16)],
        compiler_params=_params(1, 40),
    )(a, w, res, g.reshape(1, n), b.reshape(1, n))


def _pack_bf16_pairs(zb):
    half = zb.shape[1] // 2
    bits = lax.bitcast_convert_type(zb.astype(F32), jnp.uint32)
    return (bits[:, :half] & jnp.uint32(0xFFFF0000)) | (bits[:, half:] >> 16)


def _unpack_bf16_pairs(w):
    hi = lax.bitcast_convert_type(w & jnp.uint32(0xFFFF0000), F32).astype(BF16)
    lo = lax.bitcast_convert_type(w << 16, F32).astype(BF16)
    return hi, lo


def _xattn_kernel(hb_ref, h_ref, wq_ref, kv_ref, wo_ref, g_ref, b_ref, o_ref, ob_ref, op_ref):
    d = XATTN_HEAD_DIM
    hd = XATTN_HEADS * d
    scale = d ** -0.5
    qb = jnp.dot(hb_ref[...], wq_ref[...], preferred_element_type=F32).astype(BF16)
    kv = kv_ref[...]
    outs = []
    for h in range(XATTN_HEADS):
        s = lax.dot_general(qb[:, h * d:(h + 1) * d], kv[:, h * d:(h + 1) * d], NT_DIMS,
                            preferred_element_type=F32) * scale
        p = jnp.exp(s - jnp.max(s, axis=1, keepdims=True))
        l = jnp.sum(p, axis=1, keepdims=True)
        oh = jnp.dot(p.astype(BF16), kv[:, hd + h * d:hd + (h + 1) * d], preferred_element_type=F32)
        outs.append((oh / l).astype(BF16))
    o = jnp.concatenate(outs, axis=1)
    y = jnp.dot(o, wo_ref[...], preferred_element_type=F32)
    zn = _layer_norm(DN_ALPHA * h_ref[...] + y, g_ref[...], b_ref[...])
    zb = zn.astype(BF16)
    o_ref[...] = zn
    ob_ref[...] = zb
    op_ref[...] = _pack_bf16_pairs(zb)


def _cross_attention(hb, h, wq, kv, wo, g, b, batch, seq, n_mem):
    m, dm = h.shape
    tm = min(256, seq)
    nq = seq // tm
    hd = XATTN_HEADS * XATTN_HEAD_DIM
    return pl.pallas_call(
        _xattn_kernel,
        grid=(batch, nq),
        in_specs=[pl.BlockSpec((tm, dm), lambda bb, i: (bb * nq + i, 0)),
                  pl.BlockSpec((tm, dm), lambda bb, i: (bb * nq + i, 0)),
                  pl.BlockSpec((dm, hd), lambda bb, i: (0, 0)),
                  pl.BlockSpec((n_mem, 2 * hd), lambda bb, i: (bb, 0)),
                  pl.BlockSpec((hd, dm), lambda bb, i: (0, 0)),
                  pl.BlockSpec((1, dm), lambda bb, i: (0, 0)),
                  pl.BlockSpec((1, dm), lambda bb, i: (0, 0))],
        out_specs=[pl.BlockSpec((tm, dm), lambda bb, i: (bb * nq + i, 0)),
                   pl.BlockSpec((tm, dm), lambda bb, i: (bb * nq + i, 0)),
                   pl.BlockSpec((tm, dm // 2), lambda bb, i: (bb * nq + i, 0))],
        out_shape=[jax.ShapeDtypeStruct((m, dm), F32), jax.ShapeDtypeStruct((m, dm), BF16),
                   jax.ShapeDtypeStruct((m, dm // 2), jnp.uint32)],
        compiler_params=_params(2, 40),
    )(hb, h, wq, kv, wo, g.reshape(1, dm), b.reshape(1, dm))


def _split_bf16(x):
    hi = x.astype(BF16)
    return hi, (x - hi.astype(F32)).astype(BF16)


def _router_kernel(h_ref, rwt_ref, rb_ref, idx_ref, gate_ref, rank_ref, cnt_ref, cnt_scr):
    tm = h_ref.shape[0]
    per = N_EXPERTS // N_GROUPS
    h_hi, h_lo = _split_bf16(h_ref[...])
    w_hi, w_lo = _split_bf16(rwt_ref[...])
    logits = (lax.dot_general(w_hi, h_hi, NT_DIMS, preferred_element_type=F32)
              + lax.dot_general(w_hi, h_lo, NT_DIMS, preferred_element_type=F32)
              + lax.dot_general(w_lo, h_hi, NT_DIMS, preferred_element_type=F32))
    scores = _sigmoid(logits)
    sel = scores + rb_ref[...]

    sel3 = sel.reshape(N_GROUPS, per, tm)
    r_iota = lax.broadcasted_iota(jnp.int32, sel3.shape, 1).astype(F32)
    m1 = jnp.max(sel3, axis=1, keepdims=True)
    first = jnp.min(jnp.where(sel3 == m1, r_iota, float(per)), axis=1, keepdims=True)
    m2 = jnp.max(jnp.where(r_iota == first, -jnp.inf, sel3), axis=1, keepdims=True)
    gscore = (m1 + m2).reshape(N_GROUPS, tm)

    g_iota = lax.broadcasted_iota(jnp.int32, gscore.shape, 0).astype(F32)
    keep = jnp.zeros(gscore.shape, F32)
    cur = gscore
    for _ in range(TOPK_GROUPS):
        m = jnp.max(cur, axis=0, keepdims=True)
        first = jnp.min(jnp.where(cur == m, g_iota, float(N_GROUPS)), axis=0, keepdims=True)
        pick = g_iota == first
        keep = jnp.where(pick, 1.0, keep)
        cur = jnp.where(pick, -jnp.inf, cur)
    keep3 = jnp.broadcast_to(keep.reshape(N_GROUPS, 1, tm), sel3.shape)
    cur = jnp.where(keep3 > 0.5, sel3, -jnp.inf).reshape(N_EXPERTS, tm)

    e_iota = lax.broadcasted_iota(jnp.int32, cur.shape, 0).astype(F32)
    gsum = jnp.zeros((1, tm), F32)
    gates, picks = [], []
    for k in range(TOP_K):
        m = jnp.max(cur, axis=0, keepdims=True)
        first = jnp.min(jnp.where(cur == m, e_iota, float(N_EXPERTS)), axis=0, keepdims=True)
        pick = e_iota == first
        gk = jnp.sum(jnp.where(pick, scores, 0.0), axis=0, keepdims=True)
        idx_ref[k:k + 1, :] = first.astype(jnp.int32)
        gates.append(gk)
        picks.append(pick)
        gsum = gsum + gk
        cur = jnp.where(pick, -jnp.inf, cur)
    for k in range(TOP_K):
        gate_ref[k:k + 1, :] = gates[k] / gsum * ROUTED_SCALE

    @pl.when(pl.program_id(0) == 0)
    def _():
        cnt_scr[...] = jnp.zeros(cnt_scr.shape, F32)

    mask = jnp.zeros(cur.shape, F32)
    for k in range(TOP_K):
        mask = jnp.where(picks[k], 1.0, mask)
    upper = (lax.broadcasted_iota(jnp.int32, (tm, tm), 0) <= lax.broadcasted_iota(jnp.int32, (tm, tm), 1))
    cum = jnp.dot(mask.astype(BF16), jnp.where(upper, 1.0, 0.0).astype(BF16), preferred_element_type=F32)
    before = cnt_scr[...][:, 0:1] + cum - mask
    for k in range(TOP_K):
        rank_ref[k:k + 1, :] = jnp.sum(jnp.where(picks[k], before, 0.0), axis=0, keepdims=True).astype(jnp.int32)
    cnt_scr[...] = cnt_scr[...] + jnp.sum(mask, axis=1, keepdims=True)
    cnt_ref[...] = cnt_scr[...].astype(jnp.int32)


def _router(h, router_w, router_bias):
    m, dm = h.shape
    tm = min(512, m)
    return pl.pallas_call(
        _router_kernel,
        grid=(m // tm,),
        in_specs=[pl.BlockSpec((tm, dm), lambda i: (i, 0)),
                  pl.BlockSpec((N_EXPERTS, dm), lambda i: (0, 0)),
                  pl.BlockSpec((N_EXPERTS, 1), lambda i: (0, 0))],
        out_specs=[pl.BlockSpec((TOP_K, tm), lambda i: (0, i)), pl.BlockSpec((TOP_K, tm), lambda i: (0, i)),
                   pl.BlockSpec((TOP_K, tm), lambda i: (0, i)), pl.BlockSpec((N_EXPERTS, 128), lambda i: (0, 0))],
        out_shape=[jax.ShapeDtypeStruct((TOP_K, m), jnp.int32), jax.ShapeDtypeStruct((TOP_K, m), F32),
                   jax.ShapeDtypeStruct((TOP_K, m), jnp.int32), jax.ShapeDtypeStruct((N_EXPERTS, 128), jnp.int32)],
        scratch_shapes=[pltpu.VMEM((N_EXPERTS, 128), F32)],
        compiler_params=_params(1, 32),
    )(h, router_w.T, router_bias.reshape(N_EXPERTS, 1))


def _dest_kernel(idx_ref, rank_ref, ps_ref, dest_ref):
    tm = idx_ref.shape[1]
    e_iota = lax.broadcasted_iota(jnp.int32, (N_EXPERTS, tm), 0)
    ps = ps_ref[...][:, 0:1]
    for k in range(TOP_K):
        base = jnp.sum(jnp.where(e_iota == idx_ref[k:k + 1, :], ps, 0), axis=0, keepdims=True)
        dest_ref[k:k + 1, :] = rank_ref[k:k + 1, :] + base


def _dest_rows(idx_t, rank_t, pad_start):
    m = idx_t.shape[1]
    tm = min(2048, m)
    ps = jnp.broadcast_to(pad_start.astype(F32).reshape(N_EXPERTS, 1), (N_EXPERTS, 128))
    out = pl.pallas_call(
        _dest_kernel,
        grid=(m // tm,),
        in_specs=[pl.BlockSpec((TOP_K, tm), lambda i: (0, i)), pl.BlockSpec((TOP_K, tm), lambda i: (0, i)),
                  pl.BlockSpec((N_EXPERTS, 128), lambda i: (0, 0))],
        out_specs=pl.BlockSpec((TOP_K, tm), lambda i: (0, i)),
        out_shape=jax.ShapeDtypeStruct((TOP_K, m), F32),
        compiler_params=_params(1, 32),
    )(idx_t, rank_t.astype(F32), ps)
    return out.astype(jnp.int32)


def _dispatch_kernel(dest_ref, x_ref, init_ref, xs_ref, sem):
    del init_ref
    tm = x_ref.shape[0]

    def copy(t, k):
        return pltpu.make_async_copy(x_ref.at[pl.ds(t, 1), :], xs_ref.at[pl.ds(dest_ref[0, k, t], 1), :], sem)

    def start(t, carry):
        for k in range(TOP_K):
            copy(t, k).start()
        return carry

    lax.fori_loop(0, tm, start, 0)
    n_rows = TOP_K * tm
    pltpu.make_async_copy(xs_ref.at[pl.ds(0, n_rows), :], xs_ref.at[pl.ds(n_rows, n_rows), :], sem).wait()


def _dispatch(xp, dest, rows):
    m, wd = xp.shape
    tm = min(512, m)
    dest3 = dest.reshape(TOP_K, m // tm, tm).transpose(1, 0, 2)
    return pl.pallas_call(
        _dispatch_kernel,
        grid=(m // tm,),
        in_specs=[pl.BlockSpec((1, TOP_K, tm), lambda i: (i, 0, 0), memory_space=pltpu.SMEM),
                  pl.BlockSpec((tm, wd), lambda i: (i, 0)),
                  pl.BlockSpec(memory_space=pl.ANY)],
        out_specs=pl.BlockSpec(memory_space=pl.ANY),
        out_shape=jax.ShapeDtypeStruct((rows, wd), xp.dtype),
        scratch_shapes=[pltpu.SemaphoreType.DMA(())],
        input_output_aliases={2: 0},
        compiler_params=_params(1, 32),
    )(dest3, xp, jnp.zeros((rows, wd), xp.dtype))


def _expert_kernel(be_ref, slot_ref, next_ref, nu_ref, x_ref, w1_ref, w3_ref, w2_ref, o_ref,
                   w1f, w3f, w2f, w1b, w3b, w2b, sems):
    i = pl.program_id(0)
    used = i < nu_ref[0]
    e = be_ref[i]
    first = jnp.logical_and(used, jnp.logical_or(i == 0, e != be_ref[jnp.maximum(i - 1, 0)]))

    def copies(expert, s):
        return (pltpu.make_async_copy(w1_ref.at[expert], w1f.at[s], sems.at[s, 0]),
                pltpu.make_async_copy(w3_ref.at[expert], w3f.at[s], sems.at[s, 1]),
                pltpu.make_async_copy(w2_ref.at[expert], w2f.at[s], sems.at[s, 2]))

    @pl.when(jnp.logical_and(used, i == 0))
    def _():
        for c in copies(e, slot_ref[0]):
            c.start()

    @pl.when(first)
    def _():
        s = slot_ref[i]
        for c in copies(e, s):
            c.wait()
        w1b[...] = w1f[s].astype(BF16)
        w3b[...] = w3f[s].astype(BF16)
        w2b[...] = w2f[s].astype(BF16)

        @pl.when(next_ref[i] >= 0)
        def _():
            for c in copies(next_ref[i], 1 - s):
                c.start()

    @pl.when(used)
    def _():
        xh, xl = _unpack_bf16_pairs(x_ref[...])
        half = xh.shape[1]
        a = (jnp.dot(xh, w1b[0:half, :], preferred_element_type=F32)
             + jnp.dot(xl, w1b[half:, :], preferred_element_type=F32))
        b = (jnp.dot(xh, w3b[0:half, :], preferred_element_type=F32)
             + jnp.dot(xl, w3b[half:, :], preferred_element_type=F32))
        hmid = (a * _sigmoid(a) * b).astype(BF16)
        o_ref[...] = _pack_bf16_pairs(jnp.dot(hmid, w2b[...], preferred_element_type=F32).astype(BF16))

    @pl.when(i >= nu_ref[0])
    def _():
        o_ref[...] = jnp.zeros(o_ref.shape, o_ref.dtype)


def _expert_blocks(x_sorted, block_e, block_slot, block_next, n_used, w1, w3, w2):
    rows, wd = x_sorted.shape
    n_blocks = rows // MOE_ROWS
    dm, de = w1.shape[1], w1.shape[2]
    grid_spec = pltpu.PrefetchScalarGridSpec(
        num_scalar_prefetch=4,
        grid=(n_blocks,),
        in_specs=[pl.BlockSpec((MOE_ROWS, wd), lambda i, *_: (i, 0)),
                  pl.BlockSpec(memory_space=pl.ANY),
                  pl.BlockSpec(memory_space=pl.ANY),
                  pl.BlockSpec(memory_space=pl.ANY)],
        out_specs=pl.BlockSpec((MOE_ROWS, wd), lambda i, *_: (i, 0)),
        scratch_shapes=[pltpu.VMEM((2, dm, de), F32), pltpu.VMEM((2, dm, de), F32), pltpu.VMEM((2, de, dm), F32),
                        pltpu.VMEM((dm, de), BF16), pltpu.VMEM((dm, de), BF16), pltpu.VMEM((de, dm), BF16),
                        pltpu.SemaphoreType.DMA((2, 3))],
    )
    return pl.pallas_call(
        _expert_kernel,
        grid_spec=grid_spec,
        out_shape=jax.ShapeDtypeStruct((rows, wd), jnp.uint32),
        compiler_params=_params(1, 56),
    )(block_e, block_slot, block_next, n_used, x_sorted, w1, w3, w2)


def _shared_kernel(x_ref, w1_ref, w3_ref, w2_ref, o_ref):
    x = x_ref[...]
    a = jnp.dot(x, w1_ref[...], preferred_element_type=F32)
    b = jnp.dot(x, w3_ref[...], preferred_element_type=F32)
    hmid = (a * _sigmoid(a) * b).astype(BF16)
    o_ref[...] = jnp.dot(hmid, w2_ref[...], preferred_element_type=F32)


def _shared_expert(xb, w1, w3, w2):
    m, dm = xb.shape
    de = w1.shape[1]
    tm = min(512, m)
    return pl.pallas_call(
        _shared_kernel,
        grid=(m // tm,),
        in_specs=[pl.BlockSpec((tm, dm), lambda i: (i, 0)),
                  pl.BlockSpec((dm, de), lambda i: (0, 0)),
                  pl.BlockSpec((dm, de), lambda i: (0, 0)),
                  pl.BlockSpec((de, dm), lambda i: (0, 0))],
        out_specs=pl.BlockSpec((tm, dm), lambda i: (i, 0)),
        out_shape=jax.ShapeDtypeStruct((m, dm), F32),
        compiler_params=_params(1, 32),
    )(xb, w1, w3, w2)


def _combine_kernel(dcur_ref, dnext_ref, h_ref, sh_ref, gate_ref, g_ref, b_ref, y_ref, o_ref, ybuf, sems):
    i = pl.program_id(0)
    n = pl.num_programs(0)
    tm = h_ref.shape[0]
    slot = lax.rem(i, 2)

    def copy(d_ref, s, t, k):
        return pltpu.make_async_copy(y_ref.at[pl.ds(d_ref[0, k, t], 1), :],
                                     ybuf.at[s, pl.ds(k * tm + t, 1), :], sems.at[s])

    def start_tile(d_ref, s):
        def body(t, carry):
            for k in range(TOP_K):
                copy(d_ref, s, t, k).start()
            return carry
        lax.fori_loop(0, tm, body, 0)

    @pl.when(i == 0)
    def _():
        start_tile(dcur_ref, 0)

    @pl.when(i + 1 < n)
    def _():
        start_tile(dnext_ref, 1 - slot)

    pltpu.make_async_copy(y_ref.at[pl.ds(0, TOP_K * tm), :], ybuf.at[slot], sems.at[slot]).wait()

    gate = gate_ref[...]
    half = ybuf.shape[2]
    yh = jnp.zeros((tm, half), F32)
    yl = jnp.zeros((tm, half), F32)
    for k in range(TOP_K):
        w = ybuf[slot, k * tm:(k + 1) * tm, :]
        gk = gate[:, k:k + 1]
        yh = yh + lax.bitcast_convert_type(w & jnp.uint32(0xFFFF0000), F32) * gk
        yl = yl + lax.bitcast_convert_type(w << 16, F32) * gk
    y = sh_ref[...] + jnp.concatenate([yh, yl], axis=1)
    o_ref[...] = _layer_norm(DN_ALPHA * h_ref[...] + y, g_ref[...], b_ref[...])


def _combine_ln(h, shared, y_sorted, dest, gate, g, b):
    m, dm = h.shape
    tm = min(128, m)
    n = m // tm
    dest3 = dest.reshape(TOP_K, n, tm).transpose(1, 0, 2)
    return pl.pallas_call(
        _combine_kernel,
        grid=(n,),
        in_specs=[pl.BlockSpec((1, TOP_K, tm), lambda i: (i, 0, 0), memory_space=pltpu.SMEM),
                  pl.BlockSpec((1, TOP_K, tm), lambda i: (jnp.minimum(i + 1, n - 1), 0, 0), memory_space=pltpu.SMEM),
                  pl.BlockSpec((tm, dm), lambda i: (i, 0)),
                  pl.BlockSpec((tm, dm), lambda i: (i, 0)),
                  pl.BlockSpec((tm, TOP_K), lambda i: (i, 0)),
                  pl.BlockSpec((1, dm), lambda i: (0, 0)),
                  pl.BlockSpec((1, dm), lambda i: (0, 0)),
                  pl.BlockSpec(memory_space=pl.ANY)],
        out_specs=pl.BlockSpec((tm, dm), lambda i: (i, 0)),
        out_shape=jax.ShapeDtypeStruct((m, dm), F32),
        scratch_shapes=[pltpu.VMEM((2, TOP_K * tm, dm // 2), jnp.uint32), pltpu.SemaphoreType.DMA((2,))],
        compiler_params=_params(1, 40),
    )(dest3, dest3, h, shared, gate, g.reshape(1, dm), b.reshape(1, dm), y_sorted)


def _moe_sublayer(h, hb, hp, router_w, router_bias, w1, w3, w2, sw1, sw3, sw2, g, b):
    n_tok, dm = h.shape
    idx_t, gate_t, rank_t, counts = _router(h, router_w, router_bias)
    counts = counts[:, 0]
    padded = (counts + MOE_ROWS - 1) // MOE_ROWS * MOE_ROWS
    pad_end = jnp.cumsum(padded)
    n_blocks = (n_tok * TOP_K + N_EXPERTS * (MOE_ROWS - 1) + MOE_ROWS - 1) // MOE_ROWS
    block_start = jnp.arange(n_blocks, dtype=jnp.int32) * MOE_ROWS
    block_e = jnp.minimum(jnp.sum((pad_end[None, :] <= block_start[:, None]).astype(jnp.int32), axis=1),
                          N_EXPERTS - 1)
    n_used = (pad_end[-1] // MOE_ROWS).astype(jnp.int32).reshape(1)
    e_ids = jnp.arange(N_EXPERTS, dtype=jnp.int32)
    has_rows = counts > 0
    ordinal = jnp.cumsum(has_rows.astype(jnp.int32)) - 1
    later = jnp.where(has_rows[None, :] & (e_ids[None, :] > e_ids[:, None]), e_ids[None, :], N_EXPERTS)
    next_e = jnp.min(later, axis=1)
    next_e = jnp.where(next_e < N_EXPERTS, next_e, -1).astype(jnp.int32)
    block_slot = (ordinal[block_e] % 2).astype(jnp.int32)
    block_next = next_e[block_e]
    dest = _dest_rows(idx_t, rank_t, pad_end - padded)
    x_sorted = _dispatch(hp, dest, n_blocks * MOE_ROWS)
    y_sorted = _expert_blocks(x_sorted, block_e, block_slot, block_next, n_used, w1, w3, w2)
    shared = _shared_expert(hb, sw1.astype(BF16), sw3.astype(BF16), sw2.astype(BF16))
    return _combine_ln(h, shared, y_sorted, dest, gate_t.T, g, b)


def kernel(x, mem, positions, rel_bias_table, w_in, diff_lambda, diff_subln, idx_k_g, idx_k_b, kv_norm_g, w_uk, w_uv, w_proj_diff, w_proj_dsa, w_mix_out, ln1_g, ln1_b, xattn_wq, xattn_wkv, xattn_wo, ln2_g, ln2_b, router_w, router_bias, exp_w1, exp_w3, exp_w2, sh_w1, sh_w3, sh_w2, ln3_g, ln3_b):
    del positions
    batch, seq, dm = x.shape
    n_mem = mem.shape[1]
    n_tok = batch * seq
    lambda_init = 0.8 - 0.6 * math.exp(-0.3 * 0)
    blk = min(ATT_BLOCK, seq)

    dq = DIFF_HEADS * DIFF_V_DIM
    c_qs = 3 * dq
    c_ckv = c_qs + DSA_HEADS * DSA_HEAD_DIM
    c_qi = c_ckv + DSA_KV_RANK
    c_ki = c_qi + IDX_HEADS * IDX_HEAD_DIM
    c_gd = c_ki + IDX_HEAD_DIM + IDX_HEADS
    c_gs = c_gd + dm

    xf = x.reshape(n_tok, dm)
    xb = xf.astype(BF16)
    w_in0 = w_in[0]
    qkv = _matmul(xb, w_in0, 0, c_ckv, BF16, 1024, 1024)
    ckv = _matmul(xb, w_in0, c_ckv, DSA_KV_RANK, F32, 1024, 512)
    qi = _matmul(xb, w_in0, c_qi, c_ki - c_qi, BF16, 1024, 512)
    w_tail = jnp.pad(w_in0[:, c_ki:c_gd], ((0, 0), (0, 128 - (c_gd - c_ki))))
    kiw = _matmul(xb, w_tail, 0, 128, F32, 1024, 128)

    dsa_tiles = _rel_bias_tiles(rel_bias_table[:, DIFF_HEADS:], blk)
    diff_tiles = _rel_bias_tiles(rel_bias_table[:, :DIFF_HEADS], min(DIFF_BLOCK, seq))
    o_d = _diff_attention(qkv, diff_tiles, diff_lambda[0], diff_subln[0], batch, seq, lambda_init)

    k_sel = min(DSA_TOPK_MAX, seq // 4)
    sel = _indexer_mask(qi, kiw, idx_k_g[0], idx_k_b[0], batch, seq, k_sel, qi_col_block=0)
    o_s = _dsa_attention(qkv, c_qs // (DSA_HEADS * DSA_HEAD_DIM), ckv, 0, kv_norm_g[0], sel,
                         w_uk[0].astype(BF16), w_uv[0].astype(BF16), dsa_tiles, batch, seq)

    merged = _gated_merge(xb, o_d, o_s, w_in0[:, c_gd:c_gs].astype(BF16), w_in0[:, c_gs:c_gs + dm].astype(BF16),
                          w_proj_diff[0].astype(BF16), w_proj_dsa[0].astype(BF16))
    h1, h1b = _matmul_res_ln(merged, w_mix_out[0].astype(BF16), xf, ln1_g[0], ln1_b[0])

    memb = mem.reshape(batch * n_mem, dm).astype(BF16)
    kv = _matmul(memb, xattn_wkv[0], 0, 2 * XATTN_HEADS * XATTN_HEAD_DIM, BF16, 1024, 512)
    h2, h2b, h2p = _cross_attention(h1b, h1, xattn_wq[0].astype(BF16), kv, xattn_wo[0].astype(BF16),
                                    ln2_g[0], ln2_b[0], batch, seq, n_mem)

    out = _moe_sublayer(h2, h2b, h2p, router_w[0], router_bias[0], exp_w1[0], exp_w3[0], exp_w2[0],
                        sh_w1[0], sh_w3[0], sh_w2[0], ln3_g[0], ln3_b[0])
    return out.reshape(batch, seq, dm)
```

```python
import functools
import math

import jax
import jax.numpy as jnp
from jax import lax
from jax.experimental import pallas as pl
from jax.experimental.pallas import tpu as pltpu

F32 = jnp.float32
BF16 = jnp.bfloat16
NEG = -1e30
INT_MIN = -(2 ** 31)
MIB = 1024 * 1024

LN_EPS = 1e-5
DIFF_HEADS = 8
DIFF_HEAD_DIM = 128
DIFF_V_DIM = 2 * DIFF_HEAD_DIM
DSA_HEADS = 8
DSA_HEAD_DIM = 128
DSA_KV_RANK = 512
IDX_HEADS = 16
IDX_HEAD_DIM = 64
DSA_TOPK_MAX = 256
REL_BUCKETS = 32
REL_MAX_DIST = 128
XATTN_HEADS = 4
XATTN_HEAD_DIM = 128
N_EXPERTS = 64
EXPERT_DIM = 512
TOP_K = 8
N_GROUPS = 8
TOPK_GROUPS = 4
ROUTED_SCALE = 2.5
DEPTH = 1
DN_ALPHA = (2 * DEPTH) ** 0.25

ATT_BLOCK = 256
DIFF_BLOCK = 512
MOE_ROWS = 256
NT_DIMS = (((1,), (1,)), ((), ()))


def _params(n_grid, vmem_mib):
    return pltpu.CompilerParams(dimension_semantics=("arbitrary",) * n_grid,
                                vmem_limit_bytes=vmem_mib * MIB)


def _sigmoid(x):
    return 1.0 / (1.0 + jnp.exp(-x))


def _layer_norm(z, g, b):
    mu = jnp.mean(z, axis=1, keepdims=True)
    zc = z - mu
    var = jnp.mean(zc * zc, axis=1, keepdims=True)
    return zc * lax.rsqrt(var + LN_EPS) * g + b


def _mm_kernel(a_ref, b_ref, o_ref, bq_ref):
    @pl.when(pl.program_id(1) == 0)
    def _():
        bq_ref[...] = b_ref[...].astype(BF16)

    o_ref[...] = jnp.dot(a_ref[...], bq_ref[...], preferred_element_type=F32).astype(o_ref.dtype)


def _matmul(a, b, col_start, n_cols, out_dtype, tm, tn):
    m, k = a.shape
    tm = min(tm, m)
    assert col_start % tn == 0 and n_cols % tn == 0 and m % tm == 0
    off = col_start // tn
    return pl.pallas_call(
        _mm_kernel,
        grid=(n_cols // tn, m // tm),
        in_specs=[pl.BlockSpec((tm, k), lambda j, i: (i, 0)),
                  pl.BlockSpec((k, tn), lambda j, i: (0, j + off))],
        out_specs=pl.BlockSpec((tm, tn), lambda j, i: (i, j)),
        out_shape=jax.ShapeDtypeStruct((m, n_cols), out_dtype),
        scratch_shapes=[pltpu.VMEM((k, tn), BF16)],
        compiler_params=_params(2, 48),
    )(a, b)


def _rel_bucket(n):
    n = jnp.maximum(n, 0)
    max_exact = REL_BUCKETS // 2
    nf = jnp.maximum(n, 1).astype(F32)
    large = max_exact + (jnp.log(nf / max_exact) / math.log(REL_MAX_DIST / max_exact)
                         * (REL_BUCKETS - max_exact)).astype(jnp.int32)
    large = jnp.minimum(large, REL_BUCKETS - 1)
    return jnp.where(n < max_exact, n, large)


def _rel_bias_tiles(table, blk):
    assert blk >= REL_MAX_DIST
    h = table.shape[1]
    w = 2 * blk
    j = jnp.arange(w, dtype=jnp.int32)
    tiles = []
    for back in range(2):
        n = jnp.where(j < blk, back * blk - j, back * blk + w - j)
        bucket = _rel_bucket(n)
        prof = jnp.zeros((h, w), F32)
        for bkt in range(REL_BUCKETS):
            prof = jnp.where(bucket == bkt, table[bkt][:, None], prof)
        prof = jnp.where(n >= 0, prof, NEG)
        skew = jnp.tile(prof, (1, blk))[:, :blk * (w - 1)].reshape(h, blk, w - 1)
        tiles.append(skew[:, :, :blk])
    tiles.append(jnp.broadcast_to(table[REL_BUCKETS - 1][:, None, None], (h, blk, blk)))
    return jnp.stack(tiles, axis=1).astype(F32)


def _lane_repeat(x, n, axis=1):
    assert axis == 1
    return x if n == 1 else jnp.concatenate([x] * n, axis=1)


def _online_softmax_step(s, v, m_ref, l_ref, acc_ref):
    tk = s.shape[1]
    e = v.shape[1]
    m_prev = m_ref[...]
    m_next = jnp.maximum(m_prev, jnp.max(s, axis=1, keepdims=True))
    p = jnp.exp(s - _lane_repeat(m_next, tk // 128, axis=1))
    alpha = jnp.exp(m_prev - m_next)
    l_ref[...] = alpha * l_ref[...] + jnp.sum(p, axis=1, keepdims=True)
    m_ref[...] = m_next
    acc_ref[...] = (acc_ref[...] * _lane_repeat(alpha, e // 128, axis=1)
                    + jnp.dot(p.astype(BF16), v, preferred_element_type=F32))


def _diff_attn_kernel(q_ref, k_ref, v_ref, bias_ref, lam_ref, g_ref, o_ref, m_scr, l_scr, acc_scr,
                      *, blk, lambda_init):
    i = pl.program_id(2)
    d = DIFF_HEAD_DIM
    scale = d ** -0.5
    m_scr[...] = jnp.full(m_scr.shape, NEG, F32)
    l_scr[...] = jnp.zeros(l_scr.shape, F32)
    acc_scr[...] = jnp.zeros(acc_scr.shape, F32)
    q = q_ref[...]

    def body(j, carry):
        start = pl.multiple_of(j * blk, blk)
        kk = k_ref[pl.ds(start, blk), :]
        vv = v_ref[pl.ds(start, blk), :]
        bt = bias_ref[jnp.minimum(i - j, 2)]
        for c in range(2):
            s = lax.dot_general(q[:, c * d:(c + 1) * d], kk[:, c * d:(c + 1) * d], NT_DIMS,
                                preferred_element_type=F32)
            _online_softmax_step(s * scale + bt, vv, m_scr.at[c], l_scr.at[c], acc_scr.at[c])
        return carry

    lax.fori_loop(0, i + 1, body, 0)

    lp = lam_ref[...]
    lam = (jnp.exp(jnp.sum(lp[0:1] * lp[1:2], axis=1, keepdims=True))
           - jnp.exp(jnp.sum(lp[2:3] * lp[3:4], axis=1, keepdims=True)) + lambda_init)
    o0 = acc_scr[0] * _lane_repeat(1.0 / l_scr[0], 2, axis=1)
    o1 = acc_scr[1] * _lane_repeat(1.0 / l_scr[1], 2, axis=1)
    o = o0 - lam * o1
    ms = jnp.mean(o * o, axis=1, keepdims=True)
    o_ref[...] = (o * lax.rsqrt(ms + LN_EPS) * g_ref[...] * (1.0 - lambda_init)).astype(o_ref.dtype)


def _diff_attention(qkv, bias_tiles, diff_lambda, subln, batch, seq, lambda_init):
    blk = bias_tiles.shape[2]
    nq = seq // blk
    h2d = DIFF_V_DIM
    kern = functools.partial(_diff_attn_kernel, blk=blk, lambda_init=lambda_init)
    return pl.pallas_call(
        kern,
        grid=(batch, DIFF_HEADS, nq),
        in_specs=[pl.BlockSpec((blk, h2d), lambda b, h, i: (b * nq + i, h)),
                  pl.BlockSpec((seq, h2d), lambda b, h, i: (b, DIFF_HEADS + h)),
                  pl.BlockSpec((seq, h2d), lambda b, h, i: (b, 2 * DIFF_HEADS + h)),
                  pl.BlockSpec((None, 3, blk, blk), lambda b, h, i: (h, 0, 0, 0)),
                  pl.BlockSpec((4, DIFF_HEAD_DIM), lambda b, h, i: (0, 0)),
                  pl.BlockSpec((1, h2d), lambda b, h, i: (0, 0))],
        out_specs=pl.BlockSpec((blk, h2d), lambda b, h, i: (b * nq + i, h)),
        out_shape=jax.ShapeDtypeStruct((batch * seq, DIFF_HEADS * h2d), BF16),
        scratch_shapes=[pltpu.VMEM((2, blk, 128), F32), pltpu.VMEM((2, blk, 128), F32),
                        pltpu.VMEM((2, blk, h2d), F32)],
        compiler_params=_params(3, 48),
    )(qkv, qkv, qkv, bias_tiles, diff_lambda, subln.reshape(1, h2d))


def _indexer_kernel(qi_ref, kfull_ref, wblk_ref, g_ref, b_ref, o_ref, ki_scr, key_scr, wb_scr, thr_scr,
                    *, blk, seq, k_sel):
    i = pl.program_id(1)

    @pl.when(i == 0)
    def _():
        kr = kfull_ref[:, 0:IDX_HEAD_DIM]
        ki_scr[...] = _layer_norm(kr, g_ref[...], b_ref[...]).astype(BF16)

    w = wblk_ref[:, IDX_HEAD_DIM:IDX_HEAD_DIM + IDX_HEADS] * (IDX_HEADS ** -0.5 * IDX_HEAD_DIM ** -0.5)
    for h in range(IDX_HEADS):
        wb_scr[h] = jnp.broadcast_to(w[:, h:h + 1], (blk, blk))
    qi = qi_ref[...].astype(BF16)
    row = i * blk + lax.broadcasted_iota(jnp.int32, (blk, blk), 0)
    col = lax.broadcasted_iota(jnp.int32, (blk, blk), 1)

    def score_tile(j, carry):
        start = pl.multiple_of(j * blk, blk)
        kc = ki_scr[pl.ds(start, blk), :]
        score = jnp.zeros((blk, blk), F32)
        for h in range(IDX_HEADS):
            lg = lax.dot_general(qi[:, h * IDX_HEAD_DIM:(h + 1) * IDX_HEAD_DIM], kc, NT_DIMS,
                                 preferred_element_type=F32)
            score = score + jnp.maximum(lg, 0.0) * wb_scr[h]
        bits = lax.bitcast_convert_type(score, jnp.int32)
        key = bits ^ ((bits >> 31) & jnp.int32(0x7FFFFFFF))
        key_scr[j] = jnp.where(start + col <= row, key, INT_MIN)
        return carry

    lax.fori_loop(0, i + 1, score_tile, 0)

    def make_search(n_tiles):
        def search(it, thr):
            trial = thr + lax.shift_left(jnp.int32(1), 31 - it)
            c = jnp.zeros((blk, 128), F32)
            for j in range(n_tiles):
                ge = jnp.where(key_scr[j] >= trial, 1.0, 0.0)
                for part in range(blk // 128):
                    c = c + ge[:, part * 128:(part + 1) * 128]
            return jnp.where(jnp.sum(c, axis=1, keepdims=True) >= k_sel, trial, thr)
        return search

    thr_scr[...] = jnp.full(thr_scr.shape, INT_MIN, jnp.int32)
    for jj in range(seq // blk):
        if (jj + 1) * blk > k_sel:
            @pl.when(i == jj)
            def _():
                t = lax.fori_loop(0, 32, make_search(jj + 1), jnp.full((blk, 1), INT_MIN, jnp.int32))
                thr_scr[...] = jnp.broadcast_to(t, thr_scr.shape)

    thr = thr_scr[...][:, 0:1]
    for jj in range(seq // blk):
        @pl.when(jj <= i)
        def _():
            keep = (jj * blk + col <= row) & (key_scr[jj] >= thr)
            o_ref[jj] = jnp.where(keep, 0.0, NEG).astype(o_ref.dtype)

        @pl.when(jj > i)
        def _():
            o_ref[jj] = jnp.full((blk, blk), NEG, o_ref.dtype)


def _indexer_mask(qi, kiw, idx_k_g, idx_k_b, batch, seq, k_sel, qi_col_block):
    blk = min(ATT_BLOCK, seq)
    nq = seq // blk
    qw = IDX_HEADS * IDX_HEAD_DIM
    kern = functools.partial(_indexer_kernel, blk=blk, seq=seq, k_sel=k_sel)
    return pl.pallas_call(
        kern,
        grid=(batch, nq),
        in_specs=[pl.BlockSpec((blk, qw), lambda b, i: (b * nq + i, qi_col_block)),
                  pl.BlockSpec((seq, 128), lambda b, i: (b, 0)),
                  pl.BlockSpec((blk, 128), lambda b, i: (b * nq + i, 0)),
                  pl.BlockSpec((1, IDX_HEAD_DIM), lambda b, i: (0, 0)),
                  pl.BlockSpec((1, IDX_HEAD_DIM), lambda b, i: (0, 0))],
        out_specs=pl.BlockSpec((None, nq, blk, blk), lambda b, i: (b, 0, i, 0)),
        out_shape=jax.ShapeDtypeStruct((batch, nq, seq, blk), BF16),
        scratch_shapes=[pltpu.VMEM((seq, IDX_HEAD_DIM), BF16), pltpu.VMEM((nq, blk, blk), jnp.int32),
                        pltpu.VMEM((IDX_HEADS, blk, blk), F32), pltpu.VMEM((blk, 128), jnp.int32)],
        compiler_params=_params(2, 48),
    )(qi, kiw, kiw, idx_k_g.reshape(1, -1), idx_k_b.reshape(1, -1))


def _dsa_attn_kernel(qs_ref, ckv_ref, g_ref, sel_ref, wuk_ref, wuv_ref, bias_ref, o_ref,
                     c_scr, ql_scr, sb_scr, m_scr, l_scr, acc_scr, *, blk):
    i = pl.program_id(1)
    d = DSA_HEAD_DIM
    scale = d ** -0.5

    @pl.when(i == 0)
    def _():
        x = ckv_ref[...]
        c = x * lax.rsqrt(jnp.mean(x * x, axis=1, keepdims=True) + LN_EPS) * g_ref[...]
        c_scr[...] = c.astype(BF16)

    qs = qs_ref[...]
    for h in range(DSA_HEADS):
        ql = jnp.dot(qs[:, h * d:(h + 1) * d], wuk_ref[h], preferred_element_type=F32)
        ql_scr[h * blk:(h + 1) * blk, :] = (ql * scale).astype(BF16)

    m_scr[...] = jnp.full(m_scr.shape, NEG, F32)
    l_scr[...] = jnp.zeros(l_scr.shape, F32)
    acc_scr[...] = jnp.zeros(acc_scr.shape, F32)

    def body(j, carry):
        start = pl.multiple_of(j * blk, blk)
        cc = c_scr[pl.ds(start, blk), :]
        s = lax.dot_general(ql_scr[...], cc, NT_DIMS, preferred_element_type=F32)
        selm = sel_ref[j].astype(F32)
        back = jnp.minimum(i - j, 2)
        for h in range(DSA_HEADS):
            sb_scr[h * blk:(h + 1) * blk, :] = bias_ref[h, back] + selm
        _online_softmax_step(s + sb_scr[...], cc, m_scr, l_scr, acc_scr)
        return carry

    lax.fori_loop(0, i + 1, body, 0)

    o_lat = (acc_scr[...] * _lane_repeat(1.0 / l_scr[...], DSA_KV_RANK // 128, axis=1)).astype(BF16)
    for h in range(DSA_HEADS):
        o_ref[:, h * d:(h + 1) * d] = jnp.dot(o_lat[h * blk:(h + 1) * blk], wuv_ref[h],
                                              preferred_element_type=F32).astype(o_ref.dtype)


def _dsa_attention(qkv, qs_col_block, ckv, ckv_col_block, kv_norm_g, sel, w_uk, w_uv, bias_tiles, batch, seq):
    blk = min(ATT_BLOCK, seq)
    nq = seq // blk
    hd = DSA_HEADS * DSA_HEAD_DIM
    r = DSA_KV_RANK
    rows = DSA_HEADS * blk
    kern = functools.partial(_dsa_attn_kernel, blk=blk)
    return pl.pallas_call(
        kern,
        grid=(batch, nq),
        in_specs=[pl.BlockSpec((blk, hd), lambda b, i: (b * nq + i, qs_col_block)),
                  pl.BlockSpec((seq, r), lambda b, i: (b, ckv_col_block)),
                  pl.BlockSpec((1, r), lambda b, i: (0, 0)),
                  pl.BlockSpec((None, nq, blk, blk), lambda b, i: (b, 0, i, 0)),
                  pl.BlockSpec((DSA_HEADS, DSA_HEAD_DIM, r), lambda b, i: (0, 0, 0)),
                  pl.BlockSpec((DSA_HEADS, r, DSA_HEAD_DIM), lambda b, i: (0, 0, 0)),
                  pl.BlockSpec((DSA_HEADS, 3, blk, blk), lambda b, i: (0, 0, 0, 0))],
        out_specs=pl.BlockSpec((blk, hd), lambda b, i: (b * nq + i, 0)),
        out_shape=jax.ShapeDtypeStruct((batch * seq, hd), BF16),
        scratch_shapes=[pltpu.VMEM((seq, r), BF16), pltpu.VMEM((rows, r), BF16), pltpu.VMEM((rows, blk), F32),
                        pltpu.VMEM((rows, 128), F32), pltpu.VMEM((rows, 128), F32), pltpu.VMEM((rows, r), F32)],
        compiler_params=_params(2, 56),
    )(qkv, ckv, kv_norm_g.reshape(1, r), sel, w_uk, w_uv, bias_tiles)


def _merge_kernel(x_ref, od_ref, os_ref, wgd_ref, wgs_ref, wpd_ref, wps_ref, o_ref):
    x = x_ref[...]
    gd = jnp.dot(x, wgd_ref[...], preferred_element_type=F32)
    gs = jnp.dot(x, wgs_ref[...], preferred_element_type=F32)
    pd = jnp.dot(od_ref[...], wpd_ref[...], preferred_element_type=F32)
    ps = jnp.dot(os_ref[...], wps_ref[...], preferred_element_type=F32)
    o_ref[...] = (_sigmoid(gd) * pd + _sigmoid(gs) * ps).astype(o_ref.dtype)


def _gated_merge(xb, o_d, o_s, wgd, wgs, wpd, wps):
    m, dm = xb.shape
    tm = min(512, m)
    tn = 512
    kd, ks = o_d.shape[1], o_s.shape[1]
    return pl.pallas_call(
        _merge_kernel,
        grid=(dm // tn, m // tm),
        in_specs=[pl.BlockSpec((tm, dm), lambda j, i: (i, 0)),
                  pl.BlockSpec((tm, kd), lambda j, i: (i, 0)),
                  pl.BlockSpec((tm, ks), lambda j, i: (i, 0)),
                  pl.BlockSpec((dm, tn), lambda j, i: (0, j)),
                  pl.BlockSpec((dm, tn), lambda j, i: (0, j)),
                  pl.BlockSpec((kd, tn), lambda j, i: (0, j)),
                  pl.BlockSpec((ks, tn), lambda j, i: (0, j))],
        out_specs=pl.BlockSpec((tm, tn), lambda j, i: (i, j)),
        out_shape=jax.ShapeDtypeStruct((m, dm), BF16),
        compiler_params=_params(2, 40),
    )(xb, o_d, o_s, wgd, wgs, wpd, wps)


def _mm_res_ln_kernel(a_ref, w_ref, res_ref, g_ref, b_ref, o_ref, ob_ref):
    y = jnp.dot(a_ref[...], w_ref[...], preferred_element_type=F32)
    zn = _layer_norm(DN_ALPHA * res_ref[...] + y, g_ref[...], b_ref[...])
    o_ref[...] = zn
    ob_ref[...] = zn.astype(BF16)


def _matmul_res_ln(a, w, res, g, b):
    m, k = a.shape
    n = w.shape[1]
    tm = min(256, m)
    return pl.pallas_call(
        _mm_res_ln_kernel,
        grid=(m // tm,),
        in_specs=[pl.BlockSpec((tm, k), lambda i: (i, 0)),
                  pl.BlockSpec((k, n), lambda i: (0, 0)),
                  pl.BlockSpec((tm, n), lambda i: (i, 0)),
                  pl.BlockSpec((1, n), lambda i: (0, 0)),
                  pl.BlockSpec((1, n), lambda i: (0, 0))],
        out_specs=[pl.BlockSpec((tm, n), lambda i: (i, 0)), pl.BlockSpec((tm, n), lambda i: (i, 0))],
        out_shape=[jax.ShapeDtypeStruct((m, n), F32), jax.ShapeDtypeStruct((m, n), BF16)],
        compiler_params=_params(1, 40),
    )(a, w, res, g.reshape(1, n), b.reshape(1, n))


def _pack_bf16_pairs(zb):
    half = zb.shape[1] // 2
    bits = lax.bitcast_convert_type(zb.astype(F32), jnp.uint32)
    return (bits[:, :half] & jnp.uint32(0xFFFF0000)) | (bits[:, half:] >> 16)


def _unpack_bf16_pairs(w):
    hi = lax.bitcast_convert_type(w & jnp.uint32(0xFFFF0000), F32).astype(BF16)
    lo = lax.bitcast_convert_type(w << 16, F32).astype(BF16)
    return hi, lo


def _xattn_kernel(hb_ref, h_ref, wq_ref, kv_ref, wo_ref, g_ref, b_ref, o_ref, ob_ref, op_ref):
    d = XATTN_HEAD_DIM
    hd = XATTN_HEADS * d
    scale = d ** -0.5
    qb = jnp.dot(hb_ref[...], wq_ref[...], preferred_element_type=F32).astype(BF16)
    kv = kv_ref[...]
    outs = []
    for h in range(XATTN_HEADS):
        s = lax.dot_general(qb[:, h * d:(h + 1) * d], kv[:, h * d:(h + 1) * d], NT_DIMS,
                            preferred_element_type=F32) * scale
        p = jnp.exp(s - jnp.max(s, axis=1, keepdims=True))
        l = jnp.sum(p, axis=1, keepdims=True)
        oh = jnp.dot(p.astype(BF16), kv[:, hd + h * d:hd + (h + 1) * d], preferred_element_type=F32)
        outs.append((oh / l).astype(BF16))
    o = jnp.concatenate(outs, axis=1)
    y = jnp.dot(o, wo_ref[...], preferred_element_type=F32)
    zn = _layer_norm(DN_ALPHA * h_ref[...] + y, g_ref[...], b_ref[...])
    zb = zn.astype(BF16)
    o_ref[...] = zn
    ob_ref[...] = zb
    op_ref[...] = _pack_bf16_pairs(zb)


def _cross_attention(hb, h, wq, kv, wo, g, b, batch, seq, n_mem):
    m, dm = h.shape
    tm = min(256, seq)
    nq = seq // tm
    hd = XATTN_HEADS * XATTN_HEAD_DIM
    return pl.pallas_call(
        _xattn_kernel,
        grid=(batch, nq),
        in_specs=[pl.BlockSpec((tm, dm), lambda bb, i: (bb * nq + i, 0)),
                  pl.BlockSpec((tm, dm), lambda bb, i: (bb * nq + i, 0)),
                  pl.BlockSpec((dm, hd), lambda bb, i: (0, 0)),
                  pl.BlockSpec((n_mem, 2 * hd), lambda bb, i: (bb, 0)),
                  pl.BlockSpec((hd, dm), lambda bb, i: (0, 0)),
                  pl.BlockSpec((1, dm), lambda bb, i: (0, 0)),
                  pl.BlockSpec((1, dm), lambda bb, i: (0, 0))],
        out_specs=[pl.BlockSpec((tm, dm), lambda bb, i: (bb * nq + i, 0)),
                   pl.BlockSpec((tm, dm), lambda bb, i: (bb * nq + i, 0)),
                   pl.BlockSpec((tm, dm // 2), lambda bb, i: (bb * nq + i, 0))],
        out_shape=[jax.ShapeDtypeStruct((m, dm), F32), jax.ShapeDtypeStruct((m, dm), BF16),
                   jax.ShapeDtypeStruct((m, dm // 2), jnp.uint32)],
        compiler_params=_params(2, 40),
    )(hb, h, wq, kv, wo, g.reshape(1, dm), b.reshape(1, dm))


def _split_bf16(x):
    hi = x.astype(BF16)
    return hi, (x - hi.astype(F32)).astype(BF16)


def _router_kernel(h_ref, rwt_ref, rb_ref, idx_ref, gate_ref, rank_ref, cnt_ref, cnt_scr):
    tm = h_ref.shape[0]
    per = N_EXPERTS // N_GROUPS
    h_hi, h_lo = _split_bf16(h_ref[...])
    w_hi, w_lo = _split_bf16(rwt_ref[...])
    logits = (lax.dot_general(w_hi, h_hi, NT_DIMS, preferred_element_type=F32)
              + lax.dot_general(w_hi, h_lo, NT_DIMS, preferred_element_type=F32)
              + lax.dot_general(w_lo, h_hi, NT_DIMS, preferred_element_type=F32))
    scores = _sigmoid(logits)
    sel = scores + rb_ref[...]

    sel3 = sel.reshape(N_GROUPS, per, tm)
    r_iota = lax.broadcasted_iota(jnp.int32, sel3.shape, 1).astype(F32)
    m1 = jnp.max(sel3, axis=1, keepdims=True)
    first = jnp.min(jnp.where(sel3 == m1, r_iota, float(per)), axis=1, keepdims=True)
    m2 = jnp.max(jnp.where(r_iota == first, -jnp.inf, sel3), axis=1, keepdims=True)
    gscore = (m1 + m2).reshape(N_GROUPS, tm)

    g_iota = lax.broadcasted_iota(jnp.int32, gscore.shape, 0).astype(F32)
    keep = jnp.zeros(gscore.shape, F32)
    cur = gscore
    for _ in range(TOPK_GROUPS):
        m = jnp.max(cur, axis=0, keepdims=True)
        first = jnp.min(jnp.where(cur == m, g_iota, float(N_GROUPS)), axis=0, keepdims=True)
        pick = g_iota == first
        keep = jnp.where(pick, 1.0, keep)
        cur = jnp.where(pick, -jnp.inf, cur)
    keep3 = jnp.broadcast_to(keep.reshape(N_GROUPS, 1, tm), sel3.shape)
    cur = jnp.where(keep3 > 0.5, sel3, -jnp.inf).reshape(N_EXPERTS, tm)

    e_iota = lax.broadcasted_iota(jnp.int32, cur.shape, 0).astype(F32)
    gsum = jnp.zeros((1, tm), F32)
    gates, picks = [], []
    for k in range(TOP_K):
        m = jnp.max(cur, axis=0, keepdims=True)
        first = jnp.min(jnp.where(cur == m, e_iota, float(N_EXPERTS)), axis=0, keepdims=True)
        pick = e_iota == first
        gk = jnp.sum(jnp.where(pick, scores, 0.0), axis=0, keepdims=True)
        idx_ref[k:k + 1, :] = first.astype(jnp.int32)
        gates.append(gk)
        picks.append(pick)
        gsum = gsum + gk
        cur = jnp.where(pick, -jnp.inf, cur)
    for k in range(TOP_K):
        gate_ref[k:k + 1, :] = gates[k] / gsum * ROUTED_SCALE

    @pl.when(pl.program_id(0) == 0)
    def _():
        cnt_scr[...] = jnp.zeros(cnt_scr.shape, F32)

    mask = jnp.zeros(cur.shape, F32)
    for k in range(TOP_K):
        mask = jnp.where(picks[k], 1.0, mask)
    upper = (lax.broadcasted_iota(jnp.int32, (tm, tm), 0) <= lax.broadcasted_iota(jnp.int32, (tm, tm), 1))
    cum = jnp.dot(mask.astype(BF16), jnp.where(upper, 1.0, 0.0).astype(BF16), preferred_element_type=F32)
    before = cnt_scr[...][:, 0:1] + cum - mask
    for k in range(TOP_K):
        rank_ref[k:k + 1, :] = jnp.sum(jnp.where(picks[k], before, 0.0), axis=0, keepdims=True).astype(jnp.int32)
    cnt_scr[...] = cnt_scr[...] + jnp.sum(mask, axis=1, keepdims=True)
    cnt_ref[...] = cnt_scr[...].astype(jnp.int32)


def _router(h, router_w, router_bias):
    m, dm = h.shape
    tm = min(512, m)
    return pl.pallas_call(
        _router_kernel,
        grid=(m // tm,),
        in_specs=[pl.BlockSpec((tm, dm), lambda i: (i, 0)),
                  pl.BlockSpec((N_EXPERTS, dm), lambda i: (0, 0)),
                  pl.BlockSpec((N_EXPERTS, 1), lambda i: (0, 0))],
        out_specs=[pl.BlockSpec((TOP_K, tm), lambda i: (0, i)), pl.BlockSpec((TOP_K, tm), lambda i: (0, i)),
                   pl.BlockSpec((TOP_K, tm), lambda i: (0, i)), pl.BlockSpec((N_EXPERTS, 128), lambda i: (0, 0))],
        out_shape=[jax.ShapeDtypeStruct((TOP_K, m), jnp.int32), jax.ShapeDtypeStruct((TOP_K, m), F32),
                   jax.ShapeDtypeStruct((TOP_K, m), jnp.int32), jax.ShapeDtypeStruct((N_EXPERTS, 128), jnp.int32)],
        scratch_shapes=[pltpu.VMEM((N_EXPERTS, 128), F32)],
        compiler_params=_params(1, 32),
    )(h, router_w.T, router_bias.reshape(N_EXPERTS, 1))


def _dest_kernel(idx_ref, rank_ref, ps_ref, dest_ref):
    tm = idx_ref.shape[1]
    e_iota = lax.broadcasted_iota(jnp.int32, (N_EXPERTS, tm), 0)
    ps = ps_ref[...][:, 0:1]
    for k in range(TOP_K):
        base = jnp.sum(jnp.where(e_iota == idx_ref[k:k + 1, :], ps, 0), axis=0, keepdims=True)
        dest_ref[k:k + 1, :] = rank_ref[k:k + 1, :] + base


def _dest_rows(idx_t, rank_t, pad_start):
    m = idx_t.shape[1]
    tm = min(2048, m)
    ps = jnp.broadcast_to(pad_start.astype(F32).reshape(N_EXPERTS, 1), (N_EXPERTS, 128))
    out = pl.pallas_call(
        _dest_kernel,
        grid=(m // tm,),
        in_specs=[pl.BlockSpec((TOP_K, tm), lambda i: (0, i)), pl.BlockSpec((TOP_K, tm), lambda i: (0, i)),
                  pl.BlockSpec((N_EXPERTS, 128), lambda i: (0, 0))],
        out_specs=pl.BlockSpec((TOP_K, tm), lambda i: (0, i)),
        out_shape=jax.ShapeDtypeStruct((TOP_K, m), F32),
        compiler_params=_params(1, 32),
    )(idx_t, rank_t.astype(F32), ps)
    return out.astype(jnp.int32)


def _dispatch_kernel(dest_ref, x_ref, init_ref, xs_ref, sem):
    del init_ref
    tm = x_ref.shape[0]

    def copy(t, k):
        return pltpu.make_async_copy(x_ref.at[pl.ds(t, 1), :], xs_ref.at[pl.ds(dest_ref[0, k, t], 1), :], sem)

    def start(t, carry):
        for k in range(TOP_K):
            copy(t, k).start()
        return carry

    lax.fori_loop(0, tm, start, 0)
    n_rows = TOP_K * tm
    pltpu.make_async_copy(xs_ref.at[pl.ds(0, n_rows), :], xs_ref.at[pl.ds(n_rows, n_rows), :], sem).wait()


def _dispatch(xp, dest, rows):
    m, wd = xp.shape
    tm = min(512, m)
    dest3 = dest.reshape(TOP_K, m // tm, tm).transpose(1, 0, 2)
    return pl.pallas_call(
        _dispatch_kernel,
        grid=(m // tm,),
        in_specs=[pl.BlockSpec((1, TOP_K, tm), lambda i: (i, 0, 0), memory_space=pltpu.SMEM),
                  pl.BlockSpec((tm, wd), lambda i: (i, 0)),
                  pl.BlockSpec(memory_space=pl.ANY)],
        out_specs=pl.BlockSpec(memory_space=pl.ANY),
        out_shape=jax.ShapeDtypeStruct((rows, wd), xp.dtype),
        scratch_shapes=[pltpu.SemaphoreType.DMA(())],
        input_output_aliases={2: 0},
        compiler_params=_params(1, 32),
    )(dest3, xp, jnp.zeros((rows, wd), xp.dtype))


def _expert_kernel(be_ref, slot_ref, next_ref, nu_ref, x_ref, w1_ref, w3_ref, w2_ref, o_ref,
                   w1f, w3f, w2f, w1b, w3b, w2b, sems):
    i = pl.program_id(0)
    used = i < nu_ref[0]
    e = be_ref[i]
    first = jnp.logical_and(used, jnp.logical_or(i == 0, e != be_ref[jnp.maximum(i - 1, 0)]))

    def copies(expert, s):
        return (pltpu.make_async_copy(w1_ref.at[expert], w1f.at[s], sems.at[s, 0]),
                pltpu.make_async_copy(w3_ref.at[expert], w3f.at[s], sems.at[s, 1]),
                pltpu.make_async_copy(w2_ref.at[expert], w2f.at[s], sems.at[s, 2]))

    @pl.when(jnp.logical_and(used, i == 0))
    def _():
        for c in copies(e, slot_ref[0]):
            c.start()

    @pl.when(first)
    def _():
        s = slot_ref[i]
        for c in copies(e, s):
            c.wait()
        w1b[...] = w1f[s].astype(BF16)
        w3b[...] = w3f[s].astype(BF16)
        w2b[...] = w2f[s].astype(BF16)

        @pl.when(next_ref[i] >= 0)
        def _():
            for c in copies(next_ref[i], 1 - s):
                c.start()

    @pl.when(used)
    def _():
        xh, xl = _unpack_bf16_pairs(x_ref[...])
        half = xh.shape[1]
        a = (jnp.dot(xh, w1b[0:half, :], preferred_element_type=F32)
             + jnp.dot(xl, w1b[half:, :], preferred_element_type=F32))
        b = (jnp.dot(xh, w3b[0:half, :], preferred_element_type=F32)
             + jnp.dot(xl, w3b[half:, :], preferred_element_type=F32))
        hmid = (a * _sigmoid(a) * b).astype(BF16)
        o_ref[...] = _pack_bf16_pairs(jnp.dot(hmid, w2b[...], preferred_element_type=F32).astype(BF16))

    @pl.when(i >= nu_ref[0])
    def _():
        o_ref[...] = jnp.zeros(o_ref.shape, o_ref.dtype)


def _expert_blocks(x_sorted, block_e, block_slot, block_next, n_used, w1, w3, w2):
    rows, wd = x_sorted.shape
    n_blocks = rows // MOE_ROWS
    dm, de = w1.shape[1], w1.shape[2]
    grid_spec = pltpu.PrefetchScalarGridSpec(
        num_scalar_prefetch=4,
        grid=(n_blocks,),
        in_specs=[pl.BlockSpec((MOE_ROWS, wd), lambda i, *_: (i, 0)),
                  pl.BlockSpec(memory_space=pl.ANY),
                  pl.BlockSpec(memory_space=pl.ANY),
                  pl.BlockSpec(memory_space=pl.ANY)],
        out_specs=pl.BlockSpec((MOE_ROWS, wd), lambda i, *_: (i, 0)),
        scratch_shapes=[pltpu.VMEM((2, dm, de), F32), pltpu.VMEM((2, dm, de), F32), pltpu.VMEM((2, de, dm), F32),
                        pltpu.VMEM((dm, de), BF16), pltpu.VMEM((dm, de), BF16), pltpu.VMEM((de, dm), BF16),
                        pltpu.SemaphoreType.DMA((2, 3))],
    )
    return pl.pallas_call(
        _expert_kernel,
        grid_spec=grid_spec,
        out_shape=jax.ShapeDtypeStruct((rows, wd), jnp.uint32),
        compiler_params=_params(1, 56),
    )(block_e, block_slot, block_next, n_used, x_sorted, w1, w3, w2)


def _shared_kernel(x_ref, w1_ref, w3_ref, w2_ref, o_ref):
    x = x_ref[...]
    a = jnp.dot(x, w1_ref[...], preferred_element_type=F32)
    b = jnp.dot(x, w3_ref[...], preferred_element_type=F32)
    hmid = (a * _sigmoid(a) * b).astype(BF16)
    o_ref[...] = jnp.dot(hmid, w2_ref[...], preferred_element_type=F32)


def _shared_expert(xb, w1, w3, w2):
    m, dm = xb.shape
    de = w1.shape[1]
    tm = min(512, m)
    return pl.pallas_call(
        _shared_kernel,
        grid=(m // tm,),
        in_specs=[pl.BlockSpec((tm, dm), lambda i: (i, 0)),
                  pl.BlockSpec((dm, de), lambda i: (0, 0)),
                  pl.BlockSpec((dm, de), lambda i: (0, 0)),
                  pl.BlockSpec((de, dm), lambda i: (0, 0))],
        out_specs=pl.BlockSpec((tm, dm), lambda i: (i, 0)),
        out_shape=jax.ShapeDtypeStruct((m, dm), F32),
        compiler_params=_params(1, 32),
    )(xb, w1, w3, w2)


def _combine_kernel(dcur_ref, dnext_ref, h_ref, sh_ref, gate_ref, g_ref, b_ref, y_ref, o_ref, ybuf, sems):
    i = pl.program_id(0)
    n = pl.num_programs(0)
    tm = h_ref.shape[0]
    slot = lax.rem(i, 2)

    def copy(d_ref, s, t, k):
        return pltpu.make_async_copy(y_ref.at[pl.ds(d_ref[0, k, t], 1), :],
                                     ybuf.at[s, pl.ds(k * tm + t, 1), :], sems.at[s])

    def start_tile(d_ref, s):
        def body(t, carry):
            for k in range(TOP_K):
                copy(d_ref, s, t, k).start()
            return carry
        lax.fori_loop(0, tm, body, 0)

    @pl.when(i == 0)
    def _():
        start_tile(dcur_ref, 0)

    for s in range(2):
        @pl.when(jnp.logical_and(i + 1 < n, slot == 1 - s))
        def _():
            start_tile(dnext_ref, s)

    pltpu.make_async_copy(y_ref.at[pl.ds(0, TOP_K * tm), :], ybuf.at[slot], sems.at[slot]).wait()

    gate = gate_ref[...]
    half = ybuf.shape[2]
    yh = jnp.zeros((tm, half), F32)
    yl = jnp.zeros((tm, half), F32)
    for k in range(TOP_K):
        w = ybuf[slot, k * tm:(k + 1) * tm, :]
        gk = gate[:, k:k + 1]
        yh = yh + lax.bitcast_convert_type(w & jnp.uint32(0xFFFF0000), F32) * gk
        yl = yl + lax.bitcast_convert_type(w << 16, F32) * gk
    y = sh_ref[...] + jnp.concatenate([yh, yl], axis=1)
    o_ref[...] = _layer_norm(DN_ALPHA * h_ref[...] + y, g_ref[...], b_ref[...])


def _combine_ln(h, shared, y_sorted, dest, gate, g, b):
    m, dm = h.shape
    tm = min(128, m)
    n = m // tm
    dest3 = dest.reshape(TOP_K, n, tm).transpose(1, 0, 2)
    return pl.pallas_call(
        _combine_kernel,
        grid=(n,),
        in_specs=[pl.BlockSpec((1, TOP_K, tm), lambda i: (i, 0, 0), memory_space=pltpu.SMEM),
                  pl.BlockSpec((1, TOP_K, tm), lambda i: (jnp.minimum(i + 1, n - 1), 0, 0), memory_space=pltpu.SMEM),
                  pl.BlockSpec((tm, dm), lambda i: (i, 0)),
                  pl.BlockSpec((tm, dm), lambda i: (i, 0)),
                  pl.BlockSpec((tm, TOP_K), lambda i: (i, 0)),
                  pl.BlockSpec((1, dm), lambda i: (0, 0)),
                  pl.BlockSpec((1, dm), lambda i: (0, 0)),
                  pl.BlockSpec(memory_space=pl.ANY)],
        out_specs=pl.BlockSpec((tm, dm), lambda i: (i, 0)),
        out_shape=jax.ShapeDtypeStruct((m, dm), F32),
        scratch_shapes=[pltpu.VMEM((2, TOP_K * tm, dm // 2), jnp.uint32), pltpu.SemaphoreType.DMA((2,))],
        compiler_params=_params(1, 40),
    )(dest3, dest3, h, shared, gate, g.reshape(1, dm), b.reshape(1, dm), y_sorted)


def _moe_sublayer(h, hb, hp, router_w, router_bias, w1, w3, w2, sw1, sw3, sw2, g, b):
    n_tok, dm = h.shape
    idx_t, gate_t, rank_t, counts = _router(h, router_w, router_bias)
    counts = counts[:, 0]
    padded = (counts + MOE_ROWS - 1) // MOE_ROWS * MOE_ROWS
    pad_end = jnp.cumsum(padded)
    n_blocks = (n_tok * TOP_K + N_EXPERTS * (MOE_ROWS - 1) + MOE_ROWS - 1) // MOE_ROWS
    block_start = jnp.arange(n_blocks, dtype=jnp.int32) * MOE_ROWS
    block_e = jnp.minimum(jnp.sum((pad_end[None, :] <= block_start[:, None]).astype(jnp.int32), axis=1),
                          N_EXPERTS - 1)
    n_used = (pad_end[-1] // MOE_ROWS).astype(jnp.int32).reshape(1)
    e_ids = jnp.arange(N_EXPERTS, dtype=jnp.int32)
    has_rows = counts > 0
    ordinal = jnp.cumsum(has_rows.astype(jnp.int32)) - 1
    later = jnp.where(has_rows[None, :] & (e_ids[None, :] > e_ids[:, None]), e_ids[None, :], N_EXPERTS)
    next_e = jnp.min(later, axis=1)
    next_e = jnp.where(next_e < N_EXPERTS, next_e, -1).astype(jnp.int32)
    block_slot = (ordinal[block_e] % 2).astype(jnp.int32)
    block_next = next_e[block_e]
    dest = _dest_rows(idx_t, rank_t, pad_end - padded)
    x_sorted = _dispatch(hp, dest, n_blocks * MOE_ROWS)
    y_sorted = _expert_blocks(x_sorted, block_e, block_slot, block_next, n_used, w1, w3, w2)
    shared = _shared_expert(hb, sw1.astype(BF16), sw3.astype(BF16), sw2.astype(BF16))
    return _combine_ln(h, shared, y_sorted, dest, gate_t.T, g, b)


def kernel(x, mem, positions, rel_bias_table, w_in, diff_lambda, diff_subln, idx_k_g, idx_k_b, kv_norm_g, w_uk, w_uv, w_proj_diff, w_proj_dsa, w_mix_out, ln1_g, ln1_b, xattn_wq, xattn_wkv, xattn_wo, ln2_g, ln2_b, router_w, router_bias, exp_w1, exp_w3, exp_w2, sh_w1, sh_w3, sh_w2, ln3_g, ln3_b):
    del positions
    batch, seq, dm = x.shape
    n_mem = mem.shape[1]
    n_tok = batch * seq
    lambda_init = 0.8 - 0.6 * math.exp(-0.3 * 0)
    blk = min(ATT_BLOCK, seq)

    dq = DIFF_HEADS * DIFF_V_DIM
    c_qs = 3 * dq
    c_ckv = c_qs + DSA_HEADS * DSA_HEAD_DIM
    c_qi = c_ckv + DSA_KV_RANK
    c_ki = c_qi + IDX_HEADS * IDX_HEAD_DIM
    c_gd = c_ki + IDX_HEAD_DIM + IDX_HEADS
    c_gs = c_gd + dm

    xf = x.reshape(n_tok, dm)
    xb = xf.astype(BF16)
    w_in0 = w_in[0]
    qkv = _matmul(xb, w_in0, 0, c_ckv, BF16, 1024, 1024)
    ckv = _matmul(xb, w_in0, c_ckv, DSA_KV_RANK, F32, 1024, 512)
    qi = _matmul(xb, w_in0, c_qi, c_ki - c_qi, BF16, 1024, 512)
    w_tail = jnp.pad(w_in0[:, c_ki:c_gd], ((0, 0), (0, 128 - (c_gd - c_ki))))
    kiw = _matmul(xb, w_tail, 0, 128, F32, 1024, 128)

    dsa_tiles = _rel_bias_tiles(rel_bias_table[:, DIFF_HEADS:], blk)
    diff_tiles = _rel_bias_tiles(rel_bias_table[:, :DIFF_HEADS], min(DIFF_BLOCK, seq))
    o_d = _diff_attention(qkv, diff_tiles, diff_lambda[0], diff_subln[0], batch, seq, lambda_init)

    k_sel = min(DSA_TOPK_MAX, seq // 4)
    sel = _indexer_mask(qi, kiw, idx_k_g[0], idx_k_b[0], batch, seq, k_sel, qi_col_block=0)
    o_s = _dsa_attention(qkv, c_qs // (DSA_HEADS * DSA_HEAD_DIM), ckv, 0, kv_norm_g[0], sel,
                         w_uk[0].astype(BF16), w_uv[0].astype(BF16), dsa_tiles, batch, seq)

    merged = _gated_merge(xb, o_d, o_s, w_in0[:, c_gd:c_gs].astype(BF16), w_in0[:, c_gs:c_gs + dm].astype(BF16),
                          w_proj_diff[0].astype(BF16), w_proj_dsa[0].astype(BF16))
    h1, h1b = _matmul_res_ln(merged, w_mix_out[0].astype(BF16), xf, ln1_g[0], ln1_b[0])

    memb = mem.reshape(batch * n_mem, dm).astype(BF16)
    kv = _matmul(memb, xattn_wkv[0], 0, 2 * XATTN_HEADS * XATTN_HEAD_DIM, BF16, 1024, 512)
    h2, h2b, h2p = _cross_attention(h1b, h1, xattn_wq[0].astype(BF16), kv, xattn_wo[0].astype(BF16),
                                    ln2_g[0], ln2_b[0], batch, seq, n_mem)

    out = _moe_sublayer(h2, h2b, h2p, router_w[0], router_bias[0], exp_w1[0], exp_w3[0], exp_w2[0],
                        sh_w1[0], sh_w3[0], sh_w2[0], ln3_g[0], ln3_b[0])
    return out.reshape(batch, seq, dm)
```

```python
import functools
import math

import jax
import jax.numpy as jnp
from jax import lax
from jax.experimental import pallas as pl
from jax.experimental.pallas import tpu as pltpu

F32 = jnp.float32
BF16 = jnp.bfloat16
NEG = -1e30
INT_MIN = -(2 ** 31)
MIB = 1024 * 1024

LN_EPS = 1e-5
DIFF_HEADS = 8
DIFF_HEAD_DIM = 128
DIFF_V_DIM = 2 * DIFF_HEAD_DIM
DSA_HEADS = 8
DSA_HEAD_DIM = 128
DSA_KV_RANK = 512
IDX_HEADS = 16
IDX_HEAD_DIM = 64
DSA_TOPK_MAX = 256
REL_BUCKETS = 32
REL_MAX_DIST = 128
XATTN_HEADS = 4
XATTN_HEAD_DIM = 128
N_EXPERTS = 64
EXPERT_DIM = 512
TOP_K = 8
N_GROUPS = 8
TOPK_GROUPS = 4
ROUTED_SCALE = 2.5
DEPTH = 1
DN_ALPHA = (2 * DEPTH) ** 0.25

ATT_BLOCK = 256
DIFF_BLOCK = 512
MOE_ROWS = 256
NT_DIMS = (((1,), (1,)), ((), ()))


def _params(n_grid, vmem_mib):
    return pltpu.CompilerParams(dimension_semantics=("arbitrary",) * n_grid,
                                vmem_limit_bytes=vmem_mib * MIB)


def _sigmoid(x):
    return 1.0 / (1.0 + jnp.exp(-x))


def _layer_norm(z, g, b):
    mu = jnp.mean(z, axis=1, keepdims=True)
    zc = z - mu
    var = jnp.mean(zc * zc, axis=1, keepdims=True)
    return zc * lax.rsqrt(var + LN_EPS) * g + b


def _mm_kernel(a_ref, b_ref, o_ref, bq_ref):
    @pl.when(pl.program_id(1) == 0)
    def _():
        bq_ref[...] = b_ref[...].astype(BF16)

    o_ref[...] = jnp.dot(a_ref[...], bq_ref[...], preferred_element_type=F32).astype(o_ref.dtype)


def _matmul(a, b, col_start, n_cols, out_dtype, tm, tn):
    m, k = a.shape
    tm = min(tm, m)
    assert col_start % tn == 0 and n_cols % tn == 0 and m % tm == 0
    off = col_start // tn
    return pl.pallas_call(
        _mm_kernel,
        grid=(n_cols // tn, m // tm),
        in_specs=[pl.BlockSpec((tm, k), lambda j, i: (i, 0)),
                  pl.BlockSpec((k, tn), lambda j, i: (0, j + off))],
        out_specs=pl.BlockSpec((tm, tn), lambda j, i: (i, j)),
        out_shape=jax.ShapeDtypeStruct((m, n_cols), out_dtype),
        scratch_shapes=[pltpu.VMEM((k, tn), BF16)],
        compiler_params=_params(2, 48),
    )(a, b)


def _rel_bucket(n):
    n = jnp.maximum(n, 0)
    max_exact = REL_BUCKETS // 2
    nf = jnp.maximum(n, 1).astype(F32)
    large = max_exact + (jnp.log(nf / max_exact) / math.log(REL_MAX_DIST / max_exact)
                         * (REL_BUCKETS - max_exact)).astype(jnp.int32)
    large = jnp.minimum(large, REL_BUCKETS - 1)
    return jnp.where(n < max_exact, n, large)


def _rel_bias_tiles(table, blk):
    assert blk >= REL_MAX_DIST
    h = table.shape[1]
    w = 2 * blk
    j = jnp.arange(w, dtype=jnp.int32)
    tiles = []
    for back in range(2):
        n = jnp.where(j < blk, back * blk - j, back * blk + w - j)
        bucket = _rel_bucket(n)
        prof = jnp.zeros((h, w), F32)
        for bkt in range(REL_BUCKETS):
            prof = jnp.where(bucket == bkt, table[bkt][:, None], prof)
        prof = jnp.where(n >= 0, prof, NEG)
        skew = jnp.tile(prof, (1, blk))[:, :blk * (w - 1)].reshape(h, blk, w - 1)
        tiles.append(skew[:, :, :blk])
    tiles.append(jnp.broadcast_to(table[REL_BUCKETS - 1][:, None, None], (h, blk, blk)))
    return jnp.stack(tiles, axis=1).astype(F32)


def _lane_repeat(x, n, axis=1):
    assert axis == 1
    return x if n == 1 else jnp.concatenate([x] * n, axis=1)


def _online_softmax_step(s, v, m_ref, l_ref, acc_ref):
    tk = s.shape[1]
    e = v.shape[1]
    m_prev = m_ref[...]
    m_next = jnp.maximum(m_prev, jnp.max(s, axis=1, keepdims=True))
    p = jnp.exp(s - _lane_repeat(m_next, tk // 128, axis=1))
    alpha = jnp.exp(m_prev - m_next)
    l_ref[...] = alpha * l_ref[...] + jnp.sum(p, axis=1, keepdims=True)
    m_ref[...] = m_next
    acc_ref[...] = (acc_ref[...] * _lane_repeat(alpha, e // 128, axis=1)
                    + jnp.dot(p.astype(BF16), v, preferred_element_type=F32))


def _diff_attn_kernel(q_ref, k_ref, v_ref, bias_ref, lam_ref, g_ref, o_ref, m_scr, l_scr, acc_scr,
                      *, blk, lambda_init):
    i = pl.program_id(2)
    d = DIFF_HEAD_DIM
    scale = d ** -0.5
    m_scr[...] = jnp.full(m_scr.shape, NEG, F32)
    l_scr[...] = jnp.zeros(l_scr.shape, F32)
    acc_scr[...] = jnp.zeros(acc_scr.shape, F32)
    q = q_ref[...]

    def body(j, carry):
        start = pl.multiple_of(j * blk, blk)
        kk = k_ref[pl.ds(start, blk), :]
        vv = v_ref[pl.ds(start, blk), :]
        bt = bias_ref[jnp.minimum(i - j, 2)]
        for c in range(2):
            s = lax.dot_general(q[:, c * d:(c + 1) * d], kk[:, c * d:(c + 1) * d], NT_DIMS,
                                preferred_element_type=F32)
            _online_softmax_step(s * scale + bt, vv, m_scr.at[c], l_scr.at[c], acc_scr.at[c])
        return carry

    lax.fori_loop(0, i + 1, body, 0)

    lp = lam_ref[...]
    lam = (jnp.exp(jnp.sum(lp[0:1] * lp[1:2], axis=1, keepdims=True))
           - jnp.exp(jnp.sum(lp[2:3] * lp[3:4], axis=1, keepdims=True)) + lambda_init)
    o0 = acc_scr[0] * _lane_repeat(1.0 / l_scr[0], 2, axis=1)
    o1 = acc_scr[1] * _lane_repeat(1.0 / l_scr[1], 2, axis=1)
    o = o0 - lam * o1
    ms = jnp.mean(o * o, axis=1, keepdims=True)
    o_ref[...] = (o * lax.rsqrt(ms + LN_EPS) * g_ref[...] * (1.0 - lambda_init)).astype(o_ref.dtype)


def _diff_attention(qkv, bias_tiles, diff_lambda, subln, batch, seq, lambda_init):
    blk = bias_tiles.shape[2]
    nq = seq // blk
    h2d = DIFF_V_DIM
    kern = functools.partial(_diff_attn_kernel, blk=blk, lambda_init=lambda_init)
    return pl.pallas_call(
        kern,
        grid=(batch, DIFF_HEADS, nq),
        in_specs=[pl.BlockSpec((blk, h2d), lambda b, h, i: (b * nq + i, h)),
                  pl.BlockSpec((seq, h2d), lambda b, h, i: (b, DIFF_HEADS + h)),
                  pl.BlockSpec((seq, h2d), lambda b, h, i: (b, 2 * DIFF_HEADS + h)),
                  pl.BlockSpec((None, 3, blk, blk), lambda b, h, i: (h, 0, 0, 0)),
                  pl.BlockSpec((4, DIFF_HEAD_DIM), lambda b, h, i: (0, 0)),
                  pl.BlockSpec((1, h2d), lambda b, h, i: (0, 0))],
        out_specs=pl.BlockSpec((blk, h2d), lambda b, h, i: (b * nq + i, h)),
        out_shape=jax.ShapeDtypeStruct((batch * seq, DIFF_HEADS * h2d), BF16),
        scratch_shapes=[pltpu.VMEM((2, blk, 128), F32), pltpu.VMEM((2, blk, 128), F32),
                        pltpu.VMEM((2, blk, h2d), F32)],
        compiler_params=_params(3, 48),
    )(qkv, qkv, qkv, bias_tiles, diff_lambda, subln.reshape(1, h2d))


def _indexer_kernel(qi_ref, kfull_ref, wblk_ref, g_ref, b_ref, o_ref, ki_scr, key_scr, wb_scr, thr_scr,
                    *, blk, seq, k_sel):
    i = pl.program_id(1)

    @pl.when(i == 0)
    def _():
        kr = kfull_ref[:, 0:IDX_HEAD_DIM]
        ki_scr[...] = _layer_norm(kr, g_ref[...], b_ref[...]).astype(BF16)

    w = wblk_ref[:, IDX_HEAD_DIM:IDX_HEAD_DIM + IDX_HEADS] * (IDX_HEADS ** -0.5 * IDX_HEAD_DIM ** -0.5)
    for h in range(IDX_HEADS):
        wb_scr[h] = jnp.broadcast_to(w[:, h:h + 1], (blk, blk))
    qi = qi_ref[...].astype(BF16)
    row = i * blk + lax.broadcasted_iota(jnp.int32, (blk, blk), 0)
    col = lax.broadcasted_iota(jnp.int32, (blk, blk), 1)

    def score_tile(j, carry):
        start = pl.multiple_of(j * blk, blk)
        kc = ki_scr[pl.ds(start, blk), :]
        score = jnp.zeros((blk, blk), F32)
        for h in range(IDX_HEADS):
            lg = lax.dot_general(qi[:, h * IDX_HEAD_DIM:(h + 1) * IDX_HEAD_DIM], kc, NT_DIMS,
                                 preferred_element_type=F32)
            score = score + jnp.maximum(lg, 0.0) * wb_scr[h]
        bits = lax.bitcast_convert_type(score, jnp.int32)
        key = bits ^ ((bits >> 31) & jnp.int32(0x7FFFFFFF))
        key_scr[j] = jnp.where(start + col <= row, key, INT_MIN)
        return carry

    lax.fori_loop(0, i + 1, score_tile, 0)

    def make_search(n_tiles):
        def search(it, thr):
            trial = thr + lax.shift_left(jnp.int32(1), 31 - it)
            c = jnp.zeros((blk, 128), F32)
            for j in range(n_tiles):
                ge = jnp.where(key_scr[j] >= trial, 1.0, 0.0)
                for part in range(blk // 128):
                    c = c + ge[:, part * 128:(part + 1) * 128]
            return jnp.where(jnp.sum(c, axis=1, keepdims=True) >= k_sel, trial, thr)
        return search

    thr_scr[...] = jnp.full(thr_scr.shape, INT_MIN, jnp.int32)
    for jj in range(seq // blk):
        if (jj + 1) * blk > k_sel:
            @pl.when(i == jj)
            def _():
                t = lax.fori_loop(0, 32, make_search(jj + 1), jnp.full((blk, 1), INT_MIN, jnp.int32))
                thr_scr[...] = jnp.broadcast_to(t, thr_scr.shape)

    thr = thr_scr[...][:, 0:1]
    for jj in range(seq // blk):
        @pl.when(jj <= i)
        def _():
            keep = (jj * blk + col <= row) & (key_scr[jj] >= thr)
            o_ref[jj] = jnp.where(keep, 0.0, NEG).astype(o_ref.dtype)

        @pl.when(jj > i)
        def _():
            o_ref[jj] = jnp.full((blk, blk), NEG, o_ref.dtype)


def _indexer_mask(qi, kiw, idx_k_g, idx_k_b, batch, seq, k_sel, qi_col_block):
    blk = min(ATT_BLOCK, seq)
    nq = seq // blk
    qw = IDX_HEADS * IDX_HEAD_DIM
    kern = functools.partial(_indexer_kernel, blk=blk, seq=seq, k_sel=k_sel)
    return pl.pallas_call(
        kern,
        grid=(batch, nq),
        in_specs=[pl.BlockSpec((blk, qw), lambda b, i: (b * nq + i, qi_col_block)),
                  pl.BlockSpec((seq, 128), lambda b, i: (b, 0)),
                  pl.BlockSpec((blk, 128), lambda b, i: (b * nq + i, 0)),
                  pl.BlockSpec((1, IDX_HEAD_DIM), lambda b, i: (0, 0)),
                  pl.BlockSpec((1, IDX_HEAD_DIM), lambda b, i: (0, 0))],
        out_specs=pl.BlockSpec((None, nq, blk, blk), lambda b, i: (b, 0, i, 0)),
        out_shape=jax.ShapeDtypeStruct((batch, nq, seq, blk), BF16),
        scratch_shapes=[pltpu.VMEM((seq, IDX_HEAD_DIM), BF16), pltpu.VMEM((nq, blk, blk), jnp.int32),
                        pltpu.VMEM((IDX_HEADS, blk, blk), F32), pltpu.VMEM((blk, 128), jnp.int32)],
        compiler_params=_params(2, 48),
    )(qi, kiw, kiw, idx_k_g.reshape(1, -1), idx_k_b.reshape(1, -1))


def _dsa_attn_kernel(qs_ref, ckv_ref, g_ref, sel_ref, wuk_ref, wuv_ref, bias_ref, o_ref,
                     c_scr, ql_scr, sb_scr, m_scr, l_scr, acc_scr, *, blk):
    i = pl.program_id(1)
    d = DSA_HEAD_DIM
    scale = d ** -0.5

    @pl.when(i == 0)
    def _():
        x = ckv_ref[...]
        c = x * lax.rsqrt(jnp.mean(x * x, axis=1, keepdims=True) + LN_EPS) * g_ref[...]
        c_scr[...] = c.astype(BF16)

    qs = qs_ref[...]
    for h in range(DSA_HEADS):
        ql = jnp.dot(qs[:, h * d:(h + 1) * d], wuk_ref[h], preferred_element_type=F32)
        ql_scr[h * blk:(h + 1) * blk, :] = (ql * scale).astype(BF16)

    m_scr[...] = jnp.full(m_scr.shape, NEG, F32)
    l_scr[...] = jnp.zeros(l_scr.shape, F32)
    acc_scr[...] = jnp.zeros(acc_scr.shape, F32)

    def body(j, carry):
        start = pl.multiple_of(j * blk, blk)
        cc = c_scr[pl.ds(start, blk), :]
        s = lax.dot_general(ql_scr[...], cc, NT_DIMS, preferred_element_type=F32)
        selm = sel_ref[j].astype(F32)
        back = jnp.minimum(i - j, 2)
        for h in range(DSA_HEADS):
            sb_scr[h * blk:(h + 1) * blk, :] = bias_ref[h, back] + selm
        _online_softmax_step(s + sb_scr[...], cc, m_scr, l_scr, acc_scr)
        return carry

    lax.fori_loop(0, i + 1, body, 0)

    o_lat = (acc_scr[...] * _lane_repeat(1.0 / l_scr[...], DSA_KV_RANK // 128, axis=1)).astype(BF16)
    for h in range(DSA_HEADS):
        o_ref[:, h * d:(h + 1) * d] = jnp.dot(o_lat[h * blk:(h + 1) * blk], wuv_ref[h],
                                              preferred_element_type=F32).astype(o_ref.dtype)


def _dsa_attention(qkv, qs_col_block, ckv, ckv_col_block, kv_norm_g, sel, w_uk, w_uv, bias_tiles, batch, seq):
    blk = min(ATT_BLOCK, seq)
    nq = seq // blk
    hd = DSA_HEADS * DSA_HEAD_DIM
    r = DSA_KV_RANK
    rows = DSA_HEADS * blk
    kern = functools.partial(_dsa_attn_kernel, blk=blk)
    return pl.pallas_call(
        kern,
        grid=(batch, nq),
        in_specs=[pl.BlockSpec((blk, hd), lambda b, i: (b * nq + i, qs_col_block)),
                  pl.BlockSpec((seq, r), lambda b, i: (b, ckv_col_block)),
                  pl.BlockSpec((1, r), lambda b, i: (0, 0)),
                  pl.BlockSpec((None, nq, blk, blk), lambda b, i: (b, 0, i, 0)),
                  pl.BlockSpec((DSA_HEADS, DSA_HEAD_DIM, r), lambda b, i: (0, 0, 0)),
                  pl.BlockSpec((DSA_HEADS, r, DSA_HEAD_DIM), lambda b, i: (0, 0, 0)),
                  pl.BlockSpec((DSA_HEADS, 3, blk, blk), lambda b, i: (0, 0, 0, 0))],
        out_specs=pl.BlockSpec((blk, hd), lambda b, i: (b * nq + i, 0)),
        out_shape=jax.ShapeDtypeStruct((batch * seq, hd), BF16),
        scratch_shapes=[pltpu.VMEM((seq, r), BF16), pltpu.VMEM((rows, r), BF16), pltpu.VMEM((rows, blk), F32),
                        pltpu.VMEM((rows, 128), F32), pltpu.VMEM((rows, 128), F32), pltpu.VMEM((rows, r), F32)],
        compiler_params=_params(2, 56),
    )(qkv, ckv, kv_norm_g.reshape(1, r), sel, w_uk, w_uv, bias_tiles)


def _merge_kernel(x_ref, od_ref, os_ref, wgd_ref, wgs_ref, wpd_ref, wps_ref, o_ref):
    x = x_ref[...]
    gd = jnp.dot(x, wgd_ref[...], preferred_element_type=F32)
    gs = jnp.dot(x, wgs_ref[...], preferred_element_type=F32)
    pd = jnp.dot(od_ref[...], wpd_ref[...], preferred_element_type=F32)
    ps = jnp.dot(os_ref[...], wps_ref[...], preferred_element_type=F32)
    o_ref[...] = (_sigmoid(gd) * pd + _sigmoid(gs) * ps).astype(o_ref.dtype)


def _gated_merge(xb, o_d, o_s, wgd, wgs, wpd, wps):
    m, dm = xb.shape
    tm = min(512, m)
    tn = 512
    kd, ks = o_d.shape[1], o_s.shape[1]
    return pl.pallas_call(
        _merge_kernel,
        grid=(dm // tn, m // tm),
        in_specs=[pl.BlockSpec((tm, dm), lambda j, i: (i, 0)),
                  pl.BlockSpec((tm, kd), lambda j, i: (i, 0)),
                  pl.BlockSpec((tm, ks), lambda j, i: (i, 0)),
                  pl.BlockSpec((dm, tn), lambda j, i: (0, j)),
                  pl.BlockSpec((dm, tn), lambda j, i: (0, j)),
                  pl.BlockSpec((kd, tn), lambda j, i: (0, j)),
                  pl.BlockSpec((ks, tn), lambda j, i: (0, j))],
        out_specs=pl.BlockSpec((tm, tn), lambda j, i: (i, j)),
        out_shape=jax.ShapeDtypeStruct((m, dm), BF16),
        compiler_params=_params(2, 40),
    )(xb, o_d, o_s, wgd, wgs, wpd, wps)


def _mm_res_ln_kernel(a_ref, w_ref, res_ref, g_ref, b_ref, o_ref, ob_ref):
    y = jnp.dot(a_ref[...], w_ref[...], preferred_element_type=F32)
    zn = _layer_norm(DN_ALPHA * res_ref[...] + y, g_ref[...], b_ref[...])
    o_ref[...] = zn
    ob_ref[...] = zn.astype(BF16)


def _matmul_res_ln(a, w, res, g, b):
    m, k = a.shape
    n = w.shape[1]
    tm = min(256, m)
    return pl.pallas_call(
        _mm_res_ln_kernel,
        grid=(m // tm,),
        in_specs=[pl.BlockSpec((tm, k), lambda i: (i, 0)),
                  pl.BlockSpec((k, n), lambda i: (0, 0)),
                  pl.BlockSpec((tm, n), lambda i: (i, 0)),
                  pl.BlockSpec((1, n), lambda i: (0, 0)),
                  pl.BlockSpec((1, n), lambda i: (0, 0))],
        out_specs=[pl.BlockSpec((tm, n), lambda i: (i, 0)), pl.BlockSpec((tm, n), lambda i: (i, 0))],
        out_shape=[jax.ShapeDtypeStruct((m, n), F32), jax.ShapeDtypeStruct((m, n), BF16)],
        compiler_params=_params(1, 40),
    )(a, w, res, g.reshape(1, n), b.reshape(1, n))


def _pack_bf16_pairs(zb):
    half = zb.shape[1] // 2
    bits = lax.bitcast_convert_type(zb.astype(F32), jnp.uint32)
    return (bits[:, :half] & jnp.uint32(0xFFFF0000)) | (bits[:, half:] >> 16)


def _unpack_bf16_pairs(w):
    hi = lax.bitcast_convert_type(w & jnp.uint32(0xFFFF0000), F32).astype(BF16)
    lo = lax.bitcast_convert_type(w << 16, F32).astype(BF16)
    return hi, lo


def _xattn_kernel(hb_ref, h_ref, wq_ref, kv_ref, wo_ref, g_ref, b_ref, o_ref, ob_ref, op_ref):
    d = XATTN_HEAD_DIM
    hd = XATTN_HEADS * d
    scale = d ** -0.5
    qb = jnp.dot(hb_ref[...], wq_ref[...], preferred_element_type=F32).astype(BF16)
    kv = kv_ref[...]
    outs = []
    for h in range(XATTN_HEADS):
        s = lax.dot_general(qb[:, h * d:(h + 1) * d], kv[:, h * d:(h + 1) * d], NT_DIMS,
                            preferred_element_type=F32) * scale
        p = jnp.exp(s - jnp.max(s, axis=1, keepdims=True))
        l = jnp.sum(p, axis=1, keepdims=True)
        oh = jnp.dot(p.astype(BF16), kv[:, hd + h * d:hd + (h + 1) * d], preferred_element_type=F32)
        outs.append((oh / l).astype(BF16))
    o = jnp.concatenate(outs, axis=1)
    y = jnp.dot(o, wo_ref[...], preferred_element_type=F32)
    zn = _layer_norm(DN_ALPHA * h_ref[...] + y, g_ref[...], b_ref[...])
    zb = zn.astype(BF16)
    o_ref[...] = zn
    ob_ref[...] = zb
    op_ref[...] = _pack_bf16_pairs(zb)


def _cross_attention(hb, h, wq, kv, wo, g, b, batch, seq, n_mem):
    m, dm = h.shape
    tm = min(256, seq)
    nq = seq // tm
    hd = XATTN_HEADS * XATTN_HEAD_DIM
    return pl.pallas_call(
        _xattn_kernel,
        grid=(batch, nq),
        in_specs=[pl.BlockSpec((tm, dm), lambda bb, i: (bb * nq + i, 0)),
                  pl.BlockSpec((tm, dm), lambda bb, i: (bb * nq + i, 0)),
                  pl.BlockSpec((dm, hd), lambda bb, i: (0, 0)),
                  pl.BlockSpec((n_mem, 2 * hd), lambda bb, i: (bb, 0)),
                  pl.BlockSpec((hd, dm), lambda bb, i: (0, 0)),
                  pl.BlockSpec((1, dm), lambda bb, i: (0, 0)),
                  pl.BlockSpec((1, dm), lambda bb, i: (0, 0))],
        out_specs=[pl.BlockSpec((tm, dm), lambda bb, i: (bb * nq + i, 0)),
                   pl.BlockSpec((tm, dm), lambda bb, i: (bb * nq + i, 0)),
                   pl.BlockSpec((tm, dm // 2), lambda bb, i: (bb * nq + i, 0))],
        out_shape=[jax.ShapeDtypeStruct((m, dm), F32), jax.ShapeDtypeStruct((m, dm), BF16),
                   jax.ShapeDtypeStruct((m, dm // 2), jnp.uint32)],
        compiler_params=_params(2, 40),
    )(hb, h, wq, kv, wo, g.reshape(1, dm), b.reshape(1, dm))


def _split_bf16(x):
    hi = x.astype(BF16)
    return hi, (x - hi.astype(F32)).astype(BF16)


def _router_kernel(h_ref, rwt_ref, rb_ref, idx_ref, gate_ref, rank_ref, cnt_ref, cnt_scr):
    tm = h_ref.shape[0]
    per = N_EXPERTS // N_GROUPS
    h_hi, h_lo = _split_bf16(h_ref[...])
    w_hi, w_lo = _split_bf16(rwt_ref[...])
    logits = (lax.dot_general(w_hi, h_hi, NT_DIMS, preferred_element_type=F32)
              + lax.dot_general(w_hi, h_lo, NT_DIMS, preferred_element_type=F32)
              + lax.dot_general(w_lo, h_hi, NT_DIMS, preferred_element_type=F32))
    scores = _sigmoid(logits)
    sel = scores + rb_ref[...]

    sel3 = sel.reshape(N_GROUPS, per, tm)
    r_iota = lax.broadcasted_iota(jnp.int32, sel3.shape, 1).astype(F32)
    m1 = jnp.max(sel3, axis=1, keepdims=True)
    first = jnp.min(jnp.where(sel3 == m1, r_iota, float(per)), axis=1, keepdims=True)
    m2 = jnp.max(jnp.where(r_iota == first, -jnp.inf, sel3), axis=1, keepdims=True)
    gscore = (m1 + m2).reshape(N_GROUPS, tm)

    g_iota = lax.broadcasted_iota(jnp.int32, gscore.shape, 0).astype(F32)
    keep = jnp.zeros(gscore.shape, F32)
    cur = gscore
    for _ in range(TOPK_GROUPS):
        m = jnp.max(cur, axis=0, keepdims=True)
        first = jnp.min(jnp.where(cur == m, g_iota, float(N_GROUPS)), axis=0, keepdims=True)
        pick = g_iota == first
        keep = jnp.where(pick, 1.0, keep)
        cur = jnp.where(pick, -jnp.inf, cur)
    keep3 = jnp.broadcast_to(keep.reshape(N_GROUPS, 1, tm), sel3.shape)
    cur = jnp.where(keep3 > 0.5, sel3, -jnp.inf).reshape(N_EXPERTS, tm)

    e_iota = lax.broadcasted_iota(jnp.int32, cur.shape, 0).astype(F32)
    gsum = jnp.zeros((1, tm), F32)
    gates, picks = [], []
    for k in range(TOP_K):
        m = jnp.max(cur, axis=0, keepdims=True)
        first = jnp.min(jnp.where(cur == m, e_iota, float(N_EXPERTS)), axis=0, keepdims=True)
        pick = e_iota == first
        gk = jnp.sum(jnp.where(pick, scores, 0.0), axis=0, keepdims=True)
        idx_ref[k:k + 1, :] = first.astype(jnp.int32)
        gates.append(gk)
        picks.append(pick)
        gsum = gsum + gk
        cur = jnp.where(pick, -jnp.inf, cur)
    for k in range(TOP_K):
        gate_ref[k:k + 1, :] = gates[k] / gsum * ROUTED_SCALE

    @pl.when(pl.program_id(0) == 0)
    def _():
        cnt_scr[...] = jnp.zeros(cnt_scr.shape, F32)

    mask = jnp.zeros(cur.shape, F32)
    for k in range(TOP_K):
        mask = jnp.where(picks[k], 1.0, mask)
    upper = (lax.broadcasted_iota(jnp.int32, (tm, tm), 0) <= lax.broadcasted_iota(jnp.int32, (tm, tm), 1))
    cum = jnp.dot(mask.astype(BF16), jnp.where(upper, 1.0, 0.0).astype(BF16), preferred_element_type=F32)
    before = cnt_scr[...][:, 0:1] + cum - mask
    for k in range(TOP_K):
        rank_ref[k:k + 1, :] = jnp.sum(jnp.where(picks[k], before, 0.0), axis=0, keepdims=True).astype(jnp.int32)
    cnt_scr[...] = cnt_scr[...] + jnp.sum(mask, axis=1, keepdims=True)
    cnt_ref[...] = cnt_scr[...].astype(jnp.int32)


def _router(h, router_w, router_bias):
    m, dm = h.shape
    tm = min(512, m)
    return pl.pallas_call(
        _router_kernel,
        grid=(m // tm,),
        in_specs=[pl.BlockSpec((tm, dm), lambda i: (i, 0)),
                  pl.BlockSpec((N_EXPERTS, dm), lambda i: (0, 0)),
                  pl.BlockSpec((N_EXPERTS, 1), lambda i: (0, 0))],
        out_specs=[pl.BlockSpec((TOP_K, tm), lambda i: (0, i)), pl.BlockSpec((TOP_K, tm), lambda i: (0, i)),
                   pl.BlockSpec((TOP_K, tm), lambda i: (0, i)), pl.BlockSpec((N_EXPERTS, 128), lambda i: (0, 0))],
        out_shape=[jax.ShapeDtypeStruct((TOP_K, m), jnp.int32), jax.ShapeDtypeStruct((TOP_K, m), F32),
                   jax.ShapeDtypeStruct((TOP_K, m), jnp.int32), jax.ShapeDtypeStruct((N_EXPERTS, 128), jnp.int32)],
        scratch_shapes=[pltpu.VMEM((N_EXPERTS, 128), F32)],
        compiler_params=_params(1, 32),
    )(h, router_w.T, router_bias.reshape(N_EXPERTS, 1))


def _dest_kernel(idx_ref, rank_ref, ps_ref, dest_ref):
    tm = idx_ref.shape[1]
    e_iota = lax.broadcasted_iota(jnp.int32, (N_EXPERTS, tm), 0)
    ps = ps_ref[...][:, 0:1]
    for k in range(TOP_K):
        base = jnp.sum(jnp.where(e_iota == idx_ref[k:k + 1, :], ps, 0), axis=0, keepdims=True)
        dest_ref[k:k + 1, :] = rank_ref[k:k + 1, :] + base


def _dest_rows(idx_t, rank_t, pad_start):
    m = idx_t.shape[1]
    tm = min(2048, m)
    ps = jnp.broadcast_to(pad_start.astype(F32).reshape(N_EXPERTS, 1), (N_EXPERTS, 128))
    out = pl.pallas_call(
        _dest_kernel,
        grid=(m // tm,),
        in_specs=[pl.BlockSpec((TOP_K, tm), lambda i: (0, i)), pl.BlockSpec((TOP_K, tm), lambda i: (0, i)),
                  pl.BlockSpec((N_EXPERTS, 128), lambda i: (0, 0))],
        out_specs=pl.BlockSpec((TOP_K, tm), lambda i: (0, i)),
        out_shape=jax.ShapeDtypeStruct((TOP_K, m), F32),
        compiler_params=_params(1, 32),
    )(idx_t, rank_t.astype(F32), ps)
    return out.astype(jnp.int32)


def _dispatch_kernel(dest_ref, x_ref, init_ref, xs_ref, sem):
    del init_ref
    tm = x_ref.shape[0]

    def copy(t, k):
        return pltpu.make_async_copy(x_ref.at[pl.ds(t, 1), :], xs_ref.at[pl.ds(dest_ref[0, k, t], 1), :], sem)

    def start(t, carry):
        for k in range(TOP_K):
            copy(t, k).start()
        return carry

    lax.fori_loop(0, tm, start, 0)
    n_rows = TOP_K * tm
    pltpu.make_async_copy(xs_ref.at[pl.ds(0, n_rows), :], xs_ref.at[pl.ds(n_rows, n_rows), :], sem).wait()


def _dispatch(xp, dest, rows):
    m, wd = xp.shape
    tm = min(512, m)
    dest3 = dest.reshape(TOP_K, m // tm, tm).transpose(1, 0, 2)
    return pl.pallas_call(
        _dispatch_kernel,
        grid=(m // tm,),
        in_specs=[pl.BlockSpec((1, TOP_K, tm), lambda i: (i, 0, 0), memory_space=pltpu.SMEM),
                  pl.BlockSpec((tm, wd), lambda i: (i, 0)),
                  pl.BlockSpec(memory_space=pl.ANY)],
        out_specs=pl.BlockSpec(memory_space=pl.ANY),
        out_shape=jax.ShapeDtypeStruct((rows, wd), xp.dtype),
        scratch_shapes=[pltpu.SemaphoreType.DMA(())],
        input_output_aliases={2: 0},
        compiler_params=_params(1, 32),
    )(dest3, xp, jnp.zeros((rows, wd), xp.dtype))


def _expert_kernel(be_ref, slot_ref, next_ref, nu_ref, x_ref, w1_ref, w3_ref, w2_ref, o_ref,
                   w1f, w3f, w2f, w1b, w3b, w2b, sems):
    i = pl.program_id(0)
    used = i < nu_ref[0]
    e = be_ref[i]
    first = jnp.logical_and(used, jnp.logical_or(i == 0, e != be_ref[jnp.maximum(i - 1, 0)]))

    def copies(expert, s):
        return (pltpu.make_async_copy(w1_ref.at[expert], w1f.at[s], sems.at[s, 0]),
                pltpu.make_async_copy(w3_ref.at[expert], w3f.at[s], sems.at[s, 1]),
                pltpu.make_async_copy(w2_ref.at[expert], w2f.at[s], sems.at[s, 2]))

    @pl.when(jnp.logical_and(used, i == 0))
    def _():
        for c in copies(e, slot_ref[0]):
            c.start(priority=1)

    @pl.when(first)
    def _():
        s = slot_ref[i]
        for c in copies(e, s):
            c.wait()
        w1b[...] = w1f[s].astype(BF16)
        w3b[...] = w3f[s].astype(BF16)
        w2b[...] = w2f[s].astype(BF16)

        @pl.when(next_ref[i] >= 0)
        def _():
            for c in copies(next_ref[i], 1 - s):
                c.start(priority=1)

    @pl.when(used)
    def _():
        xh, xl = _unpack_bf16_pairs(x_ref[...])
        half = xh.shape[1]
        a = (jnp.dot(xh, w1b[0:half, :], preferred_element_type=F32)
             + jnp.dot(xl, w1b[half:, :], preferred_element_type=F32))
        b = (jnp.dot(xh, w3b[0:half, :], preferred_element_type=F32)
             + jnp.dot(xl, w3b[half:, :], preferred_element_type=F32))
        hmid = (a * _sigmoid(a) * b).astype(BF16)
        o_ref[...] = _pack_bf16_pairs(jnp.dot(hmid, w2b[...], preferred_element_type=F32).astype(BF16))

    @pl.when(i >= nu_ref[0])
    def _():
        o_ref[...] = jnp.zeros(o_ref.shape, o_ref.dtype)


def _expert_blocks(x_sorted, block_e, block_slot, block_next, n_used, w1, w3, w2):
    rows, wd = x_sorted.shape
    n_blocks = rows // MOE_ROWS
    dm, de = w1.shape[1], w1.shape[2]
    grid_spec = pltpu.PrefetchScalarGridSpec(
        num_scalar_prefetch=4,
        grid=(n_blocks,),
        in_specs=[pl.BlockSpec((MOE_ROWS, wd), lambda i, *_: (i, 0)),
                  pl.BlockSpec(memory_space=pl.ANY),
                  pl.BlockSpec(memory_space=pl.ANY),
                  pl.BlockSpec(memory_space=pl.ANY)],
        out_specs=pl.BlockSpec((MOE_ROWS, wd), lambda i, *_: (i, 0)),
        scratch_shapes=[pltpu.VMEM((2, dm, de), F32), pltpu.VMEM((2, dm, de), F32), pltpu.VMEM((2, de, dm), F32),
                        pltpu.VMEM((dm, de), BF16), pltpu.VMEM((dm, de), BF16), pltpu.VMEM((de, dm), BF16),
                        pltpu.SemaphoreType.DMA((2, 3))],
    )
    return pl.pallas_call(
        _expert_kernel,
        grid_spec=grid_spec,
        out_shape=jax.ShapeDtypeStruct((rows, wd), jnp.uint32),
        compiler_params=_params(1, 56),
    )(block_e, block_slot, block_next, n_used, x_sorted, w1, w3, w2)


def _shared_kernel(x_ref, w1_ref, w3_ref, w2_ref, o_ref):
    x = x_ref[...]
    a = jnp.dot(x, w1_ref[...], preferred_element_type=F32)
    b = jnp.dot(x, w3_ref[...], preferred_element_type=F32)
    hmid = (a * _sigmoid(a) * b).astype(BF16)
    o_ref[...] = jnp.dot(hmid, w2_ref[...], preferred_element_type=F32)


def _shared_expert(xb, w1, w3, w2):
    m, dm = xb.shape
    de = w1.shape[1]
    tm = min(512, m)
    return pl.pallas_call(
        _shared_kernel,
        grid=(m // tm,),
        in_specs=[pl.BlockSpec((tm, dm), lambda i: (i, 0)),
                  pl.BlockSpec((dm, de), lambda i: (0, 0)),
                  pl.BlockSpec((dm, de), lambda i: (0, 0)),
                  pl.BlockSpec((de, dm), lambda i: (0, 0))],
        out_specs=pl.BlockSpec((tm, dm), lambda i: (i, 0)),
        out_shape=jax.ShapeDtypeStruct((m, dm), F32),
        compiler_params=_params(1, 32),
    )(xb, w1, w3, w2)


def _combine_kernel(dcur_ref, dnext_ref, h_ref, sh_ref, gate_ref, g_ref, b_ref, y_ref, o_ref, ybuf, sems):
    i = pl.program_id(0)
    n = pl.num_programs(0)
    tm = h_ref.shape[0]
    slot = lax.rem(i, 2)

    def copy(d_ref, s, t, k):
        return pltpu.make_async_copy(y_ref.at[pl.ds(d_ref[0, k, t], 1), :],
                                     ybuf.at[s, pl.ds(k * tm + t, 1), :], sems.at[s])

    def start_tile(d_ref, s):
        def body(t, carry):
            for k in range(TOP_K):
                copy(d_ref, s, t, k).start()
            return carry
        lax.fori_loop(0, tm, body, 0)

    @pl.when(i == 0)
    def _():
        start_tile(dcur_ref, 0)

    for s in range(2):
        @pl.when(jnp.logical_and(i + 1 < n, slot == 1 - s))
        def _():
            start_tile(dnext_ref, s)

    pltpu.make_async_copy(y_ref.at[pl.ds(0, TOP_K * tm), :], ybuf.at[slot], sems.at[slot]).wait()

    gate = gate_ref[...]
    half = ybuf.shape[2]
    yh = jnp.zeros((tm, half), F32)
    yl = jnp.zeros((tm, half), F32)
    for k in range(TOP_K):
        w = ybuf[slot, k * tm:(k + 1) * tm, :]
        gk = gate[:, k:k + 1]
        yh = yh + lax.bitcast_convert_type(w & jnp.uint32(0xFFFF0000), F32) * gk
        yl = yl + lax.bitcast_convert_type(w << 16, F32) * gk
    y = sh_ref[...] + jnp.concatenate([yh, yl], axis=1)
    o_ref[...] = _layer_norm(DN_ALPHA * h_ref[...] + y, g_ref[...], b_ref[...])


def _combine_ln(h, shared, y_sorted, dest, gate, g, b):
    m, dm = h.shape
    tm = min(128, m)
    n = m // tm
    dest3 = dest.reshape(TOP_K, n, tm).transpose(1, 0, 2)
    return pl.pallas_call(
        _combine_kernel,
        grid=(n,),
        in_specs=[pl.BlockSpec((1, TOP_K, tm), lambda i: (i, 0, 0), memory_space=pltpu.SMEM),
                  pl.BlockSpec((1, TOP_K, tm), lambda i: (jnp.minimum(i + 1, n - 1), 0, 0), memory_space=pltpu.SMEM),
                  pl.BlockSpec((tm, dm), lambda i: (i, 0)),
                  pl.BlockSpec((tm, dm), lambda i: (i, 0)),
                  pl.BlockSpec((tm, TOP_K), lambda i: (i, 0)),
                  pl.BlockSpec((1, dm), lambda i: (0, 0)),
                  pl.BlockSpec((1, dm), lambda i: (0, 0)),
                  pl.BlockSpec(memory_space=pl.ANY)],
        out_specs=pl.BlockSpec((tm, dm), lambda i: (i, 0)),
        out_shape=jax.ShapeDtypeStruct((m, dm), F32),
        scratch_shapes=[pltpu.VMEM((2, TOP_K * tm, dm // 2), jnp.uint32), pltpu.SemaphoreType.DMA((2,))],
        compiler_params=_params(1, 40),
    )(dest3, dest3, h, shared, gate, g.reshape(1, dm), b.reshape(1, dm), y_sorted)


def _moe_sublayer(h, hb, hp, router_w, router_bias, w1, w3, w2, sw1, sw3, sw2, g, b):
    n_tok, dm = h.shape
    idx_t, gate_t, rank_t, counts = _router(h, router_w, router_bias)
    counts = counts[:, 0]
    padded = (counts + MOE_ROWS - 1) // MOE_ROWS * MOE_ROWS
    pad_end = jnp.cumsum(padded)
    n_blocks = (n_tok * TOP_K + N_EXPERTS * (MOE_ROWS - 1) + MOE_ROWS - 1) // MOE_ROWS
    block_start = jnp.arange(n_blocks, dtype=jnp.int32) * MOE_ROWS
    block_e = jnp.minimum(jnp.sum((pad_end[None, :] <= block_start[:, None]).astype(jnp.int32), axis=1),
                          N_EXPERTS - 1)
    n_used = (pad_end[-1] // MOE_ROWS).astype(jnp.int32).reshape(1)
    e_ids = jnp.arange(N_EXPERTS, dtype=jnp.int32)
    has_rows = counts > 0
    ordinal = jnp.cumsum(has_rows.astype(jnp.int32)) - 1
    later = jnp.where(has_rows[None, :] & (e_ids[None, :] > e_ids[:, None]), e_ids[None, :], N_EXPERTS)
    next_e = jnp.min(later, axis=1)
    next_e = jnp.where(next_e < N_EXPERTS, next_e, -1).astype(jnp.int32)
    block_slot = (ordinal[block_e] % 2).astype(jnp.int32)
    block_next = next_e[block_e]
    dest = _dest_rows(idx_t, rank_t, pad_end - padded)
    x_sorted = _dispatch(hp, dest, n_blocks * MOE_ROWS)
    y_sorted = _expert_blocks(x_sorted, block_e, block_slot, block_next, n_used, w1, w3, w2)
    shared = _shared_expert(hb, sw1.astype(BF16), sw3.astype(BF16), sw2.astype(BF16))
    return _combine_ln(h, shared, y_sorted, dest, gate_t.T, g, b)


def kernel(x, mem, positions, rel_bias_table, w_in, diff_lambda, diff_subln, idx_k_g, idx_k_b, kv_norm_g, w_uk, w_uv, w_proj_diff, w_proj_dsa, w_mix_out, ln1_g, ln1_b, xattn_wq, xattn_wkv, xattn_wo, ln2_g, ln2_b, router_w, router_bias, exp_w1, exp_w3, exp_w2, sh_w1, sh_w3, sh_w2, ln3_g, ln3_b):
    del positions
    batch, seq, dm = x.shape
    n_mem = mem.shape[1]
    n_tok = batch * seq
    lambda_init = 0.8 - 0.6 * math.exp(-0.3 * 0)
    blk = min(ATT_BLOCK, seq)

    dq = DIFF_HEADS * DIFF_V_DIM
    c_qs = 3 * dq
    c_ckv = c_qs + DSA_HEADS * DSA_HEAD_DIM
    c_qi = c_ckv + DSA_KV_RANK
    c_ki = c_qi + IDX_HEADS * IDX_HEAD_DIM
    c_gd = c_ki + IDX_HEAD_DIM + IDX_HEADS
    c_gs = c_gd + dm

    xf = x.reshape(n_tok, dm)
    xb = xf.astype(BF16)
    w_in0 = w_in[0]
    qkv = _matmul(xb, w_in0, 0, c_ckv, BF16, 1024, 1024)
    ckv = _matmul(xb, w_in0, c_ckv, DSA_KV_RANK, F32, 1024, 512)
    qi = _matmul(xb, w_in0, c_qi, c_ki - c_qi, BF16, 1024, 512)
    w_tail = jnp.pad(w_in0[:, c_ki:c_gd], ((0, 0), (0, 128 - (c_gd - c_ki))))
    kiw = _matmul(xb, w_tail, 0, 128, F32, 1024, 128)

    dsa_tiles = _rel_bias_tiles(rel_bias_table[:, DIFF_HEADS:], blk)
    diff_tiles = _rel_bias_tiles(rel_bias_table[:, :DIFF_HEADS], min(DIFF_BLOCK, seq))
    o_d = _diff_attention(qkv, diff_tiles, diff_lambda[0], diff_subln[0], batch, seq, lambda_init)

    k_sel = min(DSA_TOPK_MAX, seq // 4)
    sel = _indexer_mask(qi, kiw, idx_k_g[0], idx_k_b[0], batch, seq, k_sel, qi_col_block=0)
    o_s = _dsa_attention(qkv, c_qs // (DSA_HEADS * DSA_HEAD_DIM), ckv, 0, kv_norm_g[0], sel,
                         w_uk[0].astype(BF16), w_uv[0].astype(BF16), dsa_tiles, batch, seq)

    merged = _gated_merge(xb, o_d, o_s, w_in0[:, c_gd:c_gs].astype(BF16), w_in0[:, c_gs:c_gs + dm].astype(BF16),
                          w_proj_diff[0].astype(BF16), w_proj_dsa[0].astype(BF16))
    h1, h1b = _matmul_res_ln(merged, w_mix_out[0].astype(BF16), xf, ln1_g[0], ln1_b[0])

    memb = mem.reshape(batch * n_mem, dm).astype(BF16)
    kv = _matmul(memb, xattn_wkv[0], 0, 2 * XATTN_HEADS * XATTN_HEAD_DIM, BF16, 1024, 512)
    h2, h2b, h2p = _cross_attention(h1b, h1, xattn_wq[0].astype(BF16), kv, xattn_wo[0].astype(BF16),
                                    ln2_g[0], ln2_b[0], batch, seq, n_mem)

    out = _moe_sublayer(h2, h2b, h2p, router_w[0], router_bias[0], exp_w1[0], exp_w3[0], exp_w2[0],
                        sh_w1[0], sh_w3[0], sh_w2[0], ln3_g[0], ln3_b[0])
    return out.reshape(batch, seq, dm)
```

```python
import functools
import math

import jax
import jax.numpy as jnp
from jax import lax
from jax.experimental import pallas as pl
from jax.experimental.pallas import tpu as pltpu

F32 = jnp.float32
BF16 = jnp.bfloat16
NEG = -1e30
INT_MIN = -(2 ** 31)
MIB = 1024 * 1024

LN_EPS = 1e-5
DIFF_HEADS = 8
DIFF_HEAD_DIM = 128
DIFF_V_DIM = 2 * DIFF_HEAD_DIM
DSA_HEADS = 8
DSA_HEAD_DIM = 128
DSA_KV_RANK = 512
IDX_HEADS = 16
IDX_HEAD_DIM = 64
DSA_TOPK_MAX = 256
REL_BUCKETS = 32
REL_MAX_DIST = 128
XATTN_HEADS = 4
XATTN_HEAD_DIM = 128
N_EXPERTS = 64
EXPERT_DIM = 512
TOP_K = 8
N_GROUPS = 8
TOPK_GROUPS = 4
ROUTED_SCALE = 2.5
DEPTH = 1
DN_ALPHA = (2 * DEPTH) ** 0.25

ATT_BLOCK = 256
DIFF_BLOCK = 512
MOE_ROWS = 256
NT_DIMS = (((1,), (1,)), ((), ()))


def _params(n_grid, vmem_mib):
    return pltpu.CompilerParams(dimension_semantics=("arbitrary",) * n_grid,
                                vmem_limit_bytes=vmem_mib * MIB)


def _sigmoid(x):
    return 1.0 / (1.0 + jnp.exp(-x))


def _layer_norm(z, g, b):
    mu = jnp.mean(z, axis=1, keepdims=True)
    zc = z - mu
    var = jnp.mean(zc * zc, axis=1, keepdims=True)
    return zc * lax.rsqrt(var + LN_EPS) * g + b


def _mm_kernel(a_ref, b_ref, o_ref, bq_ref):
    @pl.when(pl.program_id(1) == 0)
    def _():
        bq_ref[...] = b_ref[...].astype(BF16)

    o_ref[...] = jnp.dot(a_ref[...], bq_ref[...], preferred_element_type=F32).astype(o_ref.dtype)


def _matmul(a, b, col_start, n_cols, out_dtype, tm, tn):
    m, k = a.shape
    tm = min(tm, m)
    assert col_start % tn == 0 and n_cols % tn == 0 and m % tm == 0
    off = col_start // tn
    return pl.pallas_call(
        _mm_kernel,
        grid=(n_cols // tn, m // tm),
        in_specs=[pl.BlockSpec((tm, k), lambda j, i: (i, 0)),
                  pl.BlockSpec((k, tn), lambda j, i: (0, j + off))],
        out_specs=pl.BlockSpec((tm, tn), lambda j, i: (i, j)),
        out_shape=jax.ShapeDtypeStruct((m, n_cols), out_dtype),
        scratch_shapes=[pltpu.VMEM((k, tn), BF16)],
        compiler_params=_params(2, 48),
    )(a, b)


def _rel_bucket(n):
    n = jnp.maximum(n, 0)
    max_exact = REL_BUCKETS // 2
    nf = jnp.maximum(n, 1).astype(F32)
    large = max_exact + (jnp.log(nf / max_exact) / math.log(REL_MAX_DIST / max_exact)
                         * (REL_BUCKETS - max_exact)).astype(jnp.int32)
    large = jnp.minimum(large, REL_BUCKETS - 1)
    return jnp.where(n < max_exact, n, large)


def _rel_bias_tiles(table, blk):
    assert blk >= REL_MAX_DIST
    h = table.shape[1]
    w = 2 * blk
    j = jnp.arange(w, dtype=jnp.int32)
    tiles = []
    for back in range(2):
        n = jnp.where(j < blk, back * blk - j, back * blk + w - j)
        bucket = _rel_bucket(n)
        prof = jnp.zeros((h, w), F32)
        for bkt in range(REL_BUCKETS):
            prof = jnp.where(bucket == bkt, table[bkt][:, None], prof)
        prof = jnp.where(n >= 0, prof, NEG)
        skew = jnp.tile(prof, (1, blk))[:, :blk * (w - 1)].reshape(h, blk, w - 1)
        tiles.append(skew[:, :, :blk])
    tiles.append(jnp.broadcast_to(table[REL_BUCKETS - 1][:, None, None], (h, blk, blk)))
    return jnp.stack(tiles, axis=1).astype(F32)


def _lane_repeat(x, n, axis=1):
    assert axis == 1
    return x if n == 1 else jnp.concatenate([x] * n, axis=1)


def _online_softmax_step(s, v, m_ref, l_ref, acc_ref):
    tk = s.shape[1]
    e = v.shape[1]
    m_prev = m_ref[...]
    m_next = jnp.maximum(m_prev, jnp.max(s, axis=1, keepdims=True))
    p = jnp.exp(s - _lane_repeat(m_next, tk // 128, axis=1))
    alpha = jnp.exp(m_prev - m_next)
    l_ref[...] = alpha * l_ref[...] + jnp.sum(p, axis=1, keepdims=True)
    m_ref[...] = m_next
    acc_ref[...] = (acc_ref[...] * _lane_repeat(alpha, e // 128, axis=1)
                    + jnp.dot(p.astype(BF16), v, preferred_element_type=F32))


def _diff_attn_kernel(q_ref, k_ref, v_ref, bias_ref, lam_ref, g_ref, o_ref, m_scr, l_scr, acc_scr,
                      *, blk, lambda_init):
    i = pl.program_id(2)
    d = DIFF_HEAD_DIM
    scale = d ** -0.5
    m_scr[...] = jnp.full(m_scr.shape, NEG, F32)
    l_scr[...] = jnp.zeros(l_scr.shape, F32)
    acc_scr[...] = jnp.zeros(acc_scr.shape, F32)
    q = q_ref[...]

    def body(j, carry):
        start = pl.multiple_of(j * blk, blk)
        kk = k_ref[pl.ds(start, blk), :]
        vv = v_ref[pl.ds(start, blk), :]
        bt = bias_ref[jnp.minimum(i - j, 2)]
        for c in range(2):
            s = lax.dot_general(q[:, c * d:(c + 1) * d], kk[:, c * d:(c + 1) * d], NT_DIMS,
                                preferred_element_type=F32)
            _online_softmax_step(s * scale + bt, vv, m_scr.at[c], l_scr.at[c], acc_scr.at[c])
        return carry

    lax.fori_loop(0, i + 1, body, 0)

    lp = lam_ref[...]
    lam = (jnp.exp(jnp.sum(lp[0:1] * lp[1:2], axis=1, keepdims=True))
           - jnp.exp(jnp.sum(lp[2:3] * lp[3:4], axis=1, keepdims=True)) + lambda_init)
    o0 = acc_scr[0] * _lane_repeat(1.0 / l_scr[0], 2, axis=1)
    o1 = acc_scr[1] * _lane_repeat(1.0 / l_scr[1], 2, axis=1)
    o = o0 - lam * o1
    ms = jnp.mean(o * o, axis=1, keepdims=True)
    o_ref[...] = (o * lax.rsqrt(ms + LN_EPS) * g_ref[...] * (1.0 - lambda_init)).astype(o_ref.dtype)


def _diff_attention(qkv, bias_tiles, diff_lambda, subln, batch, seq, lambda_init):
    blk = bias_tiles.shape[2]
    nq = seq // blk
    h2d = DIFF_V_DIM
    kern = functools.partial(_diff_attn_kernel, blk=blk, lambda_init=lambda_init)
    return pl.pallas_call(
        kern,
        grid=(batch, DIFF_HEADS, nq),
        in_specs=[pl.BlockSpec((blk, h2d), lambda b, h, i: (b * nq + i, h)),
                  pl.BlockSpec((seq, h2d), lambda b, h, i: (b, DIFF_HEADS + h)),
                  pl.BlockSpec((seq, h2d), lambda b, h, i: (b, 2 * DIFF_HEADS + h)),
                  pl.BlockSpec((None, 3, blk, blk), lambda b, h, i: (h, 0, 0, 0)),
                  pl.BlockSpec((4, DIFF_HEAD_DIM), lambda b, h, i: (0, 0)),
                  pl.BlockSpec((1, h2d), lambda b, h, i: (0, 0))],
        out_specs=pl.BlockSpec((blk, h2d), lambda b, h, i: (b * nq + i, h)),
        out_shape=jax.ShapeDtypeStruct((batch * seq, DIFF_HEADS * h2d), BF16),
        scratch_shapes=[pltpu.VMEM((2, blk, 128), F32), pltpu.VMEM((2, blk, 128), F32),
                        pltpu.VMEM((2, blk, h2d), F32)],
        compiler_params=_params(3, 48),
    )(qkv, qkv, qkv, bias_tiles, diff_lambda, subln.reshape(1, h2d))


def _indexer_kernel(qi_ref, kfull_ref, wblk_ref, g_ref, b_ref, o_ref, ki_scr, key_scr, wb_scr, thr_scr,
                    *, blk, seq, k_sel):
    i = pl.program_id(1)

    @pl.when(i == 0)
    def _():
        kr = kfull_ref[:, 0:IDX_HEAD_DIM]
        ki_scr[...] = _layer_norm(kr, g_ref[...], b_ref[...]).astype(BF16)

    w = wblk_ref[:, IDX_HEAD_DIM:IDX_HEAD_DIM + IDX_HEADS] * (IDX_HEADS ** -0.5 * IDX_HEAD_DIM ** -0.5)
    for h in range(IDX_HEADS):
        wb_scr[h] = jnp.broadcast_to(w[:, h:h + 1], (blk, blk))
    qi = qi_ref[...].astype(BF16)
    row = i * blk + lax.broadcasted_iota(jnp.int32, (blk, blk), 0)
    col = lax.broadcasted_iota(jnp.int32, (blk, blk), 1)

    def score_tile(j, carry):
        start = pl.multiple_of(j * blk, blk)
        kc = ki_scr[pl.ds(start, blk), :]
        score = jnp.zeros((blk, blk), F32)
        for h in range(IDX_HEADS):
            lg = lax.dot_general(qi[:, h * IDX_HEAD_DIM:(h + 1) * IDX_HEAD_DIM], kc, NT_DIMS,
                                 preferred_element_type=F32)
            score = score + jnp.maximum(lg, 0.0) * wb_scr[h]
        bits = lax.bitcast_convert_type(score, jnp.int32)
        key = bits ^ ((bits >> 31) & jnp.int32(0x7FFFFFFF))
        key_scr[j] = jnp.where(start + col <= row, key, INT_MIN)
        return carry

    lax.fori_loop(0, i + 1, score_tile, 0)

    def make_search(n_tiles):
        def search(it, thr):
            trial = thr + lax.shift_left(jnp.int32(1), 31 - it)
            c = jnp.zeros((blk, 128), F32)
            for j in range(n_tiles):
                ge = jnp.where(key_scr[j] >= trial, 1.0, 0.0)
                for part in range(blk // 128):
                    c = c + ge[:, part * 128:(part + 1) * 128]
            return jnp.where(jnp.sum(c, axis=1, keepdims=True) >= k_sel, trial, thr)
        return search

    thr_scr[...] = jnp.full(thr_scr.shape, INT_MIN, jnp.int32)
    for jj in range(seq // blk):
        if (jj + 1) * blk > k_sel:
            @pl.when(i == jj)
            def _():
                t = lax.fori_loop(0, 32, make_search(jj + 1), jnp.full((blk, 1), INT_MIN, jnp.int32))
                thr_scr[...] = jnp.broadcast_to(t, thr_scr.shape)

    thr = thr_scr[...][:, 0:1]
    for jj in range(seq // blk):
        @pl.when(jj <= i)
        def _():
            keep = (jj * blk + col <= row) & (key_scr[jj] >= thr)
            o_ref[jj] = jnp.where(keep, 0.0, NEG).astype(o_ref.dtype)

        @pl.when(jj > i)
        def _():
            o_ref[jj] = jnp.full((blk, blk), NEG, o_ref.dtype)


def _indexer_mask(qi, kiw, idx_k_g, idx_k_b, batch, seq, k_sel, qi_col_block):
    blk = min(ATT_BLOCK, seq)
    nq = seq // blk
    qw = IDX_HEADS * IDX_HEAD_DIM
    kern = functools.partial(_indexer_kernel, blk=blk, seq=seq, k_sel=k_sel)
    return pl.pallas_call(
        kern,
        grid=(batch, nq),
        in_specs=[pl.BlockSpec((blk, qw), lambda b, i: (b * nq + i, qi_col_block)),
                  pl.BlockSpec((seq, 128), lambda b, i: (b, 0)),
                  pl.BlockSpec((blk, 128), lambda b, i: (b * nq + i, 0)),
                  pl.BlockSpec((1, IDX_HEAD_DIM), lambda b, i: (0, 0)),
                  pl.BlockSpec((1, IDX_HEAD_DIM), lambda b, i: (0, 0))],
        out_specs=pl.BlockSpec((None, nq, blk, blk), lambda b, i: (b, 0, i, 0)),
        out_shape=jax.ShapeDtypeStruct((batch, nq, seq, blk), BF16),
        scratch_shapes=[pltpu.VMEM((seq, IDX_HEAD_DIM), BF16), pltpu.VMEM((nq, blk, blk), jnp.int32),
                        pltpu.VMEM((IDX_HEADS, blk, blk), F32), pltpu.VMEM((blk, 128), jnp.int32)],
        compiler_params=_params(2, 48),
    )(qi, kiw, kiw, idx_k_g.reshape(1, -1), idx_k_b.reshape(1, -1))


def _dsa_attn_kernel(qs_ref, ckv_ref, g_ref, sel_ref, wuk_ref, wuv_ref, bias_ref, o_ref,
                     c_scr, ql_scr, sb_scr, m_scr, l_scr, acc_scr, *, blk):
    i = pl.program_id(1)
    d = DSA_HEAD_DIM
    scale = d ** -0.5

    @pl.when(i == 0)
    def _():
        x = ckv_ref[...]
        c = x * lax.rsqrt(jnp.mean(x * x, axis=1, keepdims=True) + LN_EPS) * g_ref[...]
        c_scr[...] = c.astype(BF16)

    qs = qs_ref[...]
    for h in range(DSA_HEADS):
        ql = jnp.dot(qs[:, h * d:(h + 1) * d], wuk_ref[h], preferred_element_type=F32)
        ql_scr[h * blk:(h + 1) * blk, :] = (ql * scale).astype(BF16)

    m_scr[...] = jnp.full(m_scr.shape, NEG, F32)
    l_scr[...] = jnp.zeros(l_scr.shape, F32)
    acc_scr[...] = jnp.zeros(acc_scr.shape, F32)

    def body(j, carry):
        start = pl.multiple_of(j * blk, blk)
        cc = c_scr[pl.ds(start, blk), :]
        s = lax.dot_general(ql_scr[...], cc, NT_DIMS, preferred_element_type=F32)
        selm = sel_ref[j].astype(F32)
        back = jnp.minimum(i - j, 2)
        for h in range(DSA_HEADS):
            sb_scr[h * blk:(h + 1) * blk, :] = bias_ref[h, back] + selm
        _online_softmax_step(s + sb_scr[...], cc, m_scr, l_scr, acc_scr)
        return carry

    lax.fori_loop(0, i + 1, body, 0)

    o_lat = (acc_scr[...] * _lane_repeat(1.0 / l_scr[...], DSA_KV_RANK // 128, axis=1)).astype(BF16)
    for h in range(DSA_HEADS):
        o_ref[:, h * d:(h + 1) * d] = jnp.dot(o_lat[h * blk:(h + 1) * blk], wuv_ref[h],
                                              preferred_element_type=F32).astype(o_ref.dtype)


def _dsa_attention(qkv, qs_col_block, ckv, ckv_col_block, kv_norm_g, sel, w_uk, w_uv, bias_tiles, batch, seq):
    blk = min(ATT_BLOCK, seq)
    nq = seq // blk
    hd = DSA_HEADS * DSA_HEAD_DIM
    r = DSA_KV_RANK
    rows = DSA_HEADS * blk
    kern = functools.partial(_dsa_attn_kernel, blk=blk)
    return pl.pallas_call(
        kern,
        grid=(batch, nq),
        in_specs=[pl.BlockSpec((blk, hd), lambda b, i: (b * nq + i, qs_col_block)),
                  pl.BlockSpec((seq, r), lambda b, i: (b, ckv_col_block)),
                  pl.BlockSpec((1, r), lambda b, i: (0, 0)),
                  pl.BlockSpec((None, nq, blk, blk), lambda b, i: (b, 0, i, 0)),
                  pl.BlockSpec((DSA_HEADS, DSA_HEAD_DIM, r), lambda b, i: (0, 0, 0)),
                  pl.BlockSpec((DSA_HEADS, r, DSA_HEAD_DIM), lambda b, i: (0, 0, 0)),
                  pl.BlockSpec((DSA_HEADS, 3, blk, blk), lambda b, i: (0, 0, 0, 0))],
        out_specs=pl.BlockSpec((blk, hd), lambda b, i: (b * nq + i, 0)),
        out_shape=jax.ShapeDtypeStruct((batch * seq, hd), BF16),
        scratch_shapes=[pltpu.VMEM((seq, r), BF16), pltpu.VMEM((rows, r), BF16), pltpu.VMEM((rows, blk), F32),
                        pltpu.VMEM((rows, 128), F32), pltpu.VMEM((rows, 128), F32), pltpu.VMEM((rows, r), F32)],
        compiler_params=_params(2, 56),
    )(qkv, ckv, kv_norm_g.reshape(1, r), sel, w_uk, w_uv, bias_tiles)


def _merge_kernel(x_ref, od_ref, os_ref, wgd_ref, wgs_ref, wpd_ref, wps_ref, o_ref):
    x = x_ref[...]
    gd = jnp.dot(x, wgd_ref[...], preferred_element_type=F32)
    gs = jnp.dot(x, wgs_ref[...], preferred_element_type=F32)
    pd = jnp.dot(od_ref[...], wpd_ref[...], preferred_element_type=F32)
    ps = jnp.dot(os_ref[...], wps_ref[...], preferred_element_type=F32)
    o_ref[...] = (_sigmoid(gd) * pd + _sigmoid(gs) * ps).astype(o_ref.dtype)


def _gated_merge(xb, o_d, o_s, wgd, wgs, wpd, wps):
    m, dm = xb.shape
    tm = min(512, m)
    tn = 512
    kd, ks = o_d.shape[1], o_s.shape[1]
    return pl.pallas_call(
        _merge_kernel,
        grid=(dm // tn, m // tm),
        in_specs=[pl.BlockSpec((tm, dm), lambda j, i: (i, 0)),
                  pl.BlockSpec((tm, kd), lambda j, i: (i, 0)),
                  pl.BlockSpec((tm, ks), lambda j, i: (i, 0)),
                  pl.BlockSpec((dm, tn), lambda j, i: (0, j)),
                  pl.BlockSpec((dm, tn), lambda j, i: (0, j)),
                  pl.BlockSpec((kd, tn), lambda j, i: (0, j)),
                  pl.BlockSpec((ks, tn), lambda j, i: (0, j))],
        out_specs=pl.BlockSpec((tm, tn), lambda j, i: (i, j)),
        out_shape=jax.ShapeDtypeStruct((m, dm), BF16),
        compiler_params=_params(2, 40),
    )(xb, o_d, o_s, wgd, wgs, wpd, wps)


def _mm_res_ln_kernel(a_ref, w_ref, res_ref, g_ref, b_ref, o_ref, ob_ref):
    y = jnp.dot(a_ref[...], w_ref[...], preferred_element_type=F32)
    zn = _layer_norm(DN_ALPHA * res_ref[...] + y, g_ref[...], b_ref[...])
    o_ref[...] = zn
    ob_ref[...] = zn.astype(BF16)


def _matmul_res_ln(a, w, res, g, b):
    m, k = a.shape
    n = w.shape[1]
    tm = min(256, m)
    return pl.pallas_call(
        _mm_res_ln_kernel,
        grid=(m // tm,),
        in_specs=[pl.BlockSpec((tm, k), lambda i: (i, 0)),
                  pl.BlockSpec((k, n), lambda i: (0, 0)),
                  pl.BlockSpec((tm, n), lambda i: (i, 0)),
                  pl.BlockSpec((1, n), lambda i: (0, 0)),
                  pl.BlockSpec((1, n), lambda i: (0, 0))],
        out_specs=[pl.BlockSpec((tm, n), lambda i: (i, 0)), pl.BlockSpec((tm, n), lambda i: (i, 0))],
        out_shape=[jax.ShapeDtypeStruct((m, n), F32), jax.ShapeDtypeStruct((m, n), BF16)],
        compiler_params=_params(1, 40),
    )(a, w, res, g.reshape(1, n), b.reshape(1, n))


def _pack_bf16_pairs(zb):
    half = zb.shape[1] // 2
    bits = lax.bitcast_convert_type(zb.astype(F32), jnp.uint32)
    return (bits[:, :half] & jnp.uint32(0xFFFF0000)) | (bits[:, half:] >> 16)


def _unpack_bf16_pairs(w):
    hi = lax.bitcast_convert_type(w & jnp.uint32(0xFFFF0000), F32).astype(BF16)
    lo = lax.bitcast_convert_type(w << 16, F32).astype(BF16)
    return hi, lo


def _xattn_kernel(hb_ref, h_ref, wq_ref, kv_ref, wo_ref, g_ref, b_ref, o_ref, ob_ref, op_ref):
    d = XATTN_HEAD_DIM
    hd = XATTN_HEADS * d
    scale = d ** -0.5
    qb = jnp.dot(hb_ref[...], wq_ref[...], preferred_element_type=F32).astype(BF16)
    kv = kv_ref[...]
    outs = []
    for h in range(XATTN_HEADS):
        s = lax.dot_general(qb[:, h * d:(h + 1) * d], kv[:, h * d:(h + 1) * d], NT_DIMS,
                            preferred_element_type=F32) * scale
        p = jnp.exp(s - jnp.max(s, axis=1, keepdims=True))
        l = jnp.sum(p, axis=1, keepdims=True)
        oh = jnp.dot(p.astype(BF16), kv[:, hd + h * d:hd + (h + 1) * d], preferred_element_type=F32)
        outs.append((oh / l).astype(BF16))
    o = jnp.concatenate(outs, axis=1)
    y = jnp.dot(o, wo_ref[...], preferred_element_type=F32)
    zn = _layer_norm(DN_ALPHA * h_ref[...] + y, g_ref[...], b_ref[...])
    zb = zn.astype(BF16)
    o_ref[...] = zn
    ob_ref[...] = zb
    op_ref[...] = _pack_bf16_pairs(zb)


def _cross_attention(hb, h, wq, kv, wo, g, b, batch, seq, n_mem):
    m, dm = h.shape
    tm = min(256, seq)
    nq = seq // tm
    hd = XATTN_HEADS * XATTN_HEAD_DIM
    return pl.pallas_call(
        _xattn_kernel,
        grid=(batch, nq),
        in_specs=[pl.BlockSpec((tm, dm), lambda bb, i: (bb * nq + i, 0)),
                  pl.BlockSpec((tm, dm), lambda bb, i: (bb * nq + i, 0)),
                  pl.BlockSpec((dm, hd), lambda bb, i: (0, 0)),
                  pl.BlockSpec((n_mem, 2 * hd), lambda bb, i: (bb, 0)),
                  pl.BlockSpec((hd, dm), lambda bb, i: (0, 0)),
                  pl.BlockSpec((1, dm), lambda bb, i: (0, 0)),
                  pl.BlockSpec((1, dm), lambda bb, i: (0, 0))],
        out_specs=[pl.BlockSpec((tm, dm), lambda bb, i: (bb * nq + i, 0)),
                   pl.BlockSpec((tm, dm), lambda bb, i: (bb * nq + i, 0)),
                   pl.BlockSpec((tm, dm // 2), lambda bb, i: (bb * nq + i, 0))],
        out_shape=[jax.ShapeDtypeStruct((m, dm), F32), jax.ShapeDtypeStruct((m, dm), BF16),
                   jax.ShapeDtypeStruct((m, dm // 2), jnp.uint32)],
        compiler_params=_params(2, 40),
    )(hb, h, wq, kv, wo, g.reshape(1, dm), b.reshape(1, dm))


def _split_bf16(x):
    hi = x.astype(BF16)
    return hi, (x - hi.astype(F32)).astype(BF16)


def _router_kernel(h_ref, rwt_ref, rb_ref, idx_ref, gate_ref, rank_ref, cnt_ref, cnt_scr):
    tm = h_ref.shape[0]
    per = N_EXPERTS // N_GROUPS
    h_hi, h_lo = _split_bf16(h_ref[...])
    w_hi, w_lo = _split_bf16(rwt_ref[...])
    logits = (lax.dot_general(w_hi, h_hi, NT_DIMS, preferred_element_type=F32)
              + lax.dot_general(w_hi, h_lo, NT_DIMS, preferred_element_type=F32)
              + lax.dot_general(w_lo, h_hi, NT_DIMS, preferred_element_type=F32))
    scores = _sigmoid(logits)
    sel = scores + rb_ref[...]

    sel3 = sel.reshape(N_GROUPS, per, tm)
    r_iota = lax.broadcasted_iota(jnp.int32, sel3.shape, 1).astype(F32)
    m1 = jnp.max(sel3, axis=1, keepdims=True)
    first = jnp.min(jnp.where(sel3 == m1, r_iota, float(per)), axis=1, keepdims=True)
    m2 = jnp.max(jnp.where(r_iota == first, -jnp.inf, sel3), axis=1, keepdims=True)
    gscore = (m1 + m2).reshape(N_GROUPS, tm)

    g_iota = lax.broadcasted_iota(jnp.int32, gscore.shape, 0).astype(F32)
    keep = jnp.zeros(gscore.shape, F32)
    cur = gscore
    for _ in range(TOPK_GROUPS):
        m = jnp.max(cur, axis=0, keepdims=True)
        first = jnp.min(jnp.where(cur == m, g_iota, float(N_GROUPS)), axis=0, keepdims=True)
        pick = g_iota == first
        keep = jnp.where(pick, 1.0, keep)
        cur = jnp.where(pick, -jnp.inf, cur)
    keep3 = jnp.broadcast_to(keep.reshape(N_GROUPS, 1, tm), sel3.shape)
    cur = jnp.where(keep3 > 0.5, sel3, -jnp.inf).reshape(N_EXPERTS, tm)

    e_iota = lax.broadcasted_iota(jnp.int32, cur.shape, 0).astype(F32)
    gsum = jnp.zeros((1, tm), F32)
    gates, picks = [], []
    for k in range(TOP_K):
        m = jnp.max(cur, axis=0, keepdims=True)
        first = jnp.min(jnp.where(cur == m, e_iota, float(N_EXPERTS)), axis=0, keepdims=True)
        pick = e_iota == first
        gk = jnp.sum(jnp.where(pick, scores, 0.0), axis=0, keepdims=True)
        idx_ref[k:k + 1, :] = first.astype(jnp.int32)
        gates.append(gk)
        picks.append(pick)
        gsum = gsum + gk
        cur = jnp.where(pick, -jnp.inf, cur)
    for k in range(TOP_K):
        gate_ref[k:k + 1, :] = gates[k] / gsum * ROUTED_SCALE

    @pl.when(pl.program_id(0) == 0)
    def _():
        cnt_scr[...] = jnp.zeros(cnt_scr.shape, F32)

    mask = jnp.zeros(cur.shape, F32)
    for k in range(TOP_K):
        mask = jnp.where(picks[k], 1.0, mask)
    upper = (lax.broadcasted_iota(jnp.int32, (tm, tm), 0) <= lax.broadcasted_iota(jnp.int32, (tm, tm), 1))
    cum = jnp.dot(mask.astype(BF16), jnp.where(upper, 1.0, 0.0).astype(BF16), preferred_element_type=F32)
    before = cnt_scr[...][:, 0:1] + cum - mask
    for k in range(TOP_K):
        rank_ref[k:k + 1, :] = jnp.sum(jnp.where(picks[k], before, 0.0), axis=0, keepdims=True).astype(jnp.int32)
    cnt_scr[...] = cnt_scr[...] + jnp.sum(mask, axis=1, keepdims=True)
    cnt_ref[...] = cnt_scr[...].astype(jnp.int32)


def _router(h, router_w, router_bias):
    m, dm = h.shape
    tm = min(512, m)
    return pl.pallas_call(
        _router_kernel,
        grid=(m // tm,),
        in_specs=[pl.BlockSpec((tm, dm), lambda i: (i, 0)),
                  pl.BlockSpec((N_EXPERTS, dm), lambda i: (0, 0)),
                  pl.BlockSpec((N_EXPERTS, 1), lambda i: (0, 0))],
        out_specs=[pl.BlockSpec((TOP_K, tm), lambda i: (0, i)), pl.BlockSpec((TOP_K, tm), lambda i: (0, i)),
                   pl.BlockSpec((TOP_K, tm), lambda i: (0, i)), pl.BlockSpec((N_EXPERTS, 128), lambda i: (0, 0))],
        out_shape=[jax.ShapeDtypeStruct((TOP_K, m), jnp.int32), jax.ShapeDtypeStruct((TOP_K, m), F32),
                   jax.ShapeDtypeStruct((TOP_K, m), jnp.int32), jax.ShapeDtypeStruct((N_EXPERTS, 128), jnp.int32)],
        scratch_shapes=[pltpu.VMEM((N_EXPERTS, 128), F32)],
        compiler_params=_params(1, 32),
    )(h, router_w.T, router_bias.reshape(N_EXPERTS, 1))


def _dest_kernel(idx_ref, rank_ref, ps_ref, dest_ref):
    tm = idx_ref.shape[1]
    e_iota = lax.broadcasted_iota(jnp.int32, (N_EXPERTS, tm), 0)
    ps = ps_ref[...][:, 0:1]
    for k in range(TOP_K):
        base = jnp.sum(jnp.where(e_iota == idx_ref[k:k + 1, :], ps, 0), axis=0, keepdims=True)
        dest_ref[k:k + 1, :] = rank_ref[k:k + 1, :] + base


def _dest_rows(idx_t, rank_t, pad_start):
    m = idx_t.shape[1]
    tm = min(2048, m)
    ps = jnp.broadcast_to(pad_start.astype(F32).reshape(N_EXPERTS, 1), (N_EXPERTS, 128))
    out = pl.pallas_call(
        _dest_kernel,
        grid=(m // tm,),
        in_specs=[pl.BlockSpec((TOP_K, tm), lambda i: (0, i)), pl.BlockSpec((TOP_K, tm), lambda i: (0, i)),
                  pl.BlockSpec((N_EXPERTS, 128), lambda i: (0, 0))],
        out_specs=pl.BlockSpec((TOP_K, tm), lambda i: (0, i)),
        out_shape=jax.ShapeDtypeStruct((TOP_K, m), F32),
        compiler_params=_params(1, 32),
    )(idx_t, rank_t.astype(F32), ps)
    return out.astype(jnp.int32)


def _dispatch_kernel(zflag_ref, dest_ref, x_ref, xs_ref, zbuf, sem, zsem):
    tm = x_ref.shape[0]
    n_blocks = xs_ref.shape[0] // MOE_ROWS

    @pl.when(pl.program_id(0) == 0)
    def _():
        zbuf[...] = jnp.zeros(zbuf.shape, zbuf.dtype)

        def zero_copy(b):
            return pltpu.make_async_copy(zbuf, xs_ref.at[pl.ds(pl.multiple_of(b * MOE_ROWS, MOE_ROWS), MOE_ROWS), :],
                                         zsem)

        def zstart(b, carry):
            @pl.when(zflag_ref[b] != 0)
            def _():
                zero_copy(b).start()
            return carry

        def zwait(b, carry):
            @pl.when(zflag_ref[b] != 0)
            def _():
                zero_copy(b).wait()
            return carry

        lax.fori_loop(0, n_blocks, zstart, 0)
        lax.fori_loop(0, n_blocks, zwait, 0)

    def copy(t, k):
        return pltpu.make_async_copy(x_ref.at[pl.ds(t, 1), :], xs_ref.at[pl.ds(dest_ref[0, k, t], 1), :], sem)

    def start(t, carry):
        for k in range(TOP_K):
            copy(t, k).start()
        return carry

    lax.fori_loop(0, tm, start, 0)
    n_rows = TOP_K * tm
    pltpu.make_async_copy(xs_ref.at[pl.ds(0, n_rows), :], xs_ref.at[pl.ds(n_rows, n_rows), :], sem).wait()


def _dispatch(xp, dest, zero_block, rows):
    m, wd = xp.shape
    tm = min(512, m)
    dest3 = dest.reshape(TOP_K, m // tm, tm).transpose(1, 0, 2)
    grid_spec = pltpu.PrefetchScalarGridSpec(
        num_scalar_prefetch=1,
        grid=(m // tm,),
        in_specs=[pl.BlockSpec((1, TOP_K, tm), lambda i, zf: (i, 0, 0), memory_space=pltpu.SMEM),
                  pl.BlockSpec((tm, wd), lambda i, zf: (i, 0))],
        out_specs=pl.BlockSpec(memory_space=pl.ANY),
        scratch_shapes=[pltpu.VMEM((MOE_ROWS, wd), xp.dtype), pltpu.SemaphoreType.DMA(()),
                        pltpu.SemaphoreType.DMA(())],
    )
    return pl.pallas_call(
        _dispatch_kernel,
        grid_spec=grid_spec,
        out_shape=jax.ShapeDtypeStruct((rows, wd), xp.dtype),
        compiler_params=_params(1, 32),
    )(zero_block, dest3, xp)


def _expert_kernel(be_ref, slot_ref, next_ref, nu_ref, x_ref, w1_ref, w3_ref, w2_ref, o_ref,
                   w1f, w3f, w2f, w1b, w3b, w2b, sems):
    i = pl.program_id(0)
    used = i < nu_ref[0]
    e = be_ref[i]
    first = jnp.logical_and(used, jnp.logical_or(i == 0, e != be_ref[jnp.maximum(i - 1, 0)]))

    def copies(expert, s):
        return (pltpu.make_async_copy(w1_ref.at[expert], w1f.at[s], sems.at[s, 0]),
                pltpu.make_async_copy(w3_ref.at[expert], w3f.at[s], sems.at[s, 1]),
                pltpu.make_async_copy(w2_ref.at[expert], w2f.at[s], sems.at[s, 2]))

    @pl.when(jnp.logical_and(used, i == 0))
    def _():
        for c in copies(e, slot_ref[0]):
            c.start(priority=1)

    @pl.when(first)
    def _():
        s = slot_ref[i]
        for c in copies(e, s):
            c.wait()
        w1b[...] = w1f[s].astype(BF16)
        w3b[...] = w3f[s].astype(BF16)
        w2b[...] = w2f[s].astype(BF16)

        @pl.when(next_ref[i] >= 0)
        def _():
            for c in copies(next_ref[i], 1 - s):
                c.start(priority=1)

    @pl.when(used)
    def _():
        xh, xl = _unpack_bf16_pairs(x_ref[...])
        half = xh.shape[1]
        a = (jnp.dot(xh, w1b[0:half, :], preferred_element_type=F32)
             + jnp.dot(xl, w1b[half:, :], preferred_element_type=F32))
        b = (jnp.dot(xh, w3b[0:half, :], preferred_element_type=F32)
             + jnp.dot(xl, w3b[half:, :], preferred_element_type=F32))
        hmid = (a * _sigmoid(a) * b).astype(BF16)
        o_ref[...] = _pack_bf16_pairs(jnp.dot(hmid, w2b[...], preferred_element_type=F32).astype(BF16))

    @pl.when(i >= nu_ref[0])
    def _():
        o_ref[...] = jnp.zeros(o_ref.shape, o_ref.dtype)


def _expert_blocks(x_sorted, block_e, block_slot, block_next, n_used, w1, w3, w2):
    rows, wd = x_sorted.shape
    n_blocks = rows // MOE_ROWS
    dm, de = w1.shape[1], w1.shape[2]
    grid_spec = pltpu.PrefetchScalarGridSpec(
        num_scalar_prefetch=4,
        grid=(n_blocks,),
        in_specs=[pl.BlockSpec((MOE_ROWS, wd), lambda i, *_: (i, 0)),
                  pl.BlockSpec(memory_space=pl.ANY),
                  pl.BlockSpec(memory_space=pl.ANY),
                  pl.BlockSpec(memory_space=pl.ANY)],
        out_specs=pl.BlockSpec((MOE_ROWS, wd), lambda i, *_: (i, 0)),
        scratch_shapes=[pltpu.VMEM((2, dm, de), F32), pltpu.VMEM((2, dm, de), F32), pltpu.VMEM((2, de, dm), F32),
                        pltpu.VMEM((dm, de), BF16), pltpu.VMEM((dm, de), BF16), pltpu.VMEM((de, dm), BF16),
                        pltpu.SemaphoreType.DMA((2, 3))],
    )
    return pl.pallas_call(
        _expert_kernel,
        grid_spec=grid_spec,
        out_shape=jax.ShapeDtypeStruct((rows, wd), jnp.uint32),
        compiler_params=_params(1, 56),
    )(block_e, block_slot, block_next, n_used, x_sorted, w1, w3, w2)


def _shared_kernel(x_ref, w1_ref, w3_ref, w2_ref, o_ref):
    x = x_ref[...]
    a = jnp.dot(x, w1_ref[...], preferred_element_type=F32)
    b = jnp.dot(x, w3_ref[...], preferred_element_type=F32)
    hmid = (a * _sigmoid(a) * b).astype(BF16)
    o_ref[...] = jnp.dot(hmid, w2_ref[...], preferred_element_type=F32)


def _shared_expert(xb, w1, w3, w2):
    m, dm = xb.shape
    de = w1.shape[1]
    tm = min(512, m)
    return pl.pallas_call(
        _shared_kernel,
        grid=(m // tm,),
        in_specs=[pl.BlockSpec((tm, dm), lambda i: (i, 0)),
                  pl.BlockSpec((dm, de), lambda i: (0, 0)),
                  pl.BlockSpec((dm, de), lambda i: (0, 0)),
                  pl.BlockSpec((de, dm), lambda i: (0, 0))],
        out_specs=pl.BlockSpec((tm, dm), lambda i: (i, 0)),
        out_shape=jax.ShapeDtypeStruct((m, dm), F32),
        compiler_params=_params(1, 32),
    )(xb, w1, w3, w2)


def _combine_kernel(dcur_ref, dnext_ref, h_ref, sh_ref, gate_ref, g_ref, b_ref, y_ref, o_ref, ybuf, sems):
    i = pl.program_id(0)
    n = pl.num_programs(0)
    tm = h_ref.shape[0]
    slot = lax.rem(i, 2)

    def copy(d_ref, s, t, k):
        return pltpu.make_async_copy(y_ref.at[pl.ds(d_ref[0, k, t], 1), :],
                                     ybuf.at[s, pl.ds(k * tm + t, 1), :], sems.at[s])

    def start_tile(d_ref, s):
        def body(t, carry):
            for k in range(TOP_K):
                copy(d_ref, s, t, k).start()
            return carry
        lax.fori_loop(0, tm, body, 0)

    @pl.when(i == 0)
    def _():
        start_tile(dcur_ref, 0)

    for s in range(2):
        @pl.when(jnp.logical_and(i + 1 < n, slot == 1 - s))
        def _():
            start_tile(dnext_ref, s)

    pltpu.make_async_copy(y_ref.at[pl.ds(0, TOP_K * tm), :], ybuf.at[slot], sems.at[slot]).wait()

    gate = gate_ref[...]
    half = ybuf.shape[2]
    yh = jnp.zeros((tm, half), F32)
    yl = jnp.zeros((tm, half), F32)
    for k in range(TOP_K):
        w = ybuf[slot, k * tm:(k + 1) * tm, :]
        gk = gate[:, k:k + 1]
        yh = yh + lax.bitcast_convert_type(w & jnp.uint32(0xFFFF0000), F32) * gk
        yl = yl + lax.bitcast_convert_type(w << 16, F32) * gk
    y = sh_ref[...] + jnp.concatenate([yh, yl], axis=1)
    o_ref[...] = _layer_norm(DN_ALPHA * h_ref[...] + y, g_ref[...], b_ref[...])


def _combine_ln(h, shared, y_sorted, dest, gate, g, b):
    m, dm = h.shape
    tm = min(128, m)
    n = m // tm
    dest3 = dest.reshape(TOP_K, n, tm).transpose(1, 0, 2)
    return pl.pallas_call(
        _combine_kernel,
        grid=(n,),
        in_specs=[pl.BlockSpec((1, TOP_K, tm), lambda i: (i, 0, 0), memory_space=pltpu.SMEM),
                  pl.BlockSpec((1, TOP_K, tm), lambda i: (jnp.minimum(i + 1, n - 1), 0, 0), memory_space=pltpu.SMEM),
                  pl.BlockSpec((tm, dm), lambda i: (i, 0)),
                  pl.BlockSpec((tm, dm), lambda i: (i, 0)),
                  pl.BlockSpec((tm, TOP_K), lambda i: (i, 0)),
                  pl.BlockSpec((1, dm), lambda i: (0, 0)),
                  pl.BlockSpec((1, dm), lambda i: (0, 0)),
                  pl.BlockSpec(memory_space=pl.ANY)],
        out_specs=pl.BlockSpec((tm, dm), lambda i: (i, 0)),
        out_shape=jax.ShapeDtypeStruct((m, dm), F32),
        scratch_shapes=[pltpu.VMEM((2, TOP_K * tm, dm // 2), jnp.uint32), pltpu.SemaphoreType.DMA((2,))],
        compiler_params=_params(1, 40),
    )(dest3, dest3, h, shared, gate, g.reshape(1, dm), b.reshape(1, dm), y_sorted)


def _moe_sublayer(h, hb, hp, router_w, router_bias, w1, w3, w2, sw1, sw3, sw2, g, b):
    n_tok, dm = h.shape
    idx_t, gate_t, rank_t, counts = _router(h, router_w, router_bias)
    counts = counts[:, 0]
    padded = (counts + MOE_ROWS - 1) // MOE_ROWS * MOE_ROWS
    pad_end = jnp.cumsum(padded)
    n_blocks = (n_tok * TOP_K + N_EXPERTS * (MOE_ROWS - 1) + MOE_ROWS - 1) // MOE_ROWS
    block_start = jnp.arange(n_blocks, dtype=jnp.int32) * MOE_ROWS
    block_e = jnp.minimum(jnp.sum((pad_end[None, :] <= block_start[:, None]).astype(jnp.int32), axis=1),
                          N_EXPERTS - 1)
    n_used = (pad_end[-1] // MOE_ROWS).astype(jnp.int32).reshape(1)
    e_ids = jnp.arange(N_EXPERTS, dtype=jnp.int32)
    has_rows = counts > 0
    ordinal = jnp.cumsum(has_rows.astype(jnp.int32)) - 1
    later = jnp.where(has_rows[None, :] & (e_ids[None, :] > e_ids[:, None]), e_ids[None, :], N_EXPERTS)
    next_e = jnp.min(later, axis=1)
    next_e = jnp.where(next_e < N_EXPERTS, next_e, -1).astype(jnp.int32)
    block_slot = (ordinal[block_e] % 2).astype(jnp.int32)
    block_next = next_e[block_e]
    dest = _dest_rows(idx_t, rank_t, pad_end - padded)
    block_end = block_start + MOE_ROWS
    last_of_expert = jnp.any(pad_end[None, :] == block_end[:, None], axis=1)
    zero_block = (last_of_expert | (block_start >= pad_end[-1])).astype(jnp.int32)
    x_sorted = _dispatch(hp, dest, zero_block, n_blocks * MOE_ROWS)
    y_sorted = _expert_blocks(x_sorted, block_e, block_slot, block_next, n_used, w1, w3, w2)
    shared = _shared_expert(hb, sw1.astype(BF16), sw3.astype(BF16), sw2.astype(BF16))
    return _combine_ln(h, shared, y_sorted, dest, gate_t.T, g, b)


def kernel(x, mem, positions, rel_bias_table, w_in, diff_lambda, diff_subln, idx_k_g, idx_k_b, kv_norm_g, w_uk, w_uv, w_proj_diff, w_proj_dsa, w_mix_out, ln1_g, ln1_b, xattn_wq, xattn_wkv, xattn_wo, ln2_g, ln2_b, router_w, router_bias, exp_w1, exp_w3, exp_w2, sh_w1, sh_w3, sh_w2, ln3_g, ln3_b):
    del positions
    batch, seq, dm = x.shape
    n_mem = mem.shape[1]
    n_tok = batch * seq
    lambda_init = 0.8 - 0.6 * math.exp(-0.3 * 0)
    blk = min(ATT_BLOCK, seq)

    dq = DIFF_HEADS * DIFF_V_DIM
    c_qs = 3 * dq
    c_ckv = c_qs + DSA_HEADS * DSA_HEAD_DIM
    c_qi = c_ckv + DSA_KV_RANK
    c_ki = c_qi + IDX_HEADS * IDX_HEAD_DIM
    c_gd = c_ki + IDX_HEAD_DIM + IDX_HEADS
    c_gs = c_gd + dm

    xf = x.reshape(n_tok, dm)
    xb = xf.astype(BF16)
    w_in0 = w_in[0]
    qkv = _matmul(xb, w_in0, 0, c_ckv, BF16, 1024, 1024)
    ckv = _matmul(xb, w_in0, c_ckv, DSA_KV_RANK, F32, 1024, 512)
    qi = _matmul(xb, w_in0, c_qi, c_ki - c_qi, BF16, 1024, 512)
    w_tail = jnp.pad(w_in0[:, c_ki:c_gd], ((0, 0), (0, 128 - (c_gd - c_ki))))
    kiw = _matmul(xb, w_tail, 0, 128, F32, 1024, 128)

    dsa_tiles = _rel_bias_tiles(rel_bias_table[:, DIFF_HEADS:], blk)
    diff_tiles = _rel_bias_tiles(rel_bias_table[:, :DIFF_HEADS], min(DIFF_BLOCK, seq))
    o_d = _diff_attention(qkv, diff_tiles, diff_lambda[0], diff_subln[0], batch, seq, lambda_init)

    k_sel = min(DSA_TOPK_MAX, seq // 4)
    sel = _indexer_mask(qi, kiw, idx_k_g[0], idx_k_b[0], batch, seq, k_sel, qi_col_block=0)
    o_s = _dsa_attention(qkv, c_qs // (DSA_HEADS * DSA_HEAD_DIM), ckv, 0, kv_norm_g[0], sel,
                         w_uk[0].astype(BF16), w_uv[0].astype(BF16), dsa_tiles, batch, seq)

    merged = _gated_merge(xb, o_d, o_s, w_in0[:, c_gd:c_gs].astype(BF16), w_in0[:, c_gs:c_gs + dm].astype(BF16),
                          w_proj_diff[0].astype(BF16), w_proj_dsa[0].astype(BF16))
    h1, h1b = _matmul_res_ln(merged, w_mix_out[0].astype(BF16), xf, ln1_g[0], ln1_b[0])

    memb = mem.reshape(batch * n_mem, dm).astype(BF16)
    kv = _matmul(memb, xattn_wkv[0], 0, 2 * XATTN_HEADS * XATTN_HEAD_DIM, BF16, 1024, 512)
    h2, h2b, h2p = _cross_attention(h1b, h1, xattn_wq[0].astype(BF16), kv, xattn_wo[0].astype(BF16),
                                    ln2_g[0], ln2_b[0], batch, seq, n_mem)

    out = _moe_sublayer(h2, h2b, h2p, router_w[0], router_bias[0], exp_w1[0], exp_w3[0], exp_w2[0],
                        sh_w1[0], sh_w3[0], sh_w2[0], ln3_g[0], ln3_b[0])
    return out.reshape(batch, seq, dm)
```

```python
import functools
import math

import jax
import jax.numpy as jnp
from jax import lax
from jax.experimental import pallas as pl
from jax.experimental.pallas import tpu as pltpu

F32 = jnp.float32
BF16 = jnp.bfloat16
NEG = -1e30
INT_MIN = -(2 ** 31)
MIB = 1024 * 1024

LN_EPS = 1e-5
DIFF_HEADS = 8
DIFF_HEAD_DIM = 128
DIFF_V_DIM = 2 * DIFF_HEAD_DIM
DSA_HEADS = 8
DSA_HEAD_DIM = 128
DSA_KV_RANK = 512
IDX_HEADS = 16
IDX_HEAD_DIM = 64
DSA_TOPK_MAX = 256
REL_BUCKETS = 32
REL_MAX_DIST = 128
XATTN_HEADS = 4
XATTN_HEAD_DIM = 128
N_EXPERTS = 64
EXPERT_DIM = 512
TOP_K = 8
N_GROUPS = 8
TOPK_GROUPS = 4
ROUTED_SCALE = 2.5
DEPTH = 1
DN_ALPHA = (2 * DEPTH) ** 0.25

ATT_BLOCK = 256
DIFF_BLOCK = 512
MOE_ROWS = 256
NT_DIMS = (((1,), (1,)), ((), ()))


def _params(n_grid, vmem_mib):
    return pltpu.CompilerParams(dimension_semantics=("arbitrary",) * n_grid,
                                vmem_limit_bytes=vmem_mib * MIB)


def _sigmoid(x):
    return 1.0 / (1.0 + jnp.exp(-x))


def _layer_norm(z, g, b):
    mu = jnp.mean(z, axis=1, keepdims=True)
    zc = z - mu
    var = jnp.mean(zc * zc, axis=1, keepdims=True)
    return zc * lax.rsqrt(var + LN_EPS) * g + b


def _mm_kernel(a_ref, b_ref, o_ref, bq_ref):
    @pl.when(pl.program_id(1) == 0)
    def _():
        bq_ref[...] = b_ref[...].astype(BF16)

    o_ref[...] = jnp.dot(a_ref[...], bq_ref[...], preferred_element_type=F32).astype(o_ref.dtype)


def _matmul(a, b, col_start, n_cols, out_dtype, tm, tn):
    m, k = a.shape
    tm = min(tm, m)
    assert col_start % tn == 0 and n_cols % tn == 0 and m % tm == 0
    off = col_start // tn
    return pl.pallas_call(
        _mm_kernel,
        grid=(n_cols // tn, m // tm),
        in_specs=[pl.BlockSpec((tm, k), lambda j, i: (i, 0)),
                  pl.BlockSpec((k, tn), lambda j, i: (0, j + off))],
        out_specs=pl.BlockSpec((tm, tn), lambda j, i: (i, j)),
        out_shape=jax.ShapeDtypeStruct((m, n_cols), out_dtype),
        scratch_shapes=[pltpu.VMEM((k, tn), BF16)],
        compiler_params=_params(2, 48),
    )(a, b)


def _rel_bucket(n):
    n = jnp.maximum(n, 0)
    max_exact = REL_BUCKETS // 2
    nf = jnp.maximum(n, 1).astype(F32)
    large = max_exact + (jnp.log(nf / max_exact) / math.log(REL_MAX_DIST / max_exact)
                         * (REL_BUCKETS - max_exact)).astype(jnp.int32)
    large = jnp.minimum(large, REL_BUCKETS - 1)
    return jnp.where(n < max_exact, n, large)


def _rel_bias_tiles(table, blk):
    assert blk >= REL_MAX_DIST
    h = table.shape[1]
    w = 2 * blk
    j = jnp.arange(w, dtype=jnp.int32)
    tiles = []
    for back in range(2):
        n = jnp.where(j < blk, back * blk - j, back * blk + w - j)
        bucket = _rel_bucket(n)
        prof = jnp.zeros((h, w), F32)
        for bkt in range(REL_BUCKETS):
            prof = jnp.where(bucket == bkt, table[bkt][:, None], prof)
        prof = jnp.where(n >= 0, prof, NEG)
        skew = jnp.tile(prof, (1, blk))[:, :blk * (w - 1)].reshape(h, blk, w - 1)
        tiles.append(skew[:, :, :blk])
    tiles.append(jnp.broadcast_to(table[REL_BUCKETS - 1][:, None, None], (h, blk, blk)))
    return jnp.stack(tiles, axis=1).astype(F32)


def _lane_repeat(x, n, axis=1):
    assert axis == 1
    return x if n == 1 else jnp.concatenate([x] * n, axis=1)


def _online_softmax_step(s, v, m_ref, l_ref, acc_ref):
    tk = s.shape[1]
    e = v.shape[1]
    m_prev = m_ref[...]
    m_next = jnp.maximum(m_prev, jnp.max(s, axis=1, keepdims=True))
    p = jnp.exp(s - _lane_repeat(m_next, tk // 128, axis=1))
    alpha = jnp.exp(m_prev - m_next)
    l_ref[...] = alpha * l_ref[...] + jnp.sum(p, axis=1, keepdims=True)
    m_ref[...] = m_next
    acc_ref[...] = (acc_ref[...] * _lane_repeat(alpha, e // 128, axis=1)
                    + jnp.dot(p.astype(BF16), v, preferred_element_type=F32))


def _diff_attn_kernel(q_ref, k_ref, v_ref, bias_ref, lam_ref, g_ref, o_ref, m_scr, l_scr, acc_scr,
                      *, blk, lambda_init):
    i = pl.program_id(2)
    d = DIFF_HEAD_DIM
    scale = d ** -0.5
    m_scr[...] = jnp.full(m_scr.shape, NEG, F32)
    l_scr[...] = jnp.zeros(l_scr.shape, F32)
    acc_scr[...] = jnp.zeros(acc_scr.shape, F32)
    q = q_ref[...]

    def body(j, carry):
        start = pl.multiple_of(j * blk, blk)
        kk = k_ref[pl.ds(start, blk), :]
        vv = v_ref[pl.ds(start, blk), :]
        bt = bias_ref[jnp.minimum(i - j, 2)]
        for c in range(2):
            s = lax.dot_general(q[:, c * d:(c + 1) * d], kk[:, c * d:(c + 1) * d], NT_DIMS,
                                preferred_element_type=F32)
            _online_softmax_step(s * scale + bt, vv, m_scr.at[c], l_scr.at[c], acc_scr.at[c])
        return carry

    lax.fori_loop(0, i + 1, body, 0)

    lp = lam_ref[...]
    lam = (jnp.exp(jnp.sum(lp[0:1] * lp[1:2], axis=1, keepdims=True))
           - jnp.exp(jnp.sum(lp[2:3] * lp[3:4], axis=1, keepdims=True)) + lambda_init)
    o0 = acc_scr[0] * _lane_repeat(1.0 / l_scr[0], 2, axis=1)
    o1 = acc_scr[1] * _lane_repeat(1.0 / l_scr[1], 2, axis=1)
    o = o0 - lam * o1
    ms = jnp.mean(o * o, axis=1, keepdims=True)
    o_ref[...] = (o * lax.rsqrt(ms + LN_EPS) * g_ref[...] * (1.0 - lambda_init)).astype(o_ref.dtype)


def _diff_attention(qkv, bias_tiles, diff_lambda, subln, batch, seq, lambda_init):
    blk = bias_tiles.shape[2]
    nq = seq // blk
    h2d = DIFF_V_DIM
    kern = functools.partial(_diff_attn_kernel, blk=blk, lambda_init=lambda_init)
    return pl.pallas_call(
        kern,
        grid=(batch, DIFF_HEADS, nq),
        in_specs=[pl.BlockSpec((blk, h2d), lambda b, h, i: (b * nq + i, h)),
                  pl.BlockSpec((seq, h2d), lambda b, h, i: (b, DIFF_HEADS + h)),
                  pl.BlockSpec((seq, h2d), lambda b, h, i: (b, 2 * DIFF_HEADS + h)),
                  pl.BlockSpec((None, 3, blk, blk), lambda b, h, i: (h, 0, 0, 0)),
                  pl.BlockSpec((4, DIFF_HEAD_DIM), lambda b, h, i: (0, 0)),
                  pl.BlockSpec((1, h2d), lambda b, h, i: (0, 0))],
        out_specs=pl.BlockSpec((blk, h2d), lambda b, h, i: (b * nq + i, h)),
        out_shape=jax.ShapeDtypeStruct((batch * seq, DIFF_HEADS * h2d), BF16),
        scratch_shapes=[pltpu.VMEM((2, blk, 128), F32), pltpu.VMEM((2, blk, 128), F32),
                        pltpu.VMEM((2, blk, h2d), F32)],
        compiler_params=_params(3, 48),
    )(qkv, qkv, qkv, bias_tiles, diff_lambda, subln.reshape(1, h2d))


def _indexer_kernel(qi_ref, kfull_ref, wblk_ref, g_ref, b_ref, o_ref, ki_scr, key_scr, wb_scr, thr_scr,
                    *, blk, seq, k_sel):
    i = pl.program_id(1)

    @pl.when(i == 0)
    def _():
        kr = kfull_ref[:, 0:IDX_HEAD_DIM]
        ki_scr[...] = _layer_norm(kr, g_ref[...], b_ref[...]).astype(BF16)

    w = wblk_ref[:, IDX_HEAD_DIM:IDX_HEAD_DIM + IDX_HEADS] * (IDX_HEADS ** -0.5 * IDX_HEAD_DIM ** -0.5)
    for h in range(IDX_HEADS):
        wb_scr[h] = jnp.broadcast_to(w[:, h:h + 1], (blk, blk))
    qi = qi_ref[...].astype(BF16)
    row = i * blk + lax.broadcasted_iota(jnp.int32, (blk, blk), 0)
    col = lax.broadcasted_iota(jnp.int32, (blk, blk), 1)

    def score_tile(j, carry):
        start = pl.multiple_of(j * blk, blk)
        kc = ki_scr[pl.ds(start, blk), :]
        score = jnp.zeros((blk, blk), F32)
        for h in range(IDX_HEADS):
            lg = lax.dot_general(qi[:, h * IDX_HEAD_DIM:(h + 1) * IDX_HEAD_DIM], kc, NT_DIMS,
                                 preferred_element_type=F32)
            score = score + jnp.maximum(lg, 0.0) * wb_scr[h]
        bits = lax.bitcast_convert_type(score, jnp.int32)
        key = bits ^ ((bits >> 31) & jnp.int32(0x7FFFFFFF))
        key_scr[j] = jnp.where(start + col <= row, key, INT_MIN)
        return carry

    lax.fori_loop(0, i + 1, score_tile, 0)

    def make_search(n_tiles):
        def search(it, thr):
            trial = thr + lax.shift_left(jnp.int32(1), 31 - it)
            c = jnp.zeros((blk, 128), F32)
            for j in range(n_tiles):
                ge = jnp.where(key_scr[j] >= trial, 1.0, 0.0)
                for part in range(blk // 128):
                    c = c + ge[:, part * 128:(part + 1) * 128]
            return jnp.where(jnp.sum(c, axis=1, keepdims=True) >= k_sel, trial, thr)
        return search

    thr_scr[...] = jnp.full(thr_scr.shape, INT_MIN, jnp.int32)
    for jj in range(seq // blk):
        if (jj + 1) * blk > k_sel:
            @pl.when(i == jj)
            def _():
                t = lax.fori_loop(0, 32, make_search(jj + 1), jnp.full((blk, 1), INT_MIN, jnp.int32))
                thr_scr[...] = jnp.broadcast_to(t, thr_scr.shape)

    thr = thr_scr[...][:, 0:1]
    for jj in range(seq // blk):
        @pl.when(jj <= i)
        def _():
            keep = (jj * blk + col <= row) & (key_scr[jj] >= thr)
            o_ref[jj] = jnp.where(keep, 0.0, NEG).astype(o_ref.dtype)

        @pl.when(jj > i)
        def _():
            o_ref[jj] = jnp.full((blk, blk), NEG, o_ref.dtype)


def _indexer_mask(qi, kiw, idx_k_g, idx_k_b, batch, seq, k_sel, qi_col_block):
    blk = min(ATT_BLOCK, seq)
    nq = seq // blk
    qw = IDX_HEADS * IDX_HEAD_DIM
    kern = functools.partial(_indexer_kernel, blk=blk, seq=seq, k_sel=k_sel)
    return pl.pallas_call(
        kern,
        grid=(batch, nq),
        in_specs=[pl.BlockSpec((blk, qw), lambda b, i: (b * nq + i, qi_col_block)),
                  pl.BlockSpec((seq, 128), lambda b, i: (b, 0)),
                  pl.BlockSpec((blk, 128), lambda b, i: (b * nq + i, 0)),
                  pl.BlockSpec((1, IDX_HEAD_DIM), lambda b, i: (0, 0)),
                  pl.BlockSpec((1, IDX_HEAD_DIM), lambda b, i: (0, 0))],
        out_specs=pl.BlockSpec((None, nq, blk, blk), lambda b, i: (b, 0, i, 0)),
        out_shape=jax.ShapeDtypeStruct((batch, nq, seq, blk), BF16),
        scratch_shapes=[pltpu.VMEM((seq, IDX_HEAD_DIM), BF16), pltpu.VMEM((nq, blk, blk), jnp.int32),
                        pltpu.VMEM((IDX_HEADS, blk, blk), F32), pltpu.VMEM((blk, 128), jnp.int32)],
        compiler_params=_params(2, 48),
    )(qi, kiw, kiw, idx_k_g.reshape(1, -1), idx_k_b.reshape(1, -1))


def _dsa_attn_kernel(qs_ref, ckv_ref, g_ref, sel_ref, wuk_ref, wuv_ref, bias_ref, o_ref,
                     c_scr, ql_scr, sb_scr, m_scr, l_scr, acc_scr, *, blk):
    i = pl.program_id(1)
    d = DSA_HEAD_DIM
    scale = d ** -0.5

    @pl.when(i == 0)
    def _():
        x = ckv_ref[...]
        c = x * lax.rsqrt(jnp.mean(x * x, axis=1, keepdims=True) + LN_EPS) * g_ref[...]
        c_scr[...] = c.astype(BF16)

    qs = qs_ref[...]
    for h in range(DSA_HEADS):
        ql = jnp.dot(qs[:, h * d:(h + 1) * d], wuk_ref[h], preferred_element_type=F32)
        ql_scr[h * blk:(h + 1) * blk, :] = (ql * scale).astype(BF16)

    m_scr[...] = jnp.full(m_scr.shape, NEG, F32)
    l_scr[...] = jnp.zeros(l_scr.shape, F32)
    acc_scr[...] = jnp.zeros(acc_scr.shape, F32)

    def body(j, carry):
        start = pl.multiple_of(j * blk, blk)
        cc = c_scr[pl.ds(start, blk), :]
        s = lax.dot_general(ql_scr[...], cc, NT_DIMS, preferred_element_type=F32)
        selm = sel_ref[j].astype(F32)
        back = jnp.minimum(i - j, 2)
        for h in range(DSA_HEADS):
            sb_scr[h * blk:(h + 1) * blk, :] = bias_ref[h, back] + selm
        _online_softmax_step(s + sb_scr[...], cc, m_scr, l_scr, acc_scr)
        return carry

    lax.fori_loop(0, i + 1, body, 0)

    o_lat = (acc_scr[...] * _lane_repeat(1.0 / l_scr[...], DSA_KV_RANK // 128, axis=1)).astype(BF16)
    for h in range(DSA_HEADS):
        o_ref[:, h * d:(h + 1) * d] = jnp.dot(o_lat[h * blk:(h + 1) * blk], wuv_ref[h],
                                              preferred_element_type=F32).astype(o_ref.dtype)


def _dsa_attention(qkv, qs_col_block, ckv, ckv_col_block, kv_norm_g, sel, w_uk, w_uv, bias_tiles, batch, seq):
    blk = min(ATT_BLOCK, seq)
    nq = seq // blk
    hd = DSA_HEADS * DSA_HEAD_DIM
    r = DSA_KV_RANK
    rows = DSA_HEADS * blk
    kern = functools.partial(_dsa_attn_kernel, blk=blk)
    return pl.pallas_call(
        kern,
        grid=(batch, nq),
        in_specs=[pl.BlockSpec((blk, hd), lambda b, i: (b * nq + i, qs_col_block)),
                  pl.BlockSpec((seq, r), lambda b, i: (b, ckv_col_block)),
                  pl.BlockSpec((1, r), lambda b, i: (0, 0)),
                  pl.BlockSpec((None, nq, blk, blk), lambda b, i: (b, 0, i, 0)),
                  pl.BlockSpec((DSA_HEADS, DSA_HEAD_DIM, r), lambda b, i: (0, 0, 0)),
                  pl.BlockSpec((DSA_HEADS, r, DSA_HEAD_DIM), lambda b, i: (0, 0, 0)),
                  pl.BlockSpec((DSA_HEADS, 3, blk, blk), lambda b, i: (0, 0, 0, 0))],
        out_specs=pl.BlockSpec((blk, hd), lambda b, i: (b * nq + i, 0)),
        out_shape=jax.ShapeDtypeStruct((batch * seq, hd), BF16),
        scratch_shapes=[pltpu.VMEM((seq, r), BF16), pltpu.VMEM((rows, r), BF16), pltpu.VMEM((rows, blk), F32),
                        pltpu.VMEM((rows, 128), F32), pltpu.VMEM((rows, 128), F32), pltpu.VMEM((rows, r), F32)],
        compiler_params=_params(2, 56),
    )(qkv, ckv, kv_norm_g.reshape(1, r), sel, w_uk, w_uv, bias_tiles)


def _merge_kernel(x_ref, od_ref, os_ref, wda_ref, wdb_ref, wsa_ref, wsb_ref, wpd_ref, wps_ref, o_ref,
                  wgd_scr, wgs_scr, *, lane_off):
    tn = o_ref.shape[1]

    @pl.when(pl.program_id(1) == 0)
    def _():
        wd = jnp.concatenate([wda_ref[...], wdb_ref[...]], axis=1)
        ws = jnp.concatenate([wsa_ref[...], wsb_ref[...]], axis=1)
        wgd_scr[...] = wd[:, lane_off:lane_off + tn].astype(BF16)
        wgs_scr[...] = ws[:, lane_off:lane_off + tn].astype(BF16)

    x = x_ref[...]
    gd = jnp.dot(x, wgd_scr[...], preferred_element_type=F32)
    gs = jnp.dot(x, wgs_scr[...], preferred_element_type=F32)
    pd = jnp.dot(od_ref[...], wpd_ref[...], preferred_element_type=F32)
    ps = jnp.dot(os_ref[...], wps_ref[...], preferred_element_type=F32)
    o_ref[...] = (_sigmoid(gd) * pd + _sigmoid(gs) * ps).astype(o_ref.dtype)


def _gated_merge(xb, o_d, o_s, w_in, c_gd, c_gs, wpd, wps):
    m, dm = xb.shape
    tm = min(512, m)
    tn = 512
    kd, ks = o_d.shape[1], o_s.shape[1]
    lane_off = c_gd % 128
    a_d, a_s = c_gd - lane_off, c_gs - lane_off
    assert c_gs % 128 == lane_off and a_d % tn == 0 and a_s % tn == 0
    kern = functools.partial(_merge_kernel, lane_off=lane_off)
    return pl.pallas_call(
        kern,
        grid=(dm // tn, m // tm),
        in_specs=[pl.BlockSpec((tm, dm), lambda j, i: (i, 0)),
                  pl.BlockSpec((tm, kd), lambda j, i: (i, 0)),
                  pl.BlockSpec((tm, ks), lambda j, i: (i, 0)),
                  pl.BlockSpec((dm, tn), lambda j, i: (0, a_d // tn + j)),
                  pl.BlockSpec((dm, 128), lambda j, i: (0, (a_d + tn * (j + 1)) // 128)),
                  pl.BlockSpec((dm, tn), lambda j, i: (0, a_s // tn + j)),
                  pl.BlockSpec((dm, 128), lambda j, i: (0, (a_s + tn * (j + 1)) // 128)),
                  pl.BlockSpec((kd, tn), lambda j, i: (0, j)),
                  pl.BlockSpec((ks, tn), lambda j, i: (0, j))],
        out_specs=pl.BlockSpec((tm, tn), lambda j, i: (i, j)),
        out_shape=jax.ShapeDtypeStruct((m, dm), BF16),
        scratch_shapes=[pltpu.VMEM((dm, tn), BF16), pltpu.VMEM((dm, tn), BF16)],
        compiler_params=_params(2, 56),
    )(xb, o_d, o_s, w_in, w_in, w_in, w_in, wpd, wps)


def _mm_res_ln_kernel(a_ref, w_ref, res_ref, g_ref, b_ref, o_ref, ob_ref):
    y = jnp.dot(a_ref[...], w_ref[...], preferred_element_type=F32)
    zn = _layer_norm(DN_ALPHA * res_ref[...] + y, g_ref[...], b_ref[...])
    o_ref[...] = zn
    ob_ref[...] = zn.astype(BF16)


def _matmul_res_ln(a, w, res, g, b):
    m, k = a.shape
    n = w.shape[1]
    tm = min(256, m)
    return pl.pallas_call(
        _mm_res_ln_kernel,
        grid=(m // tm,),
        in_specs=[pl.BlockSpec((tm, k), lambda i: (i, 0)),
                  pl.BlockSpec((k, n), lambda i: (0, 0)),
                  pl.BlockSpec((tm, n), lambda i: (i, 0)),
                  pl.BlockSpec((1, n), lambda i: (0, 0)),
                  pl.BlockSpec((1, n), lambda i: (0, 0))],
        out_specs=[pl.BlockSpec((tm, n), lambda i: (i, 0)), pl.BlockSpec((tm, n), lambda i: (i, 0))],
        out_shape=[jax.ShapeDtypeStruct((m, n), F32), jax.ShapeDtypeStruct((m, n), BF16)],
        compiler_params=_params(1, 40),
    )(a, w, res, g.reshape(1, n), b.reshape(1, n))


def _pack_bf16_pairs(zb):
    half = zb.shape[1] // 2
    bits = lax.bitcast_convert_type(zb.astype(F32), jnp.uint32)
    return (bits[:, :half] & jnp.uint32(0xFFFF0000)) | (bits[:, half:] >> 16)


def _unpack_bf16_pairs(w):
    hi = lax.bitcast_convert_type(w & jnp.uint32(0xFFFF0000), F32).astype(BF16)
    lo = lax.bitcast_convert_type(w << 16, F32).astype(BF16)
    return hi, lo


def _xattn_kernel(hb_ref, h_ref, wq_ref, kv_ref, wo_ref, g_ref, b_ref, o_ref, ob_ref, op_ref):
    d = XATTN_HEAD_DIM
    hd = XATTN_HEADS * d
    scale = d ** -0.5
    qb = jnp.dot(hb_ref[...], wq_ref[...], preferred_element_type=F32).astype(BF16)
    kv = kv_ref[...]
    outs = []
    for h in range(XATTN_HEADS):
        s = lax.dot_general(qb[:, h * d:(h + 1) * d], kv[:, h * d:(h + 1) * d], NT_DIMS,
                            preferred_element_type=F32) * scale
        p = jnp.exp(s - jnp.max(s, axis=1, keepdims=True))
        l = jnp.sum(p, axis=1, keepdims=True)
        oh = jnp.dot(p.astype(BF16), kv[:, hd + h * d:hd + (h + 1) * d], preferred_element_type=F32)
        outs.append((oh / l).astype(BF16))
    o = jnp.concatenate(outs, axis=1)
    y = jnp.dot(o, wo_ref[...], preferred_element_type=F32)
    zn = _layer_norm(DN_ALPHA * h_ref[...] + y, g_ref[...], b_ref[...])
    zb = zn.astype(BF16)
    o_ref[...] = zn
    ob_ref[...] = zb
    op_ref[...] = _pack_bf16_pairs(zb)


def _cross_attention(hb, h, wq, kv, wo, g, b, batch, seq, n_mem):
    m, dm = h.shape
    tm = min(256, seq)
    nq = seq // tm
    hd = XATTN_HEADS * XATTN_HEAD_DIM
    return pl.pallas_call(
        _xattn_kernel,
        grid=(batch, nq),
        in_specs=[pl.BlockSpec((tm, dm), lambda bb, i: (bb * nq + i, 0)),
                  pl.BlockSpec((tm, dm), lambda bb, i: (bb * nq + i, 0)),
                  pl.BlockSpec((dm, hd), lambda bb, i: (0, 0)),
                  pl.BlockSpec((n_mem, 2 * hd), lambda bb, i: (bb, 0)),
                  pl.BlockSpec((hd, dm), lambda bb, i: (0, 0)),
                  pl.BlockSpec((1, dm), lambda bb, i: (0, 0)),
                  pl.BlockSpec((1, dm), lambda bb, i: (0, 0))],
        out_specs=[pl.BlockSpec((tm, dm), lambda bb, i: (bb * nq + i, 0)),
                   pl.BlockSpec((tm, dm), lambda bb, i: (bb * nq + i, 0)),
                   pl.BlockSpec((tm, dm // 2), lambda bb, i: (bb * nq + i, 0))],
        out_shape=[jax.ShapeDtypeStruct((m, dm), F32), jax.ShapeDtypeStruct((m, dm), BF16),
                   jax.ShapeDtypeStruct((m, dm // 2), jnp.uint32)],
        compiler_params=_params(2, 40),
    )(hb, h, wq, kv, wo, g.reshape(1, dm), b.reshape(1, dm))


def _split_bf16(x):
    hi = x.astype(BF16)
    return hi, (x - hi.astype(F32)).astype(BF16)


def _router_kernel(h_ref, rwt_ref, rb_ref, idx_ref, gate_ref, rank_ref, cnt_ref, cnt_scr):
    tm = h_ref.shape[0]
    per = N_EXPERTS // N_GROUPS
    h_hi, h_lo = _split_bf16(h_ref[...])
    w_hi, w_lo = _split_bf16(rwt_ref[...])
    logits = (lax.dot_general(w_hi, h_hi, NT_DIMS, preferred_element_type=F32)
              + lax.dot_general(w_hi, h_lo, NT_DIMS, preferred_element_type=F32)
              + lax.dot_general(w_lo, h_hi, NT_DIMS, preferred_element_type=F32))
    scores = _sigmoid(logits)
    sel = scores + rb_ref[...]

    sel3 = sel.reshape(N_GROUPS, per, tm)
    r_iota = lax.broadcasted_iota(jnp.int32, sel3.shape, 1).astype(F32)
    m1 = jnp.max(sel3, axis=1, keepdims=True)
    first = jnp.min(jnp.where(sel3 == m1, r_iota, float(per)), axis=1, keepdims=True)
    m2 = jnp.max(jnp.where(r_iota == first, -jnp.inf, sel3), axis=1, keepdims=True)
    gscore = (m1 + m2).reshape(N_GROUPS, tm)

    g_iota = lax.broadcasted_iota(jnp.int32, gscore.shape, 0).astype(F32)
    keep = jnp.zeros(gscore.shape, F32)
    cur = gscore
    for _ in range(TOPK_GROUPS):
        m = jnp.max(cur, axis=0, keepdims=True)
        first = jnp.min(jnp.where(cur == m, g_iota, float(N_GROUPS)), axis=0, keepdims=True)
        pick = g_iota == first
        keep = jnp.where(pick, 1.0, keep)
        cur = jnp.where(pick, -jnp.inf, cur)
    keep3 = jnp.broadcast_to(keep.reshape(N_GROUPS, 1, tm), sel3.shape)
    cur = jnp.where(keep3 > 0.5, sel3, -jnp.inf).reshape(N_EXPERTS, tm)

    e_iota = lax.broadcasted_iota(jnp.int32, cur.shape, 0).astype(F32)
    gsum = jnp.zeros((1, tm), F32)
    gates, picks = [], []
    for k in range(TOP_K):
        m = jnp.max(cur, axis=0, keepdims=True)
        first = jnp.min(jnp.where(cur == m, e_iota, float(N_EXPERTS)), axis=0, keepdims=True)
        pick = e_iota == first
        gk = jnp.sum(jnp.where(pick, scores, 0.0), axis=0, keepdims=True)
        idx_ref[k:k + 1, :] = first.astype(jnp.int32)
        gates.append(gk)
        picks.append(pick)
        gsum = gsum + gk
        cur = jnp.where(pick, -jnp.inf, cur)
    for k in range(TOP_K):
        gate_ref[k:k + 1, :] = gates[k] / gsum * ROUTED_SCALE

    @pl.when(pl.program_id(0) == 0)
    def _():
        cnt_scr[...] = jnp.zeros(cnt_scr.shape, F32)

    mask = jnp.zeros(cur.shape, F32)
    for k in range(TOP_K):
        mask = jnp.where(picks[k], 1.0, mask)
    upper = (lax.broadcasted_iota(jnp.int32, (tm, tm), 0) <= lax.broadcasted_iota(jnp.int32, (tm, tm), 1))
    cum = jnp.dot(mask.astype(BF16), jnp.where(upper, 1.0, 0.0).astype(BF16), preferred_element_type=F32)
    before = cnt_scr[...][:, 0:1] + cum - mask
    for k in range(TOP_K):
        rank_ref[k:k + 1, :] = jnp.sum(jnp.where(picks[k], before, 0.0), axis=0, keepdims=True).astype(jnp.int32)
    cnt_scr[...] = cnt_scr[...] + jnp.sum(mask, axis=1, keepdims=True)
    cnt_ref[...] = cnt_scr[...].astype(jnp.int32)


def _router(h, router_w, router_bias):
    m, dm = h.shape
    tm = min(512, m)
    return pl.pallas_call(
        _router_kernel,
        grid=(m // tm,),
        in_specs=[pl.BlockSpec((tm, dm), lambda i: (i, 0)),
                  pl.BlockSpec((N_EXPERTS, dm), lambda i: (0, 0)),
                  pl.BlockSpec((N_EXPERTS, 1), lambda i: (0, 0))],
        out_specs=[pl.BlockSpec((TOP_K, tm), lambda i: (0, i)), pl.BlockSpec((TOP_K, tm), lambda i: (0, i)),
                   pl.BlockSpec((TOP_K, tm), lambda i: (0, i)), pl.BlockSpec((N_EXPERTS, 128), lambda i: (0, 0))],
        out_shape=[jax.ShapeDtypeStruct((TOP_K, m), jnp.int32), jax.ShapeDtypeStruct((TOP_K, m), F32),
                   jax.ShapeDtypeStruct((TOP_K, m), jnp.int32), jax.ShapeDtypeStruct((N_EXPERTS, 128), jnp.int32)],
        scratch_shapes=[pltpu.VMEM((N_EXPERTS, 128), F32)],
        compiler_params=_params(1, 32),
    )(h, router_w.T, router_bias.reshape(N_EXPERTS, 1))


def _dest_kernel(idx_ref, rank_ref, ps_ref, dest_ref):
    tm = idx_ref.shape[1]
    e_iota = lax.broadcasted_iota(jnp.int32, (N_EXPERTS, tm), 0)
    ps = ps_ref[...][:, 0:1]
    for k in range(TOP_K):
        base = jnp.sum(jnp.where(e_iota == idx_ref[k:k + 1, :], ps, 0), axis=0, keepdims=True)
        dest_ref[k:k + 1, :] = rank_ref[k:k + 1, :] + base


def _dest_rows(idx_t, rank_t, pad_start):
    m = idx_t.shape[1]
    tm = min(2048, m)
    ps = jnp.broadcast_to(pad_start.astype(F32).reshape(N_EXPERTS, 1), (N_EXPERTS, 128))
    out = pl.pallas_call(
        _dest_kernel,
        grid=(m // tm,),
        in_specs=[pl.BlockSpec((TOP_K, tm), lambda i: (0, i)), pl.BlockSpec((TOP_K, tm), lambda i: (0, i)),
                  pl.BlockSpec((N_EXPERTS, 128), lambda i: (0, 0))],
        out_specs=pl.BlockSpec((TOP_K, tm), lambda i: (0, i)),
        out_shape=jax.ShapeDtypeStruct((TOP_K, m), F32),
        compiler_params=_params(1, 32),
    )(idx_t, rank_t.astype(F32), ps)
    return out.astype(jnp.int32)


def _dispatch_kernel(zflag_ref, dest_ref, x_ref, xs_ref, zbuf, sem, zsem):
    tm = x_ref.shape[0]
    n_blocks = xs_ref.shape[0] // MOE_ROWS

    @pl.when(pl.program_id(0) == 0)
    def _():
        zbuf[...] = jnp.zeros(zbuf.shape, zbuf.dtype)

        def zero_copy(b):
            return pltpu.make_async_copy(zbuf, xs_ref.at[pl.ds(pl.multiple_of(b * MOE_ROWS, MOE_ROWS), MOE_ROWS), :],
                                         zsem)

        def zstart(b, carry):
            @pl.when(zflag_ref[b] != 0)
            def _():
                zero_copy(b).start()
            return carry

        def zwait(b, carry):
            @pl.when(zflag_ref[b] != 0)
            def _():
                zero_copy(b).wait()
            return carry

        lax.fori_loop(0, n_blocks, zstart, 0)
        lax.fori_loop(0, n_blocks, zwait, 0)

    def copy(t, k):
        return pltpu.make_async_copy(x_ref.at[pl.ds(t, 1), :], xs_ref.at[pl.ds(dest_ref[0, k, t], 1), :], sem)

    def start(t, carry):
        for k in range(TOP_K):
            copy(t, k).start()
        return carry

    lax.fori_loop(0, tm, start, 0)
    n_rows = TOP_K * tm
    pltpu.make_async_copy(xs_ref.at[pl.ds(0, n_rows), :], xs_ref.at[pl.ds(n_rows, n_rows), :], sem).wait()


def _dispatch(xp, dest, zero_block, rows):
    m, wd = xp.shape
    tm = min(512, m)
    dest3 = dest.reshape(TOP_K, m // tm, tm).transpose(1, 0, 2)
    grid_spec = pltpu.PrefetchScalarGridSpec(
        num_scalar_prefetch=1,
        grid=(m // tm,),
        in_specs=[pl.BlockSpec((1, TOP_K, tm), lambda i, zf: (i, 0, 0), memory_space=pltpu.SMEM),
                  pl.BlockSpec((tm, wd), lambda i, zf: (i, 0))],
        out_specs=pl.BlockSpec(memory_space=pl.ANY),
        scratch_shapes=[pltpu.VMEM((MOE_ROWS, wd), xp.dtype), pltpu.SemaphoreType.DMA(()),
                        pltpu.SemaphoreType.DMA(())],
    )
    return pl.pallas_call(
        _dispatch_kernel,
        grid_spec=grid_spec,
        out_shape=jax.ShapeDtypeStruct((rows, wd), xp.dtype),
        compiler_params=_params(1, 32),
    )(zero_block, dest3, xp)


def _expert_kernel(be_ref, slot_ref, next_ref, nu_ref, x_ref, w1_ref, w3_ref, w2_ref, o_ref,
                   w1f, w3f, w2f, w1b, w3b, w2b, sems):
    i = pl.program_id(0)
    used = i < nu_ref[0]
    e = be_ref[i]
    first = jnp.logical_and(used, jnp.logical_or(i == 0, e != be_ref[jnp.maximum(i - 1, 0)]))

    def copies(expert, s):
        return (pltpu.make_async_copy(w1_ref.at[expert], w1f.at[s], sems.at[s, 0]),
                pltpu.make_async_copy(w3_ref.at[expert], w3f.at[s], sems.at[s, 1]),
                pltpu.make_async_copy(w2_ref.at[expert], w2f.at[s], sems.at[s, 2]))

    @pl.when(jnp.logical_and(used, i == 0))
    def _():
        for c in copies(e, slot_ref[0]):
            c.start(priority=1)

    @pl.when(first)
    def _():
        s = slot_ref[i]
        for c in copies(e, s):
            c.wait()
        w1b[...] = w1f[s].astype(BF16)
        w3b[...] = w3f[s].astype(BF16)
        w2b[...] = w2f[s].astype(BF16)

        @pl.when(next_ref[i] >= 0)
        def _():
            for c in copies(next_ref[i], 1 - s):
                c.start(priority=1)

    @pl.when(used)
    def _():
        xh, xl = _unpack_bf16_pairs(x_ref[...])
        half = xh.shape[1]
        a = (jnp.dot(xh, w1b[0:half, :], preferred_element_type=F32)
             + jnp.dot(xl, w1b[half:, :], preferred_element_type=F32))
        b = (jnp.dot(xh, w3b[0:half, :], preferred_element_type=F32)
             + jnp.dot(xl, w3b[half:, :], preferred_element_type=F32))
        hmid = (a * _sigmoid(a) * b).astype(BF16)
        o_ref[...] = _pack_bf16_pairs(jnp.dot(hmid, w2b[...], preferred_element_type=F32).astype(BF16))

    @pl.when(i >= nu_ref[0])
    def _():
        o_ref[...] = jnp.zeros(o_ref.shape, o_ref.dtype)


def _expert_blocks(x_sorted, block_e, block_slot, block_next, n_used, w1, w3, w2):
    rows, wd = x_sorted.shape
    n_blocks = rows // MOE_ROWS
    dm, de = w1.shape[1], w1.shape[2]
    grid_spec = pltpu.PrefetchScalarGridSpec(
        num_scalar_prefetch=4,
        grid=(n_blocks,),
        in_specs=[pl.BlockSpec((MOE_ROWS, wd), lambda i, *_: (i, 0)),
                  pl.BlockSpec(memory_space=pl.ANY),
                  pl.BlockSpec(memory_space=pl.ANY),
                  pl.BlockSpec(memory_space=pl.ANY)],
        out_specs=pl.BlockSpec((MOE_ROWS, wd), lambda i, *_: (i, 0)),
        scratch_shapes=[pltpu.VMEM((2, dm, de), F32), pltpu.VMEM((2, dm, de), F32), pltpu.VMEM((2, de, dm), F32),
                        pltpu.VMEM((dm, de), BF16), pltpu.VMEM((dm, de), BF16), pltpu.VMEM((de, dm), BF16),
                        pltpu.SemaphoreType.DMA((2, 3))],
    )
    return pl.pallas_call(
        _expert_kernel,
        grid_spec=grid_spec,
        out_shape=jax.ShapeDtypeStruct((rows, wd), jnp.uint32),
        compiler_params=_params(1, 56),
    )(block_e, block_slot, block_next, n_used, x_sorted, w1, w3, w2)


def _shared_kernel(x_ref, w1_ref, w3_ref, w2_ref, o_ref):
    x = x_ref[...]
    a = jnp.dot(x, w1_ref[...], preferred_element_type=F32)
    b = jnp.dot(x, w3_ref[...], preferred_element_type=F32)
    hmid = (a * _sigmoid(a) * b).astype(BF16)
    o_ref[...] = jnp.dot(hmid, w2_ref[...], preferred_element_type=F32)


def _shared_expert(xb, w1, w3, w2):
    m, dm = xb.shape
    de = w1.shape[1]
    tm = min(512, m)
    return pl.pallas_call(
        _shared_kernel,
        grid=(m // tm,),
        in_specs=[pl.BlockSpec((tm, dm), lambda i: (i, 0)),
                  pl.BlockSpec((dm, de), lambda i: (0, 0)),
                  pl.BlockSpec((dm, de), lambda i: (0, 0)),
                  pl.BlockSpec((de, dm), lambda i: (0, 0))],
        out_specs=pl.BlockSpec((tm, dm), lambda i: (i, 0)),
        out_shape=jax.ShapeDtypeStruct((m, dm), F32),
        compiler_params=_params(1, 32),
    )(xb, w1, w3, w2)


def _combine_kernel(dcur_ref, dnext_ref, h_ref, sh_ref, gate_ref, g_ref, b_ref, y_ref, o_ref, ybuf, sems):
    i = pl.program_id(0)
    n = pl.num_programs(0)
    tm = h_ref.shape[0]
    slot = lax.rem(i, 2)

    def copy(d_ref, s, t, k):
        return pltpu.make_async_copy(y_ref.at[pl.ds(d_ref[0, k, t], 1), :],
                                     ybuf.at[s, pl.ds(k * tm + t, 1), :], sems.at[s])

    def start_tile(d_ref, s):
        def body(t, carry):
            for k in range(TOP_K):
                copy(d_ref, s, t, k).start()
            return carry
        lax.fori_loop(0, tm, body, 0)

    @pl.when(i == 0)
    def _():
        start_tile(dcur_ref, 0)

    for s in range(2):
        @pl.when(jnp.logical_and(i + 1 < n, slot == 1 - s))
        def _():
            start_tile(dnext_ref, s)

    pltpu.make_async_copy(y_ref.at[pl.ds(0, TOP_K * tm), :], ybuf.at[slot], sems.at[slot]).wait()

    gate = gate_ref[...]
    half = ybuf.shape[2]
    yh = jnp.zeros((tm, half), F32)
    yl = jnp.zeros((tm, half), F32)
    for k in range(TOP_K):
        w = ybuf[slot, k * tm:(k + 1) * tm, :]
        gk = gate[:, k:k + 1]
        yh = yh + lax.bitcast_convert_type(w & jnp.uint32(0xFFFF0000), F32) * gk
        yl = yl + lax.bitcast_convert_type(w << 16, F32) * gk
    y = sh_ref[...] + jnp.concatenate([yh, yl], axis=1)
    o_ref[...] = _layer_norm(DN_ALPHA * h_ref[...] + y, g_ref[...], b_ref[...])


def _combine_ln(h, shared, y_sorted, dest, gate, g, b):
    m, dm = h.shape
    tm = min(128, m)
    n = m // tm
    dest3 = dest.reshape(TOP_K, n, tm).transpose(1, 0, 2)
    return pl.pallas_call(
        _combine_kernel,
        grid=(n,),
        in_specs=[pl.BlockSpec((1, TOP_K, tm), lambda i: (i, 0, 0), memory_space=pltpu.SMEM),
                  pl.BlockSpec((1, TOP_K, tm), lambda i: (jnp.minimum(i + 1, n - 1), 0, 0), memory_space=pltpu.SMEM),
                  pl.BlockSpec((tm, dm), lambda i: (i, 0)),
                  pl.BlockSpec((tm, dm), lambda i: (i, 0)),
                  pl.BlockSpec((tm, TOP_K), lambda i: (i, 0)),
                  pl.BlockSpec((1, dm), lambda i: (0, 0)),
                  pl.BlockSpec((1, dm), lambda i: (0, 0)),
                  pl.BlockSpec(memory_space=pl.ANY)],
        out_specs=pl.BlockSpec((tm, dm), lambda i: (i, 0)),
        out_shape=jax.ShapeDtypeStruct((m, dm), F32),
        scratch_shapes=[pltpu.VMEM((2, TOP_K * tm, dm // 2), jnp.uint32), pltpu.SemaphoreType.DMA((2,))],
        compiler_params=_params(1, 40),
    )(dest3, dest3, h, shared, gate, g.reshape(1, dm), b.reshape(1, dm), y_sorted)


def _moe_sublayer(h, hb, hp, router_w, router_bias, w1, w3, w2, sw1, sw3, sw2, g, b):
    n_tok, dm = h.shape
    idx_t, gate_t, rank_t, counts = _router(h, router_w, router_bias)
    counts = counts[:, 0]
    padded = (counts + MOE_ROWS - 1) // MOE_ROWS * MOE_ROWS
    pad_end = jnp.cumsum(padded)
    n_blocks = (n_tok * TOP_K + N_EXPERTS * (MOE_ROWS - 1) + MOE_ROWS - 1) // MOE_ROWS
    block_start = jnp.arange(n_blocks, dtype=jnp.int32) * MOE_ROWS
    block_e = jnp.minimum(jnp.sum((pad_end[None, :] <= block_start[:, None]).astype(jnp.int32), axis=1),
                          N_EXPERTS - 1)
    n_used = (pad_end[-1] // MOE_ROWS).astype(jnp.int32).reshape(1)
    e_ids = jnp.arange(N_EXPERTS, dtype=jnp.int32)
    has_rows = counts > 0
    ordinal = jnp.cumsum(has_rows.astype(jnp.int32)) - 1
    later = jnp.where(has_rows[None, :] & (e_ids[None, :] > e_ids[:, None]), e_ids[None, :], N_EXPERTS)
    next_e = jnp.min(later, axis=1)
    next_e = jnp.where(next_e < N_EXPERTS, next_e, -1).astype(jnp.int32)
    block_slot = (ordinal[block_e] % 2).astype(jnp.int32)
    block_next = next_e[block_e]
    dest = _dest_rows(idx_t, rank_t, pad_end - padded)
    block_end = block_start + MOE_ROWS
    last_of_expert = jnp.any(pad_end[None, :] == block_end[:, None], axis=1)
    zero_block = (last_of_expert | (block_start >= pad_end[-1])).astype(jnp.int32)
    x_sorted = _dispatch(hp, dest, zero_block, n_blocks * MOE_ROWS)
    y_sorted = _expert_blocks(x_sorted, block_e, block_slot, block_next, n_used, w1, w3, w2)
    shared = _shared_expert(hb, sw1.astype(BF16), sw3.astype(BF16), sw2.astype(BF16))
    return _combine_ln(h, shared, y_sorted, dest, gate_t.T, g, b)


def kernel(x, mem, positions, rel_bias_table, w_in, diff_lambda, diff_subln, idx_k_g, idx_k_b, kv_norm_g, w_uk, w_uv, w_proj_diff, w_proj_dsa, w_mix_out, ln1_g, ln1_b, xattn_wq, xattn_wkv, xattn_wo, ln2_g, ln2_b, router_w, router_bias, exp_w1, exp_w3, exp_w2, sh_w1, sh_w3, sh_w2, ln3_g, ln3_b):
    del positions
    batch, seq, dm = x.shape
    n_mem = mem.shape[1]
    n_tok = batch * seq
    lambda_init = 0.8 - 0.6 * math.exp(-0.3 * 0)
    blk = min(ATT_BLOCK, seq)

    dq = DIFF_HEADS * DIFF_V_DIM
    c_qs = 3 * dq
    c_ckv = c_qs + DSA_HEADS * DSA_HEAD_DIM
    c_qi = c_ckv + DSA_KV_RANK
    c_ki = c_qi + IDX_HEADS * IDX_HEAD_DIM
    c_gd = c_ki + IDX_HEAD_DIM + IDX_HEADS
    c_gs = c_gd + dm

    xf = x.reshape(n_tok, dm)
    xb = xf.astype(BF16)
    w_in0 = w_in[0]
    qkv = _matmul(xb, w_in0, 0, c_ckv, BF16, 1024, 1024)
    ckv = _matmul(xb, w_in0, c_ckv, DSA_KV_RANK, F32, 1024, 512)
    qi = _matmul(xb, w_in0, c_qi, c_ki - c_qi, BF16, 1024, 512)
    w_tail = jnp.pad(w_in0[:, c_ki:c_gd], ((0, 0), (0, 128 - (c_gd - c_ki))))
    kiw = _matmul(xb, w_tail, 0, 128, F32, 1024, 128)

    dsa_tiles = _rel_bias_tiles(rel_bias_table[:, DIFF_HEADS:], blk)
    diff_tiles = _rel_bias_tiles(rel_bias_table[:, :DIFF_HEADS], min(DIFF_BLOCK, seq))
    o_d = _diff_attention(qkv, diff_tiles, diff_lambda[0], diff_subln[0], batch, seq, lambda_init)

    k_sel = min(DSA_TOPK_MAX, seq // 4)
    sel = _indexer_mask(qi, kiw, idx_k_g[0], idx_k_b[0], batch, seq, k_sel, qi_col_block=0)
    o_s = _dsa_attention(qkv, c_qs // (DSA_HEADS * DSA_HEAD_DIM), ckv, 0, kv_norm_g[0], sel,
                         w_uk[0].astype(BF16), w_uv[0].astype(BF16), dsa_tiles, batch, seq)

    merged = _gated_merge(xb, o_d, o_s, w_in0, c_gd, c_gs,
                          w_proj_diff[0].astype(BF16), w_proj_dsa[0].astype(BF16))
    h1, h1b = _matmul_res_ln(merged, w_mix_out[0].astype(BF16), xf, ln1_g[0], ln1_b[0])

    memb = mem.reshape(batch * n_mem, dm).astype(BF16)
    kv = _matmul(memb, xattn_wkv[0], 0, 2 * XATTN_HEADS * XATTN_HEAD_DIM, BF16, 1024, 512)
    h2, h2b, h2p = _cross_attention(h1b, h1, xattn_wq[0].astype(BF16), kv, xattn_wo[0].astype(BF16),
                                    ln2_g[0], ln2_b[0], batch, seq, n_mem)

    out = _moe_sublayer(h2, h2b, h2p, router_w[0], router_bias[0], exp_w1[0], exp_w3[0], exp_w2[0],
                        sh_w1[0], sh_w3[0], sh_w2[0], ln3_g[0], ln3_b[0])
    return out.reshape(batch, seq, dm)
```
